```python
import jax, jax.numpy as jnp
from jax import lax
import numpy as np

D_MODEL = 4096
BATCH = 2
SEQ = 4096
DEPTH = 2

D_MIX = D_MODEL
HEAD_DIM = 128
A_GROUPS = 8
A_CH = 128
A_WIDTH = A_GROUPS * A_CH
CHUNK = 128
B_HEADS = 12
B_WIDTH = B_HEADS * HEAD_DIM
C_HEADS = 12
C_NOPE = 128
C_ROPE = 64
C_VDIM = 128
C_WIDTH = C_HEADS * C_VDIM
Q_LORA = 768
KV_LORA = 512
ROPE_THETA = 10000.0
D_IN = 3 * A_WIDTH + 4 * B_WIDTH + Q_LORA + KV_LORA + C_ROPE + C_WIDTH
EPS = 1e-6
Q_BLOCK = 128

kernel_name = "hybrid_gmlp_stickbreak_mla_parallel_heads"


def rmsnorm(x, g):
    xf = x.astype(jnp.float32)
    xf = xf * lax.rsqrt(jnp.mean(xf * xf, axis=-1, keepdims=True) + EPS)
    return (xf * g.astype(jnp.float32)).astype(x.dtype)


def gated_norm(y, g, z):
    return rmsnorm(y, g) * jax.nn.silu(z)


def rope(x, cos, sin):
    half = x.shape[-1] // 2
    x1, x2 = x[..., :half], x[..., half:]
    out = jnp.concatenate([x1 * cos - x2 * sin, x2 * cos + x1 * sin], axis=-1)
    return out.astype(x.dtype)


def chunked_gmlp(u, v, g_v, w_s, b_s):
    bn, s, _ = u.shape
    u = jax.nn.gelu(u)
    v = rmsnorm(jax.nn.gelu(v).reshape(bn, s, A_GROUPS, A_CH), g_v)
    v = v.reshape(bn, s // CHUNK, CHUNK, A_GROUPS, A_CH)
    causal = jnp.tril(jnp.ones((CHUNK, CHUNK), dtype=bool))
    w = jnp.where(causal[None], w_s, 0.0).astype(v.dtype)
    sv = jnp.einsum('gts,bcsgd->bctgd', w, v) + b_s.T[None, None, :, :, None]
    return u * sv.reshape(bn, s, A_WIDTH)


def stick_breaking(q, k, v):
    bn, s, h, d = q.shape
    nb = s // Q_BLOCK
    scale = d ** -0.5
    kpos = jnp.arange(s)
    qb = q.reshape(bn, nb, Q_BLOCK, h, d).transpose(1, 0, 2, 3, 4)

    def block(args):
        qi, bi = args
        z = jnp.einsum('bthd,bshd->bhts', qi, k).astype(jnp.float32) * scale
        qpos = bi * Q_BLOCK + jnp.arange(Q_BLOCK)
        strict = kpos[None, :] < qpos[:, None]
        log_keep = jnp.where(strict, jax.nn.log_sigmoid(-z), 0.0)
        after = lax.cumsum(log_keep, axis=3, reverse=True) - log_keep
        a = jnp.where(strict, jnp.exp(jax.nn.log_sigmoid(z) + after), 0.0)
        return jnp.einsum('bhts,bshd->bthd', a.astype(v.dtype), v)

    out = lax.map(block, (qb, jnp.arange(nb)))
    return out.transpose(1, 0, 2, 3, 4).reshape(bn, s, h * d)


def mla(c_q, c_kv, k_rope, cos, sin, g_q, g_kv, w_uq, w_ukv):
    bn, s, _ = c_q.shape
    q = jnp.einsum('bsr,rn->bsn', rmsnorm(c_q, g_q), w_uq).reshape(bn, s, C_HEADS, C_NOPE + C_ROPE)
    q_nope = q[..., :C_NOPE]
    q_rope = rope(q[..., C_NOPE:], cos, sin)
    kv = jnp.einsum('bsr,rn->bsn', rmsnorm(c_kv, g_kv), w_ukv).reshape(bn, s, C_HEADS, C_NOPE + C_VDIM)
    k_nope, v = kv[..., :C_NOPE], kv[..., C_NOPE:]
    k_r = rope(k_rope, cos[:, :, 0], sin[:, :, 0])
    scale = (C_NOPE + C_ROPE) ** -0.5
    nb = s // Q_BLOCK
    kpos = jnp.arange(s)
    qn_b = q_nope.reshape(bn, nb, Q_BLOCK, C_HEADS, C_NOPE).transpose(1, 0, 2, 3, 4)
    qr_b = q_rope.reshape(bn, nb, Q_BLOCK, C_HEADS, C_ROPE).transpose(1, 0, 2, 3, 4)

    def block(args):
        qn, qr, bi = args
        z = (jnp.einsum('bthd,bshd->bhts', qn, k_nope)
             + jnp.einsum('bthr,bsr->bhts', qr, k_r)).astype(jnp.float32) * scale
        qpos = bi * Q_BLOCK + jnp.arange(Q_BLOCK)
        causal = kpos[None, :] <= qpos[:, None]
        p = jax.nn.softmax(jnp.where(causal, z, -jnp.inf), axis=-1)
        return jnp.einsum('bhts,bshd->bthd', p.astype(v.dtype), v)

    out = lax.map(block, (qn_b, qr_b, jnp.arange(nb)))
    return out.transpose(1, 0, 2, 3, 4).reshape(bn, s, C_WIDTH)


def hybrid_layer(x, cos, sin, g_pre, w_in, a_g_v, a_w_s, a_b_s,
                 c_g_q, c_g_kv, c_w_uq, c_w_ukv, g_out, w_out):
    bn, s, _ = x.shape
    h = rmsnorm(x, g_pre)
    proj = jnp.einsum('bsd,dn->bsn', h, w_in)
    sizes = [A_WIDTH, A_WIDTH, A_WIDTH, B_WIDTH, B_WIDTH, B_WIDTH, B_WIDTH,
             Q_LORA, KV_LORA, C_ROPE, C_WIDTH]
    offsets, acc = [], 0
    for sz in sizes[:-1]:
        acc += sz
        offsets.append(acc)
    u_a, v_a, z_a, q_b, k_b, v_b, z_b, cq, ckv, kr, z_c = jnp.split(proj, offsets, axis=-1)

    y_a = chunked_gmlp(u_a, v_a, a_g_v, a_w_s, a_b_s)
    y_b = stick_breaking(q_b.reshape(bn, s, B_HEADS, HEAD_DIM),
                         k_b.reshape(bn, s, B_HEADS, HEAD_DIM),
                         v_b.reshape(bn, s, B_HEADS, HEAD_DIM))
    y_c = mla(cq, ckv, kr, cos, sin, c_g_q, c_g_kv, c_w_uq, c_w_ukv)

    y = jnp.concatenate([
        gated_norm(y_a, g_out[:A_WIDTH], z_a),
        gated_norm(y_b, g_out[A_WIDTH:A_WIDTH + B_WIDTH], z_b),
        gated_norm(y_c, g_out[A_WIDTH + B_WIDTH:], z_c),
    ], axis=-1)
    return x + jnp.einsum('bsn,nd->bsd', y, w_out)


def setup_inputs(seed: int = 0) -> dict:
    key = jax.random.key(seed)
    ks = jax.random.split(key, 16)
    f32 = jnp.float32

    def nrm(k, shape, scale):
        return jax.random.normal(k, shape, f32) * scale

    x = jax.random.normal(ks[0], (BATCH, SEQ, D_MODEL), f32)
    offset = jax.random.randint(ks[1], (BATCH, 1), 0, 1024, dtype=jnp.int32)
    positions = (offset + jnp.arange(SEQ, dtype=jnp.int32)[None, :]).astype(jnp.int32)
    return {
        "x": x,
        "positions": positions,
        "g_pre": 1.0 + nrm(ks[2], (DEPTH, D_MODEL), 0.02),
        "w_in": nrm(ks[3], (DEPTH, D_MODEL, D_IN), D_MODEL ** -0.5),
        "a_g_v": 1.0 + nrm(ks[4], (DEPTH, A_GROUPS, A_CH), 0.02),
        "a_w_s": nrm(ks[5], (DEPTH, A_GROUPS, CHUNK, CHUNK), CHUNK ** -0.5),
        "a_b_s": 1.0 + nrm(ks[6], (DEPTH, A_GROUPS, CHUNK), 0.02),
        "c_g_q": 1.0 + nrm(ks[7], (DEPTH, Q_LORA), 0.02),
        "c_g_kv": 1.0 + nrm(ks[8], (DEPTH, KV_LORA), 0.02),
        "c_w_uq": nrm(ks[9], (DEPTH, Q_LORA, C_HEADS * (C_NOPE + C_ROPE)), Q_LORA ** -0.5),
        "c_w_ukv": nrm(ks[10], (DEPTH, KV_LORA, C_HEADS * (C_NOPE + C_VDIM)), KV_LORA ** -0.5),
        "g_out": 1.0 + nrm(ks[11], (DEPTH, D_MIX), 0.02),
        "w_out": nrm(ks[12], (DEPTH, D_MIX, D_MODEL), D_MIX ** -0.5),
        "g_final": 1.0 + nrm(ks[13], (D_MODEL,), 0.02),
    }


def reference(x, positions, g_pre, w_in, a_g_v, a_w_s, a_b_s, c_g_q, c_g_kv,
              c_w_uq, c_w_ukv, g_out, w_out, g_final):
    inv_freq = 1.0 / (ROPE_THETA ** (jnp.arange(0, C_ROPE, 2, dtype=jnp.float32) / C_ROPE))
    ang = positions.astype(jnp.float32)[..., None] * inv_freq
    cos = jnp.cos(ang)[:, :, None, :]
    sin = jnp.sin(ang)[:, :, None, :]
    h = x
    for l in range(DEPTH):
        h = hybrid_layer(h, cos, sin, g_pre[l], w_in[l], a_g_v[l], a_w_s[l], a_b_s[l],
                         c_g_q[l], c_g_kv[l], c_w_uq[l], c_w_ukv[l], g_out[l], w_out[l])
    return rmsnorm(h, g_final)
```

```python
import functools
import math

import jax
import jax.numpy as jnp
from jax import lax
from jax.experimental import pallas as pl
from jax.experimental.pallas import tpu as pltpu

F32 = jnp.float32
BF16 = jnp.bfloat16

D_MODEL = 4096
A_GROUPS = 8
A_CH = 128
A_WIDTH = A_GROUPS * A_CH
CHUNK = 128
HEADS = 12
HEAD_DIM = 128
B_WIDTH = HEADS * HEAD_DIM
C_NOPE = 128
C_ROPE = 64
C_WIDTH = HEADS * HEAD_DIM
Q_LORA = 768
KV_LORA = 512
ROPE_THETA = 10000.0
EPS = 1e-6

C_QK = 256
ROPE_HALF = C_ROPE // 2

OFF_U = 0
OFF_V = OFF_U + A_WIDTH
OFF_ZA = OFF_V + A_WIDTH
OFF_QB = OFF_ZA + A_WIDTH
OFF_KB = OFF_QB + B_WIDTH
OFF_VB = OFF_KB + B_WIDTH
OFF_ZB = OFF_VB + B_WIDTH
OFF_ZC = OFF_ZB + B_WIDTH
OFF_CKV = OFF_ZC + C_WIDTH
OFF_KR = OFF_CKV + KV_LORA
OFF_CQ = OFF_KR + C_QK
D_PROJ = OFF_CQ + Q_LORA

VMEM_LIMIT_BYTES = 56 * 1024 * 1024


def _params(sem):
    return pltpu.CompilerParams(dimension_semantics=sem,
                                vmem_limit_bytes=VMEM_LIMIT_BYTES)


def _gelu(x):
    c = math.sqrt(2.0 / math.pi)
    return x * (0.5 * (1.0 + jnp.tanh(c * (x + 0.044715 * (x * x * x)))))


def _silu(z):
    return z / (1.0 + jnp.exp(-z))


def _rms_scale(x, width):
    return lax.rsqrt(jnp.sum(x * x, axis=-1, keepdims=True) * (1.0 / width) + EPS)


def _rmsnorm_kernel(x_ref, g_ref, o_ref):
    x = x_ref[...]
    o_ref[...] = (x * _rms_scale(x, x.shape[-1]) * g_ref[...]).astype(o_ref.dtype)


def _rmsnorm(x, g, out_dtype, tm=256):
    m, d = x.shape
    return pl.pallas_call(
        _rmsnorm_kernel,
        grid=(m // tm,),
        in_specs=[pl.BlockSpec((tm, d), lambda i: (i, 0)),
                  pl.BlockSpec((1, d), lambda i: (0, 0))],
        out_specs=pl.BlockSpec((tm, d), lambda i: (i, 0)),
        out_shape=jax.ShapeDtypeStruct((m, d), out_dtype),
        compiler_params=_params(("parallel",)),
        name="rmsnorm",
    )(x, g.reshape(1, d))


def _matmul_kernel(x_ref, w_ref, o_ref):
    o_ref[...] = jnp.dot(x_ref[...], w_ref[...],
                         preferred_element_type=F32).astype(o_ref.dtype)


def _matmul(x, w, out_dtype, tm, tn):
    m, k = x.shape
    n = w.shape[1]
    return pl.pallas_call(
        _matmul_kernel,
        grid=(m // tm, n // tn),
        in_specs=[pl.BlockSpec((tm, k), lambda i, j: (i, 0)),
                  pl.BlockSpec((k, tn), lambda i, j: (0, j))],
        out_specs=pl.BlockSpec((tm, tn), lambda i, j: (i, j)),
        out_shape=jax.ShapeDtypeStruct((m, n), out_dtype),
        compiler_params=_params(("parallel", "parallel")),
        name="in_proj",
    )(x, w)


def _matmul_residual_kernel(x_ref, w_ref, r_ref, o_ref):
    o_ref[...] = r_ref[...] + jnp.dot(x_ref[...], w_ref[...],
                                      preferred_element_type=F32)


def _matmul_residual(x, w, res, tm, tn):
    m, k = x.shape
    n = w.shape[1]
    return pl.pallas_call(
        _matmul_residual_kernel,
        grid=(m // tm, n // tn),
        in_specs=[pl.BlockSpec((tm, k), lambda i, j: (i, 0)),
                  pl.BlockSpec((k, tn), lambda i, j: (0, j)),
                  pl.BlockSpec((tm, tn), lambda i, j: (i, j))],
        out_specs=pl.BlockSpec((tm, tn), lambda i, j: (i, j)),
        out_shape=jax.ShapeDtypeStruct((m, n), F32),
        compiler_params=_params(("parallel", "parallel")),
        name="out_proj",
    )(x, w, res)


def _gmlp_kernel(u_ref, v_ref, z_ref, gv_ref, w_ref, bias_ref, go_ref, o_ref,
                 y_ref):
    tm = u_ref.shape[0]
    t_idx = lax.broadcasted_iota(jnp.int32, (CHUNK, CHUNK), 0)
    s_idx = lax.broadcasted_iota(jnp.int32, (CHUNK, CHUNK), 1)
    causal = s_idx <= t_idx

    def chunk(c, carry):
        rows = pl.ds(pl.multiple_of(c * CHUNK, CHUNK), CHUNK)
        ssq = jnp.zeros((CHUNK, 1), F32)
        for g in range(A_GROUPS):
            cols = slice(g * A_CH, (g + 1) * A_CH)
            vg = _gelu(v_ref[rows, cols].astype(F32))
            vn = vg * _rms_scale(vg, A_CH) * gv_ref[:, cols]
            w = jnp.where(causal, w_ref[g], 0.0).astype(BF16)
            sv = jnp.dot(w, vn.astype(BF16), preferred_element_type=F32)
            sv = sv + bias_ref[:, cols]
            y = _gelu(u_ref[rows, cols].astype(F32)) * sv
            ssq = ssq + jnp.sum(y * y, axis=-1, keepdims=True)
            y_ref[:, cols] = y
        r = lax.rsqrt(ssq * (1.0 / A_WIDTH) + EPS)
        z = z_ref[rows, :].astype(F32)
        o_ref[rows, :] = (y_ref[...] * r * go_ref[...] * _silu(z)).astype(o_ref.dtype)
        return carry

    lax.fori_loop(0, tm // CHUNK, chunk, 0)


def _gmlp(proj, g_v, w_s, b_s, g_out_a, tm=512):
    m = proj.shape[0]
    bias = jnp.repeat(b_s.T, A_CH, axis=1)
    col = lambda off: (lambda i: (i, off // A_WIDTH))
    return pl.pallas_call(
        _gmlp_kernel,
        grid=(m // tm,),
        in_specs=[pl.BlockSpec((tm, A_WIDTH), col(OFF_U)),
                  pl.BlockSpec((tm, A_WIDTH), col(OFF_V)),
                  pl.BlockSpec((tm, A_WIDTH), col(OFF_ZA)),
                  pl.BlockSpec((1, A_WIDTH), lambda i: (0, 0)),
                  pl.BlockSpec((A_GROUPS, CHUNK, CHUNK), lambda i: (0, 0, 0)),
                  pl.BlockSpec((CHUNK, A_WIDTH), lambda i: (0, 0)),
                  pl.BlockSpec((1, A_WIDTH), lambda i: (0, 0))],
        out_specs=pl.BlockSpec((tm, A_WIDTH), lambda i: (i, 0)),
        out_shape=jax.ShapeDtypeStruct((m, A_WIDTH), BF16),
        scratch_shapes=[pltpu.VMEM((CHUNK, A_WIDTH), F32)],
        compiler_params=_params(("parallel",)),
        name="gmlp",
    )(proj, proj, proj, g_v.reshape(1, A_WIDTH), w_s, bias,
      g_out_a.reshape(1, A_WIDTH))


SB_T = 128


def _sb_kernel(q_ref, k_ref, v_ref, o_ref):
    t = SB_T
    i = pl.program_id(2)
    scale = HEAD_DIM ** -0.5
    q = q_ref[0]
    row = lax.broadcasted_iota(jnp.int32, (t, t), 0)
    col = lax.broadcasted_iota(jnp.int32, (t, t), 1)
    suffix = (row > col).astype(BF16)
    strict = col < row

    def tile(j, carry, acc, diagonal):
        ks = pl.ds(pl.multiple_of(j * t, t), t)
        k = k_ref[0, ks, :]
        v = v_ref[0, ks, :]
        z = lax.dot_general(q, k, (((1,), (1,)), ((), ())),
                            preferred_element_type=F32) * scale
        sp = jnp.maximum(z, 0.0) + jnp.log1p(jnp.exp(-jnp.abs(z)))
        log_keep = -sp
        if diagonal:
            log_keep = jnp.where(strict, log_keep, 0.0)
        hi = log_keep.astype(BF16)
        lo = (log_keep - hi.astype(F32)).astype(BF16)
        after = (jnp.dot(hi, suffix, preferred_element_type=F32)
                 + jnp.dot(lo, suffix, preferred_element_type=F32))
        a = jnp.exp((z - sp) + after + carry)
        if diagonal:
            a = jnp.where(strict, a, 0.0)
        acc = acc + jnp.dot(a.astype(BF16), v, preferred_element_type=F32)
        carry = carry + jnp.sum(log_keep, axis=-1, keepdims=True)
        return carry, acc

    carry = jnp.zeros((t, 1), F32)
    acc = jnp.zeros((t, HEAD_DIM), F32)
    carry, acc = tile(i, carry, acc, True)

    def body(step, state):
        return tile(i - 1 - step, state[0], state[1], False)

    carry, acc = lax.fori_loop(0, i, body, (carry, acc))
    o_ref[0] = acc.astype(o_ref.dtype)


def _stick_breaking(proj3):
    b, s, _ = proj3.shape
    t = SB_T
    blk = lambda off: off // HEAD_DIM
    return pl.pallas_call(
        _sb_kernel,
        grid=(b, HEADS, s // t),
        in_specs=[
            pl.BlockSpec((1, t, HEAD_DIM), lambda bi, h, i: (bi, i, blk(OFF_QB) + h)),
            pl.BlockSpec((1, s, HEAD_DIM), lambda bi, h, i: (bi, 0, blk(OFF_KB) + h)),
            pl.BlockSpec((1, s, HEAD_DIM), lambda bi, h, i: (bi, 0, blk(OFF_VB) + h)),
        ],
        out_specs=pl.BlockSpec((1, t, HEAD_DIM), lambda bi, h, i: (bi, i, h)),
        out_shape=jax.ShapeDtypeStruct((b, s, B_WIDTH), BF16),
        compiler_params=_params(("parallel", "parallel", "arbitrary")),
        name="stick_breaking",
    )(proj3, proj3, proj3)


def _rope256(x, c_ref, sa_ref, sb_ref):
    return (x * c_ref[...]
            + pltpu.roll(x, ROPE_HALF, axis=1) * sa_ref[...]
            + pltpu.roll(x, C_QK - ROPE_HALF, axis=1) * sb_ref[...])


def _mla_q_kernel(cq_ref, g_ref, w_ref, c_ref, sa_ref, sb_ref, o_ref):
    scale = (C_NOPE + C_ROPE) ** -0.5
    x = cq_ref[...].astype(F32)
    xn = (x * _rms_scale(x, Q_LORA) * g_ref[...]).astype(BF16)
    q = jnp.dot(xn, w_ref[...], preferred_element_type=F32)
    for h in range(HEADS):
        cols = slice(h * C_QK, (h + 1) * C_QK)
        o_ref[:, cols] = (_rope256(q[:, cols], c_ref, sa_ref, sb_ref)
                          * scale).astype(o_ref.dtype)


def _mla_kv_kernel(ckv_ref, kr_ref, g_ref, w_ref, c_ref, sa_ref, sb_ref,
                   k_ref, v_ref):
    x = ckv_ref[...].astype(F32)
    xn = (x * _rms_scale(x, KV_LORA) * g_ref[...]).astype(BF16)
    kv = jnp.dot(xn, w_ref[...], preferred_element_type=F32)
    kr = _rope256(kr_ref[...].astype(F32), c_ref, sa_ref, sb_ref)
    for h in range(HEADS):
        cols = slice(h * C_QK, (h + 1) * C_QK)
        k_ref[:, cols] = (kv[:, cols] + kr).astype(k_ref.dtype)
    v_ref[...] = kv[:, HEADS * C_QK:].astype(v_ref.dtype)


def _mla_project(proj, g_q, g_kv, w_uq_p, w_ukv_p, tabs, tm=512):
    m = proj.shape[0]
    tab_spec = pl.BlockSpec((tm, C_QK), lambda i: (i, 0))
    q_cat = pl.pallas_call(
        _mla_q_kernel,
        grid=(m // tm,),
        in_specs=[pl.BlockSpec((tm, Q_LORA), lambda i: (i, OFF_CQ // Q_LORA)),
                  pl.BlockSpec((1, Q_LORA), lambda i: (0, 0)),
                  pl.BlockSpec((Q_LORA, HEADS * C_QK), lambda i: (0, 0)),
                  tab_spec, tab_spec, tab_spec],
        out_specs=pl.BlockSpec((tm, HEADS * C_QK), lambda i: (i, 0)),
        out_shape=jax.ShapeDtypeStruct((m, HEADS * C_QK), BF16),
        compiler_params=_params(("parallel",)),
        name="mla_q",
    )(proj, g_q.reshape(1, Q_LORA), w_uq_p, *tabs)
    n_kv = HEADS * C_QK + C_WIDTH
    k_cat, v = pl.pallas_call(
        _mla_kv_kernel,
        grid=(m // tm,),
        in_specs=[pl.BlockSpec((tm, KV_LORA), lambda i: (i, OFF_CKV // KV_LORA)),
                  pl.BlockSpec((tm, C_QK), lambda i: (i, OFF_KR // C_QK)),
                  pl.BlockSpec((1, KV_LORA), lambda i: (0, 0)),
                  pl.BlockSpec((KV_LORA, n_kv), lambda i: (0, 0)),
                  tab_spec, tab_spec, tab_spec],
        out_specs=[pl.BlockSpec((tm, HEADS * C_QK), lambda i: (i, 0)),
                   pl.BlockSpec((tm, C_WIDTH), lambda i: (i, 0))],
        out_shape=[jax.ShapeDtypeStruct((m, HEADS * C_QK), BF16),
                   jax.ShapeDtypeStruct((m, C_WIDTH), BF16)],
        compiler_params=_params(("parallel",)),
        name="mla_kv",
    )(proj, proj, g_kv.reshape(1, KV_LORA), w_ukv_p, *tabs)
    return q_cat, k_cat, v


MLA_T = 256


def _mla_attn_kernel(q_ref, k_ref, v_ref, o_ref):
    t = MLA_T
    i = pl.program_id(2)
    q = q_ref[0]
    row = lax.broadcasted_iota(jnp.int32, (t, t), 0)
    col = lax.broadcasted_iota(jnp.int32, (t, t), 1)
    causal = col <= row

    def tile(j, m, l, acc, diagonal):
        ks = pl.ds(pl.multiple_of(j * t, t), t)
        s = lax.dot_general(q, k_ref[0, ks, :], (((1,), (1,)), ((), ())),
                            preferred_element_type=F32)
        if diagonal:
            s = jnp.where(causal, s, -jnp.inf)
        m_new = jnp.maximum(m, jnp.max(s, axis=-1, keepdims=True))
        alpha = jnp.exp(m - m_new)
        p = jnp.exp(s - m_new)
        l = alpha * l + jnp.sum(p, axis=-1, keepdims=True)
        acc = alpha * acc + jnp.dot(p.astype(BF16), v_ref[0, ks, :],
                                    preferred_element_type=F32)
        return m_new, l, acc

    def body(j, state):
        return tile(j, state[0], state[1], state[2], False)

    state = (jnp.full((t, 1), -jnp.inf, F32), jnp.zeros((t, 1), F32),
             jnp.zeros((t, HEAD_DIM), F32))
    m, l, acc = lax.fori_loop(0, i, body, state)
    m, l, acc = tile(i, m, l, acc, True)
    o_ref[0] = (acc / l).astype(o_ref.dtype)


def _mla_attention(q_cat, k_cat, v, b, s):
    t = MLA_T
    q3 = q_cat.reshape(b, s, HEADS * C_QK)
    k3 = k_cat.reshape(b, s, HEADS * C_QK)
    v3 = v.reshape(b, s, C_WIDTH)
    return pl.pallas_call(
        _mla_attn_kernel,
        grid=(b, HEADS, s // t),
        in_specs=[pl.BlockSpec((1, t, C_QK), lambda bi, h, i: (bi, i, h)),
                  pl.BlockSpec((1, s, C_QK), lambda bi, h, i: (bi, 0, h)),
                  pl.BlockSpec((1, s, HEAD_DIM), lambda bi, h, i: (bi, 0, h))],
        out_specs=pl.BlockSpec((1, t, HEAD_DIM), lambda bi, h, i: (bi, i, h)),
        out_shape=jax.ShapeDtypeStruct((b, s, C_WIDTH), BF16),
        compiler_params=_params(("parallel", "parallel", "arbitrary")),
        name="mla_attention",
    )(q3, k3, v3)


def _gate_kernel(ya_ref, yb_ref, zb_ref, yc_ref, zc_ref, g_ref, o_ref):
    o_ref[:, :A_WIDTH] = ya_ref[...]
    lo = A_WIDTH
    for y_ref, z_ref, width in ((yb_ref, zb_ref, B_WIDTH), (yc_ref, zc_ref, C_WIDTH)):
        y = y_ref[...].astype(F32)
        z = z_ref[...].astype(F32)
        g = g_ref[:, lo:lo + width]
        o_ref[:, lo:lo + width] = (y * _rms_scale(y, width) * g
                                   * _silu(z)).astype(o_ref.dtype)
        lo += width


def _gate(ya, yb, yc, proj, g_out, tm=512):
    m = ya.shape[0]
    return pl.pallas_call(
        _gate_kernel,
        grid=(m // tm,),
        in_specs=[pl.BlockSpec((tm, A_WIDTH), lambda i: (i, 0)),
                  pl.BlockSpec((tm, B_WIDTH), lambda i: (i, 0)),
                  pl.BlockSpec((tm, B_WIDTH), lambda i: (i, OFF_ZB // B_WIDTH)),
                  pl.BlockSpec((tm, C_WIDTH), lambda i: (i, 0)),
                  pl.BlockSpec((tm, C_WIDTH), lambda i: (i, OFF_ZC // C_WIDTH)),
                  pl.BlockSpec((1, D_MODEL), lambda i: (0, 0))],
        out_specs=pl.BlockSpec((tm, D_MODEL), lambda i: (i, 0)),
        out_shape=jax.ShapeDtypeStruct((m, D_MODEL), BF16),
        compiler_params=_params(("parallel",)),
        name="gate",
    )(ya, yb, proj, yc, proj, g_out.reshape(1, D_MODEL))


def _prep_w_in(w):
    cq0 = 3 * A_WIDTH + 4 * B_WIDTH
    ckv0 = cq0 + Q_LORA
    kr0 = ckv0 + KV_LORA
    zc0 = kr0 + C_ROPE
    k = w.shape[0]
    pieces = [w[:, :cq0], w[:, zc0:zc0 + C_WIDTH], w[:, ckv0:kr0],
              jnp.zeros((k, C_NOPE), w.dtype), w[:, kr0:zc0],
              jnp.zeros((k, C_QK - C_NOPE - C_ROPE), w.dtype), w[:, cq0:ckv0]]
    return jnp.concatenate(pieces, axis=1).astype(BF16)


def _prep_w_uq(w):
    w = w.reshape(Q_LORA, HEADS, C_NOPE + C_ROPE)
    w = jnp.pad(w, ((0, 0), (0, 0), (0, C_QK - C_NOPE - C_ROPE)))
    return w.reshape(Q_LORA, HEADS * C_QK).astype(BF16)


def _prep_w_ukv(w):
    w = w.reshape(KV_LORA, HEADS, C_NOPE + HEAD_DIM)
    wk = jnp.pad(w[:, :, :C_NOPE], ((0, 0), (0, 0), (0, C_QK - C_NOPE)))
    wv = w[:, :, C_NOPE:]
    return jnp.concatenate([wk.reshape(KV_LORA, HEADS * C_QK),
                            wv.reshape(KV_LORA, C_WIDTH)], axis=1).astype(BF16)


def _rope_tables(positions):
    inv_freq = 1.0 / (ROPE_THETA ** (jnp.arange(0, C_ROPE, 2, dtype=F32) / C_ROPE))
    ang = positions.astype(F32).reshape(-1, 1) * inv_freq
    cos, sin = jnp.cos(ang), jnp.sin(ang)
    m = ang.shape[0]
    pad = C_QK - C_NOPE - C_ROPE
    c = jnp.concatenate([jnp.ones((m, C_NOPE), F32), cos, cos,
                         jnp.zeros((m, pad), F32)], axis=1)
    sa = jnp.concatenate([jnp.zeros((m, C_NOPE + ROPE_HALF), F32), sin,
                          jnp.zeros((m, pad), F32)], axis=1)
    sb = jnp.concatenate([jnp.zeros((m, C_NOPE), F32), -sin,
                          jnp.zeros((m, ROPE_HALF + pad), F32)], axis=1)
    return c, sa, sb


def kernel(x, positions, g_pre, w_in, a_g_v, a_w_s, a_b_s, c_g_q, c_g_kv,
           c_w_uq, c_w_ukv, g_out, w_out, g_final):
    b, s, d = x.shape
    m = b * s
    depth = w_in.shape[0]
    tabs = _rope_tables(positions)
    h = x.reshape(m, d)
    for l in range(depth):
        hn = _rmsnorm(h, g_pre[l], BF16)
        proj = _matmul(hn, _prep_w_in(w_in[l]), BF16, tm=1024, tn=1024)
        ya = _gmlp(proj, a_g_v[l], a_w_s[l], a_b_s[l], g_out[l, :A_WIDTH])
        yb = _stick_breaking(proj.reshape(b, s, D_PROJ)).reshape(m, B_WIDTH)
        q_cat, k_cat, v = _mla_project(proj, c_g_q[l], c_g_kv[l],
                                       _prep_w_uq(c_w_uq[l]),
                                       _prep_w_ukv(c_w_ukv[l]), tabs)
        yc = _mla_attention(q_cat, k_cat, v, b, s).reshape(m, C_WIDTH)
        y = _gate(ya, yb, yc, proj, g_out[l])
        h = _matmul_residual(y, w_out[l].astype(BF16), h, tm=1024, tn=512)
    return _rmsnorm(h, g_final, F32).reshape(b, s, d)
```

```python
import functools
import math

import jax
import jax.numpy as jnp
from jax import lax
from jax.experimental import pallas as pl
from jax.experimental.pallas import tpu as pltpu

F32 = jnp.float32
BF16 = jnp.bfloat16

D_MODEL = 4096
A_GROUPS = 8
A_CH = 128
A_WIDTH = A_GROUPS * A_CH
CHUNK = 128
HEADS = 12
HEAD_DIM = 128
B_WIDTH = HEADS * HEAD_DIM
C_NOPE = 128
C_ROPE = 64
C_WIDTH = HEADS * HEAD_DIM
Q_LORA = 768
KV_LORA = 512
ROPE_THETA = 10000.0
EPS = 1e-6

C_QK = 256
ROPE_HALF = C_ROPE // 2

OFF_U = 0
OFF_V = OFF_U + A_WIDTH
OFF_ZA = OFF_V + A_WIDTH
OFF_QB = OFF_ZA + A_WIDTH
OFF_KB = OFF_QB + B_WIDTH
OFF_VB = OFF_KB + B_WIDTH
OFF_ZB = OFF_VB + B_WIDTH
D_MAIN = OFF_ZB + B_WIDTH
REF_CQ = D_MAIN
REF_CKV = REF_CQ + Q_LORA
REF_KR = REF_CKV + KV_LORA
REF_ZC = REF_KR + C_ROPE
TAIL_ZC = 0
TAIL_CKV = TAIL_ZC + C_WIDTH
TAIL_KR = TAIL_CKV + KV_LORA
TAIL_CQ = TAIL_KR + C_QK
D_TAIL = TAIL_CQ + Q_LORA

VMEM_LIMIT_BYTES = 56 * 1024 * 1024


def _params(sem):
    return pltpu.CompilerParams(dimension_semantics=sem,
                                vmem_limit_bytes=VMEM_LIMIT_BYTES)


def _gelu(x):
    c = math.sqrt(2.0 / math.pi)
    return x * (0.5 * (1.0 + jnp.tanh(c * (x + 0.044715 * (x * x * x)))))


def _silu(z):
    return z / (1.0 + jnp.exp(-z))


def _rms_scale(x, width):
    return lax.rsqrt(jnp.sum(x * x, axis=-1, keepdims=True) * (1.0 / width) + EPS)


def _rmsnorm_kernel(x_ref, g_ref, o_ref):
    x = x_ref[...]
    o_ref[...] = (x * _rms_scale(x, x.shape[-1]) * g_ref[...]).astype(o_ref.dtype)


def _rmsnorm(x, g, out_dtype, tm=256):
    m, d = x.shape
    return pl.pallas_call(
        _rmsnorm_kernel,
        grid=(m // tm,),
        in_specs=[pl.BlockSpec((tm, d), lambda i: (i, 0)),
                  pl.BlockSpec((1, d), lambda i: (0, 0))],
        out_specs=pl.BlockSpec((tm, d), lambda i: (i, 0)),
        out_shape=jax.ShapeDtypeStruct((m, d), out_dtype),
        compiler_params=_params(("parallel",)),
        name="rmsnorm",
    )(x, g.reshape(1, d))


def _matmul_kernel(x_ref, w_ref, o_ref):
    o_ref[...] = jnp.dot(x_ref[...], w_ref[...].astype(BF16),
                         preferred_element_type=F32).astype(o_ref.dtype)


def _matmul(x, w, n, out_dtype, tm, tn, layer=None, name="matmul"):
    m, k = x.shape
    if layer is None:
        w_spec = pl.BlockSpec((k, tn), lambda i, j: (0, j))
    else:
        w_spec = pl.BlockSpec((None, k, tn), lambda i, j: (layer, 0, j))
    return pl.pallas_call(
        _matmul_kernel,
        grid=(m // tm, n // tn),
        in_specs=[pl.BlockSpec((tm, k), lambda i, j: (i, 0)), w_spec],
        out_specs=pl.BlockSpec((tm, tn), lambda i, j: (i, j)),
        out_shape=jax.ShapeDtypeStruct((m, n), out_dtype),
        compiler_params=_params(("parallel", "parallel")),
        name=name,
    )(x, w)


def _matmul_residual_kernel(x_ref, w_ref, r_ref, o_ref):
    o_ref[...] = r_ref[...] + jnp.dot(x_ref[...], w_ref[...].astype(BF16),
                                      preferred_element_type=F32)


def _matmul_residual(x, w, layer, res, tm, tn):
    m, k = x.shape
    n = w.shape[2]
    return pl.pallas_call(
        _matmul_residual_kernel,
        grid=(m // tm, n // tn),
        in_specs=[pl.BlockSpec((tm, k), lambda i, j: (i, 0)),
                  pl.BlockSpec((None, k, tn), lambda i, j: (layer, 0, j)),
                  pl.BlockSpec((tm, tn), lambda i, j: (i, j))],
        out_specs=pl.BlockSpec((tm, tn), lambda i, j: (i, j)),
        out_shape=jax.ShapeDtypeStruct((m, n), F32),
        compiler_params=_params(("parallel", "parallel")),
        name="out_proj",
    )(x, w, res)


def _gmlp_kernel(u_ref, v_ref, z_ref, gv_ref, w_ref, bias_ref, go_ref, o_ref,
                 y_ref):
    tm = u_ref.shape[0]
    t_idx = lax.broadcasted_iota(jnp.int32, (CHUNK, CHUNK), 0)
    s_idx = lax.broadcasted_iota(jnp.int32, (CHUNK, CHUNK), 1)
    causal = s_idx <= t_idx

    def chunk(c, carry):
        rows = pl.ds(pl.multiple_of(c * CHUNK, CHUNK), CHUNK)
        ssq = jnp.zeros((CHUNK, 1), F32)
        for g in range(A_GROUPS):
            cols = slice(g * A_CH, (g + 1) * A_CH)
            vg = _gelu(v_ref[rows, cols].astype(F32))
            vn = vg * _rms_scale(vg, A_CH) * gv_ref[:, cols]
            w = jnp.where(causal, w_ref[g], 0.0).astype(BF16)
            sv = jnp.dot(w, vn.astype(BF16), preferred_element_type=F32)
            sv = sv + bias_ref[:, cols]
            y = _gelu(u_ref[rows, cols].astype(F32)) * sv
            ssq = ssq + jnp.sum(y * y, axis=-1, keepdims=True)
            y_ref[:, cols] = y
        r = lax.rsqrt(ssq * (1.0 / A_WIDTH) + EPS)
        z = z_ref[rows, :].astype(F32)
        o_ref[rows, :] = (y_ref[...] * r * go_ref[...] * _silu(z)).astype(o_ref.dtype)
        return carry

    lax.fori_loop(0, tm // CHUNK, chunk, 0)


def _gmlp(proj, g_v, w_s, b_s, g_out_a, tm=512):
    m = proj.shape[0]
    bias = jnp.repeat(b_s.T, A_CH, axis=1)
    col = lambda off: (lambda i: (i, off // A_WIDTH))
    return pl.pallas_call(
        _gmlp_kernel,
        grid=(m // tm,),
        in_specs=[pl.BlockSpec((tm, A_WIDTH), col(OFF_U)),
                  pl.BlockSpec((tm, A_WIDTH), col(OFF_V)),
                  pl.BlockSpec((tm, A_WIDTH), col(OFF_ZA)),
                  pl.BlockSpec((1, A_WIDTH), lambda i: (0, 0)),
                  pl.BlockSpec((A_GROUPS, CHUNK, CHUNK), lambda i: (0, 0, 0)),
                  pl.BlockSpec((CHUNK, A_WIDTH), lambda i: (0, 0)),
                  pl.BlockSpec((1, A_WIDTH), lambda i: (0, 0))],
        out_specs=pl.BlockSpec((tm, A_WIDTH), lambda i: (i, 0)),
        out_shape=jax.ShapeDtypeStruct((m, A_WIDTH), BF16),
        scratch_shapes=[pltpu.VMEM((CHUNK, A_WIDTH), F32)],
        compiler_params=_params(("parallel",)),
        name="gmlp",
    )(proj, proj, proj, g_v.reshape(1, A_WIDTH), w_s, bias,
      g_out_a.reshape(1, A_WIDTH))


SB_T = 128
SB_SUB = 128
SB_W = 2 * SB_SUB
SB_HG = 6
SB_ZERO_WEIGHT_LOG = -105.0


def _sb_kernel(q_ref, k_ref, v_ref, o_ref, acc_ref, carry_ref):
    t, w, sub = SB_T, SB_W, SB_SUB
    i = pl.program_id(2)
    scale = HEAD_DIM ** -0.5
    q_pos = i * t + lax.broadcasted_iota(jnp.int32, (t, w), 0)
    k_off = lax.broadcasted_iota(jnp.int32, (t, w), 1)
    row = lax.broadcasted_iota(jnp.int32, (sub, sub), 0)
    col = lax.broadcasted_iota(jnp.int32, (sub, sub), 1)
    suffix = (row > col).astype(BF16)

    acc_ref[...] = jnp.zeros_like(acc_ref)
    carry_ref[...] = jnp.zeros_like(carry_ref)

    def window(start, bound):
        valid = (start + k_off) < jnp.minimum(q_pos, bound)
        ks = pl.ds(pl.multiple_of(start, sub), w)
        log_beta, log_keep, pieces = [], [], []
        for h in range(SB_HG):
            hc = slice(h * HEAD_DIM, (h + 1) * HEAD_DIM)
            z = lax.dot_general(q_ref[0, :, hc], k_ref[0, ks, hc],
                                (((1,), (1,)), ((), ())),
                                preferred_element_type=F32) * scale
            sp = jnp.maximum(z, 0.0) + jnp.log1p(jnp.exp(-jnp.abs(z)))
            lk = jnp.where(valid, -sp, 0.0)
            log_beta.append(z - sp)
            log_keep.append(lk)
            hi = lk.astype(BF16)
            lo = (lk - hi.astype(F32)).astype(BF16)
            pieces += [hi[:, :sub], lo[:, :sub], hi[:, sub:], lo[:, sub:]]
        local = jnp.dot(jnp.concatenate(pieces, axis=0), suffix,
                        preferred_element_type=F32)
        worst = None
        for h in range(SB_HG):
            hc = slice(h * HEAD_DIM, (h + 1) * HEAD_DIM)
            base = 4 * h * t
            lower = local[base:base + t] + local[base + t:base + 2 * t]
            upper = (local[base + 2 * t:base + 3 * t]
                     + local[base + 3 * t:base + 4 * t])
            lk = log_keep[h]
            upper_total = jnp.sum(lk[:, sub:], axis=-1, keepdims=True)
            lower_total = jnp.sum(lk[:, :sub], axis=-1, keepdims=True)
            carry = carry_ref[h]
            after = jnp.concatenate(
                [lower + (upper_total + carry), upper + carry], axis=1)
            a = jnp.where(valid, jnp.exp(log_beta[h] + after), 0.0)
            acc_ref[h] += jnp.dot(a.astype(BF16), v_ref[0, ks, hc],
                                  preferred_element_type=F32)
            carry = carry + (upper_total + lower_total)
            carry_ref[h] = carry
            top = jnp.max(carry)
            worst = top if worst is None else jnp.maximum(worst, top)
        return worst

    first = jnp.maximum(i - 1, 0) * sub
    worst = window(first, jnp.int32(2 ** 30))

    def more(state):
        start, worst = state
        return jnp.logical_and(start > 0, worst > SB_ZERO_WEIGHT_LOG)

    def step(state):
        start, _ = state
        nxt = jnp.maximum(start - w, 0)
        return nxt, window(nxt, start)

    lax.while_loop(more, step, (first, worst))
    for h in range(SB_HG):
        o_ref[0, :, h * HEAD_DIM:(h + 1) * HEAD_DIM] = acc_ref[h].astype(o_ref.dtype)


def _stick_breaking(proj3):
    b, s, _ = proj3.shape
    t, gw = SB_T, SB_HG * HEAD_DIM
    blk = lambda off: off // gw
    return pl.pallas_call(
        _sb_kernel,
        grid=(b, HEADS // SB_HG, s // t),
        in_specs=[
            pl.BlockSpec((1, t, gw), lambda bi, g, i: (bi, i, blk(OFF_QB) + g)),
            pl.BlockSpec((1, s, gw), lambda bi, g, i: (bi, 0, blk(OFF_KB) + g)),
            pl.BlockSpec((1, s, gw), lambda bi, g, i: (bi, 0, blk(OFF_VB) + g)),
        ],
        out_specs=pl.BlockSpec((1, t, gw), lambda bi, g, i: (bi, i, g)),
        out_shape=jax.ShapeDtypeStruct((b, s, B_WIDTH), BF16),
        scratch_shapes=[pltpu.VMEM((SB_HG, t, HEAD_DIM), F32),
                        pltpu.VMEM((SB_HG, t, 1), F32)],
        compiler_params=_params(("parallel", "parallel", "arbitrary")),
        name="stick_breaking",
    )(proj3, proj3, proj3)


def _rope256(x, c_ref, sa_ref, sb_ref):
    return (x * c_ref[...]
            + pltpu.roll(x, ROPE_HALF, axis=1) * sa_ref[...]
            + pltpu.roll(x, C_QK - ROPE_HALF, axis=1) * sb_ref[...])


def _mla_q_kernel(cq_ref, g_ref, w_ref, c_ref, sa_ref, sb_ref, o_ref):
    scale = (C_NOPE + C_ROPE) ** -0.5
    x = cq_ref[...].astype(F32)
    xn = (x * _rms_scale(x, Q_LORA) * g_ref[...]).astype(BF16)
    q = jnp.dot(xn, w_ref[...], preferred_element_type=F32)
    for h in range(HEADS):
        cols = slice(h * C_QK, (h + 1) * C_QK)
        o_ref[:, cols] = (_rope256(q[:, cols], c_ref, sa_ref, sb_ref)
                          * scale).astype(o_ref.dtype)


def _mla_kv_kernel(ckv_ref, kr_ref, g_ref, w_ref, c_ref, sa_ref, sb_ref,
                   k_ref, v_ref):
    x = ckv_ref[...].astype(F32)
    xn = (x * _rms_scale(x, KV_LORA) * g_ref[...]).astype(BF16)
    kv = jnp.dot(xn, w_ref[...], preferred_element_type=F32)
    kr = _rope256(kr_ref[...].astype(F32), c_ref, sa_ref, sb_ref)
    for h in range(HEADS):
        cols = slice(h * C_QK, (h + 1) * C_QK)
        k_ref[:, cols] = (kv[:, cols] + kr).astype(k_ref.dtype)
    v_ref[...] = kv[:, HEADS * C_QK:].astype(v_ref.dtype)


def _mla_project(proj, g_q, g_kv, w_uq_p, w_ukv_p, tabs, tm=512):
    m = proj.shape[0]
    tab_spec = pl.BlockSpec((tm, C_QK), lambda i: (i, 0))
    q_cat = pl.pallas_call(
        _mla_q_kernel,
        grid=(m // tm,),
        in_specs=[pl.BlockSpec((tm, Q_LORA), lambda i: (i, TAIL_CQ // Q_LORA)),
                  pl.BlockSpec((1, Q_LORA), lambda i: (0, 0)),
                  pl.BlockSpec((Q_LORA, HEADS * C_QK), lambda i: (0, 0)),
                  tab_spec, tab_spec, tab_spec],
        out_specs=pl.BlockSpec((tm, HEADS * C_QK), lambda i: (i, 0)),
        out_shape=jax.ShapeDtypeStruct((m, HEADS * C_QK), BF16),
        compiler_params=_params(("parallel",)),
        name="mla_q",
    )(proj, g_q.reshape(1, Q_LORA), w_uq_p, *tabs)
    n_kv = HEADS * C_QK + C_WIDTH
    k_cat, v = pl.pallas_call(
        _mla_kv_kernel,
        grid=(m // tm,),
        in_specs=[pl.BlockSpec((tm, KV_LORA), lambda i: (i, TAIL_CKV // KV_LORA)),
                  pl.BlockSpec((tm, C_QK), lambda i: (i, TAIL_KR // C_QK)),
                  pl.BlockSpec((1, KV_LORA), lambda i: (0, 0)),
                  pl.BlockSpec((KV_LORA, n_kv), lambda i: (0, 0)),
                  tab_spec, tab_spec, tab_spec],
        out_specs=[pl.BlockSpec((tm, HEADS * C_QK), lambda i: (i, 0)),
                   pl.BlockSpec((tm, C_WIDTH), lambda i: (i, 0))],
        out_shape=[jax.ShapeDtypeStruct((m, HEADS * C_QK), BF16),
                   jax.ShapeDtypeStruct((m, C_WIDTH), BF16)],
        compiler_params=_params(("parallel",)),
        name="mla_kv",
    )(proj, proj, g_kv.reshape(1, KV_LORA), w_ukv_p, *tabs)
    return q_cat, k_cat, v


MLA_T = 256
MLA_W = 512
MLA_HG = 4


def _mla_attn_kernel(q_ref, k_ref, v_ref, o_ref, m_ref, l_ref, acc_ref):
    t, w = MLA_T, MLA_W
    i = pl.program_id(2)
    q_pos = i * t + lax.broadcasted_iota(jnp.int32, (t, w), 0)
    k_off = lax.broadcasted_iota(jnp.int32, (t, w), 1)

    m_ref[...] = jnp.full_like(m_ref, -jnp.inf)
    l_ref[...] = jnp.zeros_like(l_ref)
    acc_ref[...] = jnp.zeros_like(acc_ref)

    def window(j, masked):
        start = pl.multiple_of(j * w, w)
        ks = pl.ds(start, w)
        for h in range(MLA_HG):
            s = lax.dot_general(q_ref[0, :, h * C_QK:(h + 1) * C_QK],
                                k_ref[0, ks, h * C_QK:(h + 1) * C_QK],
                                (((1,), (1,)), ((), ())),
                                preferred_element_type=F32)
            if masked:
                s = jnp.where((start + k_off) <= q_pos, s, -jnp.inf)
            m_old = m_ref[h]
            m_new = jnp.maximum(m_old, jnp.max(s, axis=-1, keepdims=True))
            alpha = jnp.exp(m_old - m_new)
            p = jnp.exp(s - m_new)
            l_ref[h] = alpha * l_ref[h] + jnp.sum(p, axis=-1, keepdims=True)
            acc_ref[h] = alpha * acc_ref[h] + jnp.dot(
                p.astype(BF16), v_ref[0, ks, h * HEAD_DIM:(h + 1) * HEAD_DIM],
                preferred_element_type=F32)
            m_ref[h] = m_new

    last = (i * t) // w

    def body(j, carry):
        window(j, False)
        return carry

    lax.fori_loop(0, last, body, 0)
    window(last, True)
    for h in range(MLA_HG):
        o_ref[0, :, h * HEAD_DIM:(h + 1) * HEAD_DIM] = (
            acc_ref[h] / l_ref[h]).astype(o_ref.dtype)


def _mla_attention(q_cat, k_cat, v, b, s):
    t, hg = MLA_T, MLA_HG
    q3 = q_cat.reshape(b, s, HEADS * C_QK)
    k3 = k_cat.reshape(b, s, HEADS * C_QK)
    v3 = v.reshape(b, s, C_WIDTH)
    return pl.pallas_call(
        _mla_attn_kernel,
        grid=(b, HEADS // hg, s // t),
        in_specs=[pl.BlockSpec((1, t, hg * C_QK), lambda bi, g, i: (bi, i, g)),
                  pl.BlockSpec((1, s, hg * C_QK), lambda bi, g, i: (bi, 0, g)),
                  pl.BlockSpec((1, s, hg * HEAD_DIM), lambda bi, g, i: (bi, 0, g))],
        out_specs=pl.BlockSpec((1, t, hg * HEAD_DIM), lambda bi, g, i: (bi, i, g)),
        out_shape=jax.ShapeDtypeStruct((b, s, C_WIDTH), BF16),
        scratch_shapes=[pltpu.VMEM((hg, t, 1), F32),
                        pltpu.VMEM((hg, t, 1), F32),
                        pltpu.VMEM((hg, t, HEAD_DIM), F32)],
        compiler_params=_params(("parallel", "parallel", "arbitrary")),
        name="mla_attention",
    )(q3, k3, v3)


def _gate_kernel(ya_ref, yb_ref, zb_ref, yc_ref, zc_ref, g_ref, o_ref):
    o_ref[:, :A_WIDTH] = ya_ref[...]
    lo = A_WIDTH
    for y_ref, z_ref, width in ((yb_ref, zb_ref, B_WIDTH), (yc_ref, zc_ref, C_WIDTH)):
        y = y_ref[...].astype(F32)
        z = z_ref[...].astype(F32)
        g = g_ref[:, lo:lo + width]
        o_ref[:, lo:lo + width] = (y * _rms_scale(y, width) * g
                                   * _silu(z)).astype(o_ref.dtype)
        lo += width


def _gate(ya, yb, yc, main, tail, g_out, tm=512):
    m = ya.shape[0]
    return pl.pallas_call(
        _gate_kernel,
        grid=(m // tm,),
        in_specs=[pl.BlockSpec((tm, A_WIDTH), lambda i: (i, 0)),
                  pl.BlockSpec((tm, B_WIDTH), lambda i: (i, 0)),
                  pl.BlockSpec((tm, B_WIDTH), lambda i: (i, OFF_ZB // B_WIDTH)),
                  pl.BlockSpec((tm, C_WIDTH), lambda i: (i, 0)),
                  pl.BlockSpec((tm, C_WIDTH), lambda i: (i, TAIL_ZC // C_WIDTH)),
                  pl.BlockSpec((1, D_MODEL), lambda i: (0, 0))],
        out_specs=pl.BlockSpec((tm, D_MODEL), lambda i: (i, 0)),
        out_shape=jax.ShapeDtypeStruct((m, D_MODEL), BF16),
        compiler_params=_params(("parallel",)),
        name="gate",
    )(ya, yb, main, yc, tail, g_out.reshape(1, D_MODEL))


def _prep_w_tail(w):
    k = w.shape[0]
    pieces = [w[:, REF_ZC:REF_ZC + C_WIDTH], w[:, REF_CKV:REF_KR],
              jnp.zeros((k, C_NOPE), w.dtype), w[:, REF_KR:REF_ZC],
              jnp.zeros((k, C_QK - C_NOPE - C_ROPE), w.dtype),
              w[:, REF_CQ:REF_CKV]]
    return jnp.concatenate(pieces, axis=1).astype(BF16)


def _prep_w_uq(w):
    w = w.reshape(Q_LORA, HEADS, C_NOPE + C_ROPE)
    w = jnp.pad(w, ((0, 0), (0, 0), (0, C_QK - C_NOPE - C_ROPE)))
    return w.reshape(Q_LORA, HEADS * C_QK).astype(BF16)


def _prep_w_ukv(w):
    w = w.reshape(KV_LORA, HEADS, C_NOPE + HEAD_DIM)
    wk = jnp.pad(w[:, :, :C_NOPE], ((0, 0), (0, 0), (0, C_QK - C_NOPE)))
    wv = w[:, :, C_NOPE:]
    return jnp.concatenate([wk.reshape(KV_LORA, HEADS * C_QK),
                            wv.reshape(KV_LORA, C_WIDTH)], axis=1).astype(BF16)


def _rope_tables(positions):
    inv_freq = 1.0 / (ROPE_THETA ** (jnp.arange(0, C_ROPE, 2, dtype=F32) / C_ROPE))
    ang = positions.astype(F32).reshape(-1, 1) * inv_freq
    cos, sin = jnp.cos(ang), jnp.sin(ang)
    m = ang.shape[0]
    pad = C_QK - C_NOPE - C_ROPE
    c = jnp.concatenate([jnp.ones((m, C_NOPE), F32), cos, cos,
                         jnp.zeros((m, pad), F32)], axis=1)
    sa = jnp.concatenate([jnp.zeros((m, C_NOPE + ROPE_HALF), F32), sin,
                          jnp.zeros((m, pad), F32)], axis=1)
    sb = jnp.concatenate([jnp.zeros((m, C_NOPE), F32), -sin,
                          jnp.zeros((m, ROPE_HALF + pad), F32)], axis=1)
    return c, sa, sb


def kernel(x, positions, g_pre, w_in, a_g_v, a_w_s, a_b_s, c_g_q, c_g_kv,
           c_w_uq, c_w_ukv, g_out, w_out, g_final):
    b, s, d = x.shape
    m = b * s
    depth = w_in.shape[0]
    tabs = _rope_tables(positions)
    h = x.reshape(m, d)
    for l in range(depth):
        hn = _rmsnorm(h, g_pre[l], BF16)
        main = _matmul(hn, w_in, D_MAIN, BF16, tm=1024, tn=512, layer=l,
                       name="in_proj_main")
        tail = _matmul(hn, _prep_w_tail(w_in[l]), D_TAIL, BF16,
                       tm=1024, tn=1024, name="in_proj_tail")
        ya = _gmlp(main, a_g_v[l], a_w_s[l], a_b_s[l], g_out[l, :A_WIDTH])
        yb = _stick_breaking(main.reshape(b, s, D_MAIN)).reshape(m, B_WIDTH)
        q_cat, k_cat, v = _mla_project(tail, c_g_q[l], c_g_kv[l],
                                       _prep_w_uq(c_w_uq[l]),
                                       _prep_w_ukv(c_w_ukv[l]), tabs)
        yc = _mla_attention(q_cat, k_cat, v, b, s).reshape(m, C_WIDTH)
        y = _gate(ya, yb, yc, main, tail, g_out[l])
        h = _matmul_residual(y, w_out, l, h, tm=1024, tn=512)
    return _rmsnorm(h, g_final, F32).reshape(b, s, d)
```

```python
import functools
import math

import jax
import jax.numpy as jnp
from jax import lax
from jax.experimental import pallas as pl
from jax.experimental.pallas import tpu as pltpu

F32 = jnp.float32
BF16 = jnp.bfloat16

D_MODEL = 4096
A_GROUPS = 8
A_CH = 128
A_WIDTH = A_GROUPS * A_CH
CHUNK = 128
HEADS = 12
HEAD_DIM = 128
B_WIDTH = HEADS * HEAD_DIM
C_NOPE = 128
C_ROPE = 64
C_WIDTH = HEADS * HEAD_DIM
Q_LORA = 768
KV_LORA = 512
ROPE_THETA = 10000.0
EPS = 1e-6

C_QK = 256
ROPE_HALF = C_ROPE // 2

OFF_U = 0
OFF_V = OFF_U + A_WIDTH
OFF_ZA = OFF_V + A_WIDTH
OFF_QB = OFF_ZA + A_WIDTH
OFF_KB = OFF_QB + B_WIDTH
OFF_VB = OFF_KB + B_WIDTH
OFF_ZB = OFF_VB + B_WIDTH
D_MAIN = OFF_ZB + B_WIDTH
REF_CQ = D_MAIN
REF_CKV = REF_CQ + Q_LORA
REF_KR = REF_CKV + KV_LORA
REF_ZC = REF_KR + C_ROPE
TAIL_ZC = 0
TAIL_CKV = TAIL_ZC + C_WIDTH
TAIL_KR = TAIL_CKV + KV_LORA
TAIL_CQ = TAIL_KR + C_QK
D_TAIL = TAIL_CQ + Q_LORA

VMEM_LIMIT_BYTES = 56 * 1024 * 1024


def _params(sem):
    return pltpu.CompilerParams(dimension_semantics=sem,
                                vmem_limit_bytes=VMEM_LIMIT_BYTES)


def _gelu(x):
    c = math.sqrt(2.0 / math.pi)
    return x * (0.5 * (1.0 + jnp.tanh(c * (x + 0.044715 * (x * x * x)))))


def _silu(z):
    return z / (1.0 + jnp.exp(-z))


def _rms_scale(x, width):
    return lax.rsqrt(jnp.sum(x * x, axis=-1, keepdims=True) * (1.0 / width) + EPS)


def _rmsnorm_kernel(x_ref, g_ref, o_ref):
    x = x_ref[...]
    o_ref[...] = (x * _rms_scale(x, x.shape[-1]) * g_ref[...]).astype(o_ref.dtype)


def _rmsnorm(x, g, out_dtype, tm=256):
    m, d = x.shape
    return pl.pallas_call(
        _rmsnorm_kernel,
        grid=(m // tm,),
        in_specs=[pl.BlockSpec((tm, d), lambda i: (i, 0)),
                  pl.BlockSpec((1, d), lambda i: (0, 0))],
        out_specs=pl.BlockSpec((tm, d), lambda i: (i, 0)),
        out_shape=jax.ShapeDtypeStruct((m, d), out_dtype),
        compiler_params=_params(("parallel",)),
        name="rmsnorm",
    )(x, g.reshape(1, d))


_NT = (((1,), (1,)), ((), ()))


def _in_proj_main_kernel(x_ref, wt_ref, o_ref):
    o_ref[...] = lax.dot_general(x_ref[...], wt_ref[...].astype(BF16), _NT,
                                 preferred_element_type=F32).astype(o_ref.dtype)


def _in_proj_main(x, wt, layer, tm, tn):
    m, k = x.shape
    return pl.pallas_call(
        _in_proj_main_kernel,
        grid=(m // tm, D_MAIN // tn),
        in_specs=[pl.BlockSpec((tm, k), lambda i, j: (i, 0)),
                  pl.BlockSpec((None, tn, k), lambda i, j: (layer, j, 0))],
        out_specs=pl.BlockSpec((tm, tn), lambda i, j: (i, j)),
        out_shape=jax.ShapeDtypeStruct((m, D_MAIN), BF16),
        compiler_params=_params(("parallel", "parallel")),
        name="in_proj_main",
    )(x, wt)


TAIL_PART_ROWS = 576
TAIL_PARTS = 5
TAIL_COPIES = (
    (TAIL_ZC, REF_ZC - D_MAIN, C_WIDTH),
    (TAIL_CKV, REF_CKV - D_MAIN, KV_LORA),
    (TAIL_KR + C_NOPE, REF_KR - D_MAIN, C_ROPE),
    (TAIL_CQ, REF_CQ - D_MAIN, Q_LORA),
)
TAIL_ZERO_ROWS = ((TAIL_KR, C_NOPE),
                  (TAIL_KR + C_NOPE + C_ROPE, C_QK - C_NOPE - C_ROPE))


def _in_proj_tail_kernel(x_ref, *refs):
    w_refs = refs[:TAIL_PARTS]
    o_ref, wp_ref, acc_ref = refs[TAIL_PARTS:]
    kk = pl.program_id(1)

    @pl.when(kk == 0)
    def _():
        acc_ref[...] = jnp.zeros_like(acc_ref)

    for dst, src, rows in TAIL_COPIES:
        done = 0
        while done < rows:
            part, lo = divmod(src + done, TAIL_PART_ROWS)
            n = min(rows - done, TAIL_PART_ROWS - lo)
            wp_ref[dst + done:dst + done + n, :] = (
                w_refs[part][lo:lo + n, :].astype(BF16))
            done += n
    for dst, rows in TAIL_ZERO_ROWS:
        wp_ref[dst:dst + rows, :] = jnp.zeros((rows, wp_ref.shape[1]), BF16)

    acc_ref[...] += lax.dot_general(x_ref[...], wp_ref[...], _NT,
                                    preferred_element_type=F32)

    @pl.when(kk == pl.num_programs(1) - 1)
    def _():
        o_ref[...] = acc_ref[...].astype(o_ref.dtype)


def _in_proj_tail(x, wt, layer, tm, tk):
    m, k = x.shape
    first = D_MAIN // TAIL_PART_ROWS
    w_specs = [pl.BlockSpec((None, TAIL_PART_ROWS, tk),
                            lambda i, kk, p=p: (layer, first + p, kk))
               for p in range(TAIL_PARTS)]
    return pl.pallas_call(
        _in_proj_tail_kernel,
        grid=(m // tm, k // tk),
        in_specs=[pl.BlockSpec((tm, tk), lambda i, kk: (i, kk))] + w_specs,
        out_specs=pl.BlockSpec((tm, D_TAIL), lambda i, kk: (i, 0)),
        out_shape=jax.ShapeDtypeStruct((m, D_TAIL), BF16),
        scratch_shapes=[pltpu.VMEM((D_TAIL, tk), BF16),
                        pltpu.VMEM((tm, D_TAIL), F32)],
        compiler_params=_params(("parallel", "arbitrary")),
        name="in_proj_tail",
    )(x, *([wt] * TAIL_PARTS))


def _matmul_residual_kernel(x_ref, w_ref, r_ref, o_ref):
    o_ref[...] = r_ref[...] + jnp.dot(x_ref[...], w_ref[...].astype(BF16),
                                      preferred_element_type=F32)


def _matmul_residual(x, w, layer, res, tm, tn):
    m, k = x.shape
    n = w.shape[2]
    return pl.pallas_call(
        _matmul_residual_kernel,
        grid=(m // tm, n // tn),
        in_specs=[pl.BlockSpec((tm, k), lambda i, j: (i, 0)),
                  pl.BlockSpec((None, k, tn), lambda i, j: (layer, 0, j)),
                  pl.BlockSpec((tm, tn), lambda i, j: (i, j))],
        out_specs=pl.BlockSpec((tm, tn), lambda i, j: (i, j)),
        out_shape=jax.ShapeDtypeStruct((m, n), F32),
        compiler_params=_params(("parallel", "parallel")),
        name="out_proj",
    )(x, w, res)


def _gmlp_kernel(u_ref, v_ref, z_ref, gv_ref, w_ref, bias_ref, go_ref, o_ref,
                 y_ref):
    tm = u_ref.shape[0]
    t_idx = lax.broadcasted_iota(jnp.int32, (CHUNK, CHUNK), 0)
    s_idx = lax.broadcasted_iota(jnp.int32, (CHUNK, CHUNK), 1)
    causal = s_idx <= t_idx

    def chunk(c, carry):
        rows = pl.ds(pl.multiple_of(c * CHUNK, CHUNK), CHUNK)
        ssq = jnp.zeros((CHUNK, 1), F32)
        for g in range(A_GROUPS):
            cols = slice(g * A_CH, (g + 1) * A_CH)
            vg = _gelu(v_ref[rows, cols].astype(F32))
            vn = vg * _rms_scale(vg, A_CH) * gv_ref[:, cols]
            w = jnp.where(causal, w_ref[g], 0.0).astype(BF16)
            sv = jnp.dot(w, vn.astype(BF16), preferred_element_type=F32)
            sv = sv + bias_ref[:, cols]
            y = _gelu(u_ref[rows, cols].astype(F32)) * sv
            ssq = ssq + jnp.sum(y * y, axis=-1, keepdims=True)
            y_ref[:, cols] = y
        r = lax.rsqrt(ssq * (1.0 / A_WIDTH) + EPS)
        z = z_ref[rows, :].astype(F32)
        o_ref[rows, :] = (y_ref[...] * r * go_ref[...] * _silu(z)).astype(o_ref.dtype)
        return carry

    lax.fori_loop(0, tm // CHUNK, chunk, 0)


def _gmlp(proj, g_v, w_s, b_s, g_out_a, tm=512):
    m = proj.shape[0]
    bias = jnp.repeat(b_s.T, A_CH, axis=1)
    col = lambda off: (lambda i: (i, off // A_WIDTH))
    return pl.pallas_call(
        _gmlp_kernel,
        grid=(m // tm,),
        in_specs=[pl.BlockSpec((tm, A_WIDTH), col(OFF_U)),
                  pl.BlockSpec((tm, A_WIDTH), col(OFF_V)),
                  pl.BlockSpec((tm, A_WIDTH), col(OFF_ZA)),
                  pl.BlockSpec((1, A_WIDTH), lambda i: (0, 0)),
                  pl.BlockSpec((A_GROUPS, CHUNK, CHUNK), lambda i: (0, 0, 0)),
                  pl.BlockSpec((CHUNK, A_WIDTH), lambda i: (0, 0)),
                  pl.BlockSpec((1, A_WIDTH), lambda i: (0, 0))],
        out_specs=pl.BlockSpec((tm, A_WIDTH), lambda i: (i, 0)),
        out_shape=jax.ShapeDtypeStruct((m, A_WIDTH), BF16),
        scratch_shapes=[pltpu.VMEM((CHUNK, A_WIDTH), F32)],
        compiler_params=_params(("parallel",)),
        name="gmlp",
    )(proj, proj, proj, g_v.reshape(1, A_WIDTH), w_s, bias,
      g_out_a.reshape(1, A_WIDTH))


SB_T = 128
SB_SUB = 128
SB_W = 2 * SB_SUB
SB_HG = 6
SB_ZERO_WEIGHT_LOG = -105.0


def _sb_kernel(q_ref, k_ref, v_ref, o_ref, acc_ref, carry_ref):
    t, w, sub = SB_T, SB_W, SB_SUB
    i = pl.program_id(2)
    scale = HEAD_DIM ** -0.5
    q_pos = i * t + lax.broadcasted_iota(jnp.int32, (t, w), 0)
    k_off = lax.broadcasted_iota(jnp.int32, (t, w), 1)
    row = lax.broadcasted_iota(jnp.int32, (sub, sub), 0)
    col = lax.broadcasted_iota(jnp.int32, (sub, sub), 1)
    suffix = (row > col).astype(BF16)

    acc_ref[...] = jnp.zeros_like(acc_ref)
    carry_ref[...] = jnp.zeros_like(carry_ref)

    def window(start, bound):
        valid = (start + k_off) < jnp.minimum(q_pos, bound)
        ks = pl.ds(pl.multiple_of(start, sub), w)
        log_beta, log_keep, pieces = [], [], []
        for h in range(SB_HG):
            hc = slice(h * HEAD_DIM, (h + 1) * HEAD_DIM)
            z = lax.dot_general(q_ref[0, :, hc], k_ref[0, ks, hc],
                                (((1,), (1,)), ((), ())),
                                preferred_element_type=F32) * scale
            sp = jnp.maximum(z, 0.0) + jnp.log(1.0 + jnp.exp(-jnp.abs(z)))
            lk = jnp.where(valid, -sp, 0.0)
            log_beta.append(z - sp)
            log_keep.append(lk)
            hi = lk.astype(BF16)
            lo = (lk - hi.astype(F32)).astype(BF16)
            pieces += [hi[:, :sub], lo[:, :sub], hi[:, sub:], lo[:, sub:]]
        local = jnp.dot(jnp.concatenate(pieces, axis=0), suffix,
                        preferred_element_type=F32)
        worst = None
        for h in range(SB_HG):
            hc = slice(h * HEAD_DIM, (h + 1) * HEAD_DIM)
            base = 4 * h * t
            lower = local[base:base + t] + local[base + t:base + 2 * t]
            upper = (local[base + 2 * t:base + 3 * t]
                     + local[base + 3 * t:base + 4 * t])
            lk = log_keep[h]
            upper_total = jnp.sum(lk[:, sub:], axis=-1, keepdims=True)
            lower_total = jnp.sum(lk[:, :sub], axis=-1, keepdims=True)
            carry = carry_ref[h]
            after = jnp.concatenate(
                [lower + (upper_total + carry), upper + carry], axis=1)
            a = jnp.where(valid, jnp.exp(log_beta[h] + after), 0.0)
            acc_ref[h] += jnp.dot(a.astype(BF16), v_ref[0, ks, hc],
                                  preferred_element_type=F32)
            carry = carry + (upper_total + lower_total)
            carry_ref[h] = carry
            top = jnp.max(carry)
            worst = top if worst is None else jnp.maximum(worst, top)
        return worst

    first = jnp.maximum(i - 1, 0) * sub
    worst = window(first, jnp.int32(2 ** 30))

    def more(state):
        start, worst = state
        return jnp.logical_and(start > 0, worst > SB_ZERO_WEIGHT_LOG)

    def step(state):
        start, _ = state
        nxt = jnp.maximum(start - w, 0)
        return nxt, window(nxt, start)

    lax.while_loop(more, step, (first, worst))
    for h in range(SB_HG):
        o_ref[0, :, h * HEAD_DIM:(h + 1) * HEAD_DIM] = acc_ref[h].astype(o_ref.dtype)


def _stick_breaking(proj3):
    b, s, _ = proj3.shape
    t, gw = SB_T, SB_HG * HEAD_DIM
    blk = lambda off: off // gw
    return pl.pallas_call(
        _sb_kernel,
        grid=(b, HEADS // SB_HG, s // t),
        in_specs=[
            pl.BlockSpec((1, t, gw), lambda bi, g, i: (bi, i, blk(OFF_QB) + g)),
            pl.BlockSpec((1, s, gw), lambda bi, g, i: (bi, 0, blk(OFF_KB) + g)),
            pl.BlockSpec((1, s, gw), lambda bi, g, i: (bi, 0, blk(OFF_VB) + g)),
        ],
        out_specs=pl.BlockSpec((1, t, gw), lambda bi, g, i: (bi, i, g)),
        out_shape=jax.ShapeDtypeStruct((b, s, B_WIDTH), BF16),
        scratch_shapes=[pltpu.VMEM((SB_HG, t, HEAD_DIM), F32),
                        pltpu.VMEM((SB_HG, t, 1), F32)],
        compiler_params=_params(("parallel", "parallel", "arbitrary")),
        name="stick_breaking",
    )(proj3, proj3, proj3)


def _rope256(x, c_ref, sa_ref, sb_ref):
    return (x * c_ref[...]
            + pltpu.roll(x, ROPE_HALF, axis=1) * sa_ref[...]
            + pltpu.roll(x, C_QK - ROPE_HALF, axis=1) * sb_ref[...])


def _mla_q_kernel(cq_ref, g_ref, w_ref, c_ref, sa_ref, sb_ref, o_ref):
    scale = (C_NOPE + C_ROPE) ** -0.5
    x = cq_ref[...].astype(F32)
    xn = (x * _rms_scale(x, Q_LORA) * g_ref[...]).astype(BF16)
    q = jnp.dot(xn, w_ref[...], preferred_element_type=F32)
    for h in range(HEADS):
        cols = slice(h * C_QK, (h + 1) * C_QK)
        o_ref[:, cols] = (_rope256(q[:, cols], c_ref, sa_ref, sb_ref)
                          * scale).astype(o_ref.dtype)


def _mla_kv_kernel(ckv_ref, kr_ref, g_ref, w_ref, c_ref, sa_ref, sb_ref,
                   k_ref, v_ref):
    x = ckv_ref[...].astype(F32)
    xn = (x * _rms_scale(x, KV_LORA) * g_ref[...]).astype(BF16)
    kv = jnp.dot(xn, w_ref[...], preferred_element_type=F32)
    kr = _rope256(kr_ref[...].astype(F32), c_ref, sa_ref, sb_ref)
    for h in range(HEADS):
        cols = slice(h * C_QK, (h + 1) * C_QK)
        k_ref[:, cols] = (kv[:, cols] + kr).astype(k_ref.dtype)
    v_ref[...] = kv[:, HEADS * C_QK:].astype(v_ref.dtype)


def _mla_project(proj, g_q, g_kv, w_uq_p, w_ukv_p, tabs, tm=512):
    m = proj.shape[0]
    tab_spec = pl.BlockSpec((tm, C_QK), lambda i: (i, 0))
    q_cat = pl.pallas_call(
        _mla_q_kernel,
        grid=(m // tm,),
        in_specs=[pl.BlockSpec((tm, Q_LORA), lambda i: (i, TAIL_CQ // Q_LORA)),
                  pl.BlockSpec((1, Q_LORA), lambda i: (0, 0)),
                  pl.BlockSpec((Q_LORA, HEADS * C_QK), lambda i: (0, 0)),
                  tab_spec, tab_spec, tab_spec],
        out_specs=pl.BlockSpec((tm, HEADS * C_QK), lambda i: (i, 0)),
        out_shape=jax.ShapeDtypeStruct((m, HEADS * C_QK), BF16),
        compiler_params=_params(("parallel",)),
        name="mla_q",
    )(proj, g_q.reshape(1, Q_LORA), w_uq_p, *tabs)
    n_kv = HEADS * C_QK + C_WIDTH
    k_cat, v = pl.pallas_call(
        _mla_kv_kernel,
        grid=(m // tm,),
        in_specs=[pl.BlockSpec((tm, KV_LORA), lambda i: (i, TAIL_CKV // KV_LORA)),
                  pl.BlockSpec((tm, C_QK), lambda i: (i, TAIL_KR // C_QK)),
                  pl.BlockSpec((1, KV_LORA), lambda i: (0, 0)),
                  pl.BlockSpec((KV_LORA, n_kv), lambda i: (0, 0)),
                  tab_spec, tab_spec, tab_spec],
        out_specs=[pl.BlockSpec((tm, HEADS * C_QK), lambda i: (i, 0)),
                   pl.BlockSpec((tm, C_WIDTH), lambda i: (i, 0))],
        out_shape=[jax.ShapeDtypeStruct((m, HEADS * C_QK), BF16),
                   jax.ShapeDtypeStruct((m, C_WIDTH), BF16)],
        compiler_params=_params(("parallel",)),
        name="mla_kv",
    )(proj, proj, g_kv.reshape(1, KV_LORA), w_ukv_p, *tabs)
    return q_cat, k_cat, v


MLA_T = 256
MLA_W = 512
MLA_HG = 4


def _mla_attn_kernel(q_ref, k_ref, v_ref, o_ref, m_ref, l_ref, acc_ref):
    t, w = MLA_T, MLA_W
    i = pl.program_id(2)
    q_pos = i * t + lax.broadcasted_iota(jnp.int32, (t, w), 0)
    k_off = lax.broadcasted_iota(jnp.int32, (t, w), 1)

    m_ref[...] = jnp.full_like(m_ref, -jnp.inf)
    l_ref[...] = jnp.zeros_like(l_ref)
    acc_ref[...] = jnp.zeros_like(acc_ref)

    def scores(j):
        ks = pl.ds(pl.multiple_of(j * w, w), w)
        return tuple(
            lax.dot_general(q_ref[0, :, h * C_QK:(h + 1) * C_QK],
                            k_ref[0, ks, h * C_QK:(h + 1) * C_QK],
                            (((1,), (1,)), ((), ())),
                            preferred_element_type=F32)
            for h in range(MLA_HG))

    def softmax_pv(j, s_all, masked):
        start = pl.multiple_of(j * w, w)
        ks = pl.ds(start, w)
        for h in range(MLA_HG):
            s = s_all[h]
            if masked:
                s = jnp.where((start + k_off) <= q_pos, s, -jnp.inf)
            m_old = m_ref[h]
            m_new = jnp.maximum(m_old, jnp.max(s, axis=-1, keepdims=True))
            alpha = jnp.exp(m_old - m_new)
            p = jnp.exp(s - m_new)
            l_ref[h] = alpha * l_ref[h] + jnp.sum(p, axis=-1, keepdims=True)
            acc_ref[h] = alpha * acc_ref[h] + jnp.dot(
                p.astype(BF16), v_ref[0, ks, h * HEAD_DIM:(h + 1) * HEAD_DIM],
                preferred_element_type=F32)
            m_ref[h] = m_new

    last = (i * t) // w

    def body(j, s_cur):
        s_next = scores(j + 1)
        softmax_pv(j, s_cur, False)
        return s_next

    s_last = lax.fori_loop(0, last, body, scores(0))
    softmax_pv(last, s_last, True)
    for h in range(MLA_HG):
        o_ref[0, :, h * HEAD_DIM:(h + 1) * HEAD_DIM] = (
            acc_ref[h] / l_ref[h]).astype(o_ref.dtype)


def _mla_attention(q_cat, k_cat, v, b, s):
    t, hg = MLA_T, MLA_HG
    q3 = q_cat.reshape(b, s, HEADS * C_QK)
    k3 = k_cat.reshape(b, s, HEADS * C_QK)
    v3 = v.reshape(b, s, C_WIDTH)
    return pl.pallas_call(
        _mla_attn_kernel,
        grid=(b, HEADS // hg, s // t),
        in_specs=[pl.BlockSpec((1, t, hg * C_QK), lambda bi, g, i: (bi, i, g)),
                  pl.BlockSpec((1, s, hg * C_QK), lambda bi, g, i: (bi, 0, g)),
                  pl.BlockSpec((1, s, hg * HEAD_DIM), lambda bi, g, i: (bi, 0, g))],
        out_specs=pl.BlockSpec((1, t, hg * HEAD_DIM), lambda bi, g, i: (bi, i, g)),
        out_shape=jax.ShapeDtypeStruct((b, s, C_WIDTH), BF16),
        scratch_shapes=[pltpu.VMEM((hg, t, 1), F32),
                        pltpu.VMEM((hg, t, 1), F32),
                        pltpu.VMEM((hg, t, HEAD_DIM), F32)],
        compiler_params=_params(("parallel", "parallel", "arbitrary")),
        name="mla_attention",
    )(q3, k3, v3)


def _gate_kernel(ya_ref, yb_ref, zb_ref, yc_ref, zc_ref, g_ref, o_ref):
    o_ref[:, :A_WIDTH] = ya_ref[...]
    lo = A_WIDTH
    for y_ref, z_ref, width in ((yb_ref, zb_ref, B_WIDTH), (yc_ref, zc_ref, C_WIDTH)):
        y = y_ref[...].astype(F32)
        z = z_ref[...].astype(F32)
        g = g_ref[:, lo:lo + width]
        o_ref[:, lo:lo + width] = (y * _rms_scale(y, width) * g
                                   * _silu(z)).astype(o_ref.dtype)
        lo += width


def _gate(ya, yb, yc, main, tail, g_out, tm=512):
    m = ya.shape[0]
    return pl.pallas_call(
        _gate_kernel,
        grid=(m // tm,),
        in_specs=[pl.BlockSpec((tm, A_WIDTH), lambda i: (i, 0)),
                  pl.BlockSpec((tm, B_WIDTH), lambda i: (i, 0)),
                  pl.BlockSpec((tm, B_WIDTH), lambda i: (i, OFF_ZB // B_WIDTH)),
                  pl.BlockSpec((tm, C_WIDTH), lambda i: (i, 0)),
                  pl.BlockSpec((tm, C_WIDTH), lambda i: (i, TAIL_ZC // C_WIDTH)),
                  pl.BlockSpec((1, D_MODEL), lambda i: (0, 0))],
        out_specs=pl.BlockSpec((tm, D_MODEL), lambda i: (i, 0)),
        out_shape=jax.ShapeDtypeStruct((m, D_MODEL), BF16),
        compiler_params=_params(("parallel",)),
        name="gate",
    )(ya, yb, main, yc, tail, g_out.reshape(1, D_MODEL))


def _prep_w_uq(w):
    w = w.reshape(Q_LORA, HEADS, C_NOPE + C_ROPE)
    w = jnp.pad(w, ((0, 0), (0, 0), (0, C_QK - C_NOPE - C_ROPE)))
    return w.reshape(Q_LORA, HEADS * C_QK).astype(BF16)


def _prep_w_ukv(w):
    w = w.reshape(KV_LORA, HEADS, C_NOPE + HEAD_DIM)
    wk = jnp.pad(w[:, :, :C_NOPE], ((0, 0), (0, 0), (0, C_QK - C_NOPE)))
    wv = w[:, :, C_NOPE:]
    return jnp.concatenate([wk.reshape(KV_LORA, HEADS * C_QK),
                            wv.reshape(KV_LORA, C_WIDTH)], axis=1).astype(BF16)


def _rope_tables(positions):
    inv_freq = 1.0 / (ROPE_THETA ** (jnp.arange(0, C_ROPE, 2, dtype=F32) / C_ROPE))
    ang = positions.astype(F32).reshape(-1, 1) * inv_freq
    cos, sin = jnp.cos(ang), jnp.sin(ang)
    m = ang.shape[0]
    pad = C_QK - C_NOPE - C_ROPE
    c = jnp.concatenate([jnp.ones((m, C_NOPE), F32), cos, cos,
                         jnp.zeros((m, pad), F32)], axis=1)
    sa = jnp.concatenate([jnp.zeros((m, C_NOPE + ROPE_HALF), F32), sin,
                          jnp.zeros((m, pad), F32)], axis=1)
    sb = jnp.concatenate([jnp.zeros((m, C_NOPE), F32), -sin,
                          jnp.zeros((m, ROPE_HALF + pad), F32)], axis=1)
    return c, sa, sb


def kernel(x, positions, g_pre, w_in, a_g_v, a_w_s, a_b_s, c_g_q, c_g_kv,
           c_w_uq, c_w_ukv, g_out, w_out, g_final):
    b, s, d = x.shape
    m = b * s
    depth = w_in.shape[0]
    tabs = _rope_tables(positions)
    w_in_t = jnp.swapaxes(w_in, 1, 2)
    h = x.reshape(m, d)
    for l in range(depth):
        hn = _rmsnorm(h, g_pre[l], BF16)
        main = _in_proj_main(hn, w_in_t, l, tm=1024, tn=512)
        tail = _in_proj_tail(hn, w_in_t, l, tm=1024, tk=512)
        ya = _gmlp(main, a_g_v[l], a_w_s[l], a_b_s[l], g_out[l, :A_WIDTH])
        yb = _stick_breaking(main.reshape(b, s, D_MAIN)).reshape(m, B_WIDTH)
        q_cat, k_cat, v = _mla_project(tail, c_g_q[l], c_g_kv[l],
                                       _prep_w_uq(c_w_uq[l]),
                                       _prep_w_ukv(c_w_ukv[l]), tabs)
        yc = _mla_attention(q_cat, k_cat, v, b, s).reshape(m, C_WIDTH)
        y = _gate(ya, yb, yc, main, tail, g_out[l])
        h = _matmul_residual(y, w_out, l, h, tm=1024, tn=512)
    return _rmsnorm(h, g_final, F32).reshape(b, s, d)
```

```python
import functools
import math

import jax
import jax.numpy as jnp
from jax import lax
from jax.experimental import pallas as pl
from jax.experimental.pallas import tpu as pltpu

F32 = jnp.float32
BF16 = jnp.bfloat16

D_MODEL = 4096
A_GROUPS = 8
A_CH = 128
A_WIDTH = A_GROUPS * A_CH
CHUNK = 128
HEADS = 12
HEAD_DIM = 128
B_WIDTH = HEADS * HEAD_DIM
C_NOPE = 128
C_ROPE = 64
C_WIDTH = HEADS * HEAD_DIM
Q_LORA = 768
KV_LORA = 512
ROPE_THETA = 10000.0
EPS = 1e-6

C_QK = 256
ROPE_HALF = C_ROPE // 2

OFF_U = 0
OFF_V = OFF_U + A_WIDTH
OFF_ZA = OFF_V + A_WIDTH
OFF_QB = OFF_ZA + A_WIDTH
OFF_KB = OFF_QB + B_WIDTH
OFF_VB = OFF_KB + B_WIDTH
OFF_ZB = OFF_VB + B_WIDTH
D_MAIN = OFF_ZB + B_WIDTH
REF_CQ = D_MAIN
REF_CKV = REF_CQ + Q_LORA
REF_KR = REF_CKV + KV_LORA
REF_ZC = REF_KR + C_ROPE
TAIL_ZC = 0
TAIL_CKV = TAIL_ZC + C_WIDTH
TAIL_KR = TAIL_CKV + KV_LORA
TAIL_CQ = TAIL_KR + C_QK
D_TAIL = TAIL_CQ + Q_LORA

VMEM_LIMIT_BYTES = 56 * 1024 * 1024


def _params(sem):
    return pltpu.CompilerParams(dimension_semantics=sem,
                                vmem_limit_bytes=VMEM_LIMIT_BYTES)


def _gelu(x):
    c = math.sqrt(2.0 / math.pi)
    return x * (0.5 * (1.0 + jnp.tanh(c * (x + 0.044715 * (x * x * x)))))


def _silu(z):
    return z / (1.0 + jnp.exp(-z))


def _rms_scale(x, width):
    return lax.rsqrt(jnp.sum(x * x, axis=-1, keepdims=True) * (1.0 / width) + EPS)


def _rmsnorm_kernel(x_ref, g_ref, o_ref):
    x = x_ref[...]
    o_ref[...] = (x * _rms_scale(x, x.shape[-1]) * g_ref[...]).astype(o_ref.dtype)


def _rmsnorm(x, g, out_dtype, tm=256):
    m, d = x.shape
    return pl.pallas_call(
        _rmsnorm_kernel,
        grid=(m // tm,),
        in_specs=[pl.BlockSpec((tm, d), lambda i: (i, 0)),
                  pl.BlockSpec((1, d), lambda i: (0, 0))],
        out_specs=pl.BlockSpec((tm, d), lambda i: (i, 0)),
        out_shape=jax.ShapeDtypeStruct((m, d), out_dtype),
        compiler_params=_params(("parallel",)),
        name="rmsnorm",
    )(x, g.reshape(1, d))


_NT = (((1,), (1,)), ((), ()))


def _in_proj_main_kernel(x_ref, wt_ref, o_ref):
    o_ref[...] = lax.dot_general(x_ref[...], wt_ref[...].astype(BF16), _NT,
                                 preferred_element_type=F32).astype(o_ref.dtype)


def _in_proj_main(x, wt, layer, tm, tn):
    m, k = x.shape
    return pl.pallas_call(
        _in_proj_main_kernel,
        grid=(m // tm, D_MAIN // tn),
        in_specs=[pl.BlockSpec((tm, k), lambda i, j: (i, 0)),
                  pl.BlockSpec((None, tn, k), lambda i, j: (layer, j, 0))],
        out_specs=pl.BlockSpec((tm, tn), lambda i, j: (i, j)),
        out_shape=jax.ShapeDtypeStruct((m, D_MAIN), BF16),
        compiler_params=_params(("parallel", "parallel")),
        name="in_proj_main",
    )(x, wt)


TAIL_PART_ROWS = 576
TAIL_PARTS = 5
TAIL_COPIES = (
    (TAIL_ZC, REF_ZC - D_MAIN, C_WIDTH),
    (TAIL_CKV, REF_CKV - D_MAIN, KV_LORA),
    (TAIL_KR + C_NOPE, REF_KR - D_MAIN, C_ROPE),
    (TAIL_CQ, REF_CQ - D_MAIN, Q_LORA),
)
TAIL_ZERO_ROWS = ((TAIL_KR, C_NOPE),
                  (TAIL_KR + C_NOPE + C_ROPE, C_QK - C_NOPE - C_ROPE))


def _in_proj_tail_kernel(x_ref, *refs):
    w_refs = refs[:TAIL_PARTS]
    o_ref, wp_ref, acc_ref = refs[TAIL_PARTS:]
    kk = pl.program_id(1)

    @pl.when(kk == 0)
    def _():
        acc_ref[...] = jnp.zeros_like(acc_ref)

    for dst, src, rows in TAIL_COPIES:
        done = 0
        while done < rows:
            part, lo = divmod(src + done, TAIL_PART_ROWS)
            n = min(rows - done, TAIL_PART_ROWS - lo)
            wp_ref[dst + done:dst + done + n, :] = (
                w_refs[part][lo:lo + n, :].astype(BF16))
            done += n
    for dst, rows in TAIL_ZERO_ROWS:
        wp_ref[dst:dst + rows, :] = jnp.zeros((rows, wp_ref.shape[1]), BF16)

    acc_ref[...] += lax.dot_general(x_ref[...], wp_ref[...], _NT,
                                    preferred_element_type=F32)

    @pl.when(kk == pl.num_programs(1) - 1)
    def _():
        o_ref[...] = acc_ref[...].astype(o_ref.dtype)


def _in_proj_tail(x, wt, layer, tm, tk):
    m, k = x.shape
    first = D_MAIN // TAIL_PART_ROWS
    w_specs = [pl.BlockSpec((None, TAIL_PART_ROWS, tk),
                            lambda i, kk, p=p: (layer, first + p, kk))
               for p in range(TAIL_PARTS)]
    return pl.pallas_call(
        _in_proj_tail_kernel,
        grid=(m // tm, k // tk),
        in_specs=[pl.BlockSpec((tm, tk), lambda i, kk: (i, kk))] + w_specs,
        out_specs=pl.BlockSpec((tm, D_TAIL), lambda i, kk: (i, 0)),
        out_shape=jax.ShapeDtypeStruct((m, D_TAIL), BF16),
        scratch_shapes=[pltpu.VMEM((D_TAIL, tk), BF16),
                        pltpu.VMEM((tm, D_TAIL), F32)],
        compiler_params=_params(("parallel", "arbitrary")),
        name="in_proj_tail",
    )(x, *([wt] * TAIL_PARTS))


def _matmul_residual_kernel(x_ref, w_ref, r_ref, o_ref):
    o_ref[...] = r_ref[...] + jnp.dot(x_ref[...], w_ref[...].astype(BF16),
                                      preferred_element_type=F32)


def _matmul_residual(x, w, layer, res, tm, tn):
    m, k = x.shape
    n = w.shape[2]
    return pl.pallas_call(
        _matmul_residual_kernel,
        grid=(m // tm, n // tn),
        in_specs=[pl.BlockSpec((tm, k), lambda i, j: (i, 0)),
                  pl.BlockSpec((None, k, tn), lambda i, j: (layer, 0, j)),
                  pl.BlockSpec((tm, tn), lambda i, j: (i, j))],
        out_specs=pl.BlockSpec((tm, tn), lambda i, j: (i, j)),
        out_shape=jax.ShapeDtypeStruct((m, n), F32),
        compiler_params=_params(("parallel", "parallel")),
        name="out_proj",
    )(x, w, res)


def _gmlp_kernel(u_ref, v_ref, z_ref, gv_ref, w_ref, bias_ref, go_ref, o_ref,
                 y_ref):
    tm = u_ref.shape[0]
    t_idx = lax.broadcasted_iota(jnp.int32, (CHUNK, CHUNK), 0)
    s_idx = lax.broadcasted_iota(jnp.int32, (CHUNK, CHUNK), 1)
    causal = s_idx <= t_idx

    def chunk(c, carry):
        rows = pl.ds(pl.multiple_of(c * CHUNK, CHUNK), CHUNK)
        ssq = jnp.zeros((CHUNK, 1), F32)
        for g in range(A_GROUPS):
            cols = slice(g * A_CH, (g + 1) * A_CH)
            vg = _gelu(v_ref[rows, cols].astype(F32))
            vn = vg * _rms_scale(vg, A_CH) * gv_ref[:, cols]
            w = jnp.where(causal, w_ref[g], 0.0).astype(BF16)
            sv = jnp.dot(w, vn.astype(BF16), preferred_element_type=F32)
            sv = sv + bias_ref[:, cols]
            y = _gelu(u_ref[rows, cols].astype(F32)) * sv
            ssq = ssq + jnp.sum(y * y, axis=-1, keepdims=True)
            y_ref[:, cols] = y
        r = lax.rsqrt(ssq * (1.0 / A_WIDTH) + EPS)
        z = z_ref[rows, :].astype(F32)
        o_ref[rows, :] = (y_ref[...] * r * go_ref[...] * _silu(z)).astype(o_ref.dtype)
        return carry

    lax.fori_loop(0, tm // CHUNK, chunk, 0)


def _gmlp(proj, g_v, w_s, b_s, g_out_a, tm=512):
    m = proj.shape[0]
    bias = jnp.repeat(b_s.T, A_CH, axis=1)
    col = lambda off: (lambda i: (i, off // A_WIDTH))
    return pl.pallas_call(
        _gmlp_kernel,
        grid=(m // tm,),
        in_specs=[pl.BlockSpec((tm, A_WIDTH), col(OFF_U)),
                  pl.BlockSpec((tm, A_WIDTH), col(OFF_V)),
                  pl.BlockSpec((tm, A_WIDTH), col(OFF_ZA)),
                  pl.BlockSpec((1, A_WIDTH), lambda i: (0, 0)),
                  pl.BlockSpec((A_GROUPS, CHUNK, CHUNK), lambda i: (0, 0, 0)),
                  pl.BlockSpec((CHUNK, A_WIDTH), lambda i: (0, 0)),
                  pl.BlockSpec((1, A_WIDTH), lambda i: (0, 0))],
        out_specs=pl.BlockSpec((tm, A_WIDTH), lambda i: (i, 0)),
        out_shape=jax.ShapeDtypeStruct((m, A_WIDTH), BF16),
        scratch_shapes=[pltpu.VMEM((CHUNK, A_WIDTH), F32)],
        compiler_params=_params(("parallel",)),
        name="gmlp",
    )(proj, proj, proj, g_v.reshape(1, A_WIDTH), w_s, bias,
      g_out_a.reshape(1, A_WIDTH))


SB_T = 128
SB_SUB = 128
SB_NSUB = 3
SB_W = SB_NSUB * SB_SUB
SB_HG = 6
SB_ZERO_WEIGHT_LOG = -105.0


def _sb_kernel(q_ref, k_ref, v_ref, o_ref, acc_ref, carry_ref):
    t, w, sub = SB_T, SB_W, SB_SUB
    i = pl.program_id(2)
    scale = HEAD_DIM ** -0.5
    q_pos = i * t + lax.broadcasted_iota(jnp.int32, (t, w), 0)
    k_off = lax.broadcasted_iota(jnp.int32, (t, w), 1)
    row = lax.broadcasted_iota(jnp.int32, (sub, sub), 0)
    col = lax.broadcasted_iota(jnp.int32, (sub, sub), 1)
    suffix = (row > col).astype(BF16)

    acc_ref[...] = jnp.zeros_like(acc_ref)
    carry_ref[...] = jnp.zeros_like(carry_ref)

    def window(start, bound):
        valid = (start + k_off) < jnp.minimum(q_pos, bound)
        ks = pl.ds(pl.multiple_of(start, sub), w)
        log_beta, log_keep, pieces = [], [], []
        for h in range(SB_HG):
            hc = slice(h * HEAD_DIM, (h + 1) * HEAD_DIM)
            z = lax.dot_general(q_ref[0, :, hc], k_ref[0, ks, hc],
                                (((1,), (1,)), ((), ())),
                                preferred_element_type=F32) * scale
            sp = jnp.maximum(z, 0.0) + jnp.log(1.0 + jnp.exp(-jnp.abs(z)))
            lk = jnp.where(valid, -sp, 0.0)
            log_beta.append(z - sp)
            log_keep.append(lk)
            hi = lk.astype(BF16)
            lo = (lk - hi.astype(F32)).astype(BF16)
            for c in range(SB_NSUB):
                pieces += [hi[:, c * sub:(c + 1) * sub], lo[:, c * sub:(c + 1) * sub]]
        local = jnp.dot(jnp.concatenate(pieces, axis=0), suffix,
                        preferred_element_type=F32)
        worst = None
        for h in range(SB_HG):
            hc = slice(h * HEAD_DIM, (h + 1) * HEAD_DIM)
            lk = log_keep[h]
            running = carry_ref[h]
            after = [None] * SB_NSUB
            for c in reversed(range(SB_NSUB)):
                base = (h * SB_NSUB + c) * 2 * t
                after[c] = (local[base:base + t] + local[base + t:base + 2 * t]
                            + running)
                running = running + jnp.sum(lk[:, c * sub:(c + 1) * sub],
                                            axis=-1, keepdims=True)
            a = jnp.where(valid,
                          jnp.exp(log_beta[h] + jnp.concatenate(after, axis=1)),
                          0.0)
            acc_ref[h] += jnp.dot(a.astype(BF16), v_ref[0, ks, hc],
                                  preferred_element_type=F32)
            carry_ref[h] = running
            top = jnp.max(running)
            worst = top if worst is None else jnp.maximum(worst, top)
        return worst

    first = jnp.maximum(i + 1 - SB_NSUB, 0) * sub
    worst = window(first, jnp.int32(2 ** 30))

    def more(state):
        start, worst = state
        return jnp.logical_and(start > 0, worst > SB_ZERO_WEIGHT_LOG)

    def step(state):
        start, _ = state
        nxt = jnp.maximum(start - w, 0)
        return nxt, window(nxt, start)

    lax.while_loop(more, step, (first, worst))
    for h in range(SB_HG):
        o_ref[0, :, h * HEAD_DIM:(h + 1) * HEAD_DIM] = acc_ref[h].astype(o_ref.dtype)


def _stick_breaking(proj3):
    b, s, _ = proj3.shape
    t, gw = SB_T, SB_HG * HEAD_DIM
    blk = lambda off: off // gw
    return pl.pallas_call(
        _sb_kernel,
        grid=(b, HEADS // SB_HG, s // t),
        in_specs=[
            pl.BlockSpec((1, t, gw), lambda bi, g, i: (bi, i, blk(OFF_QB) + g)),
            pl.BlockSpec((1, s, gw), lambda bi, g, i: (bi, 0, blk(OFF_KB) + g)),
            pl.BlockSpec((1, s, gw), lambda bi, g, i: (bi, 0, blk(OFF_VB) + g)),
        ],
        out_specs=pl.BlockSpec((1, t, gw), lambda bi, g, i: (bi, i, g)),
        out_shape=jax.ShapeDtypeStruct((b, s, B_WIDTH), BF16),
        scratch_shapes=[pltpu.VMEM((SB_HG, t, HEAD_DIM), F32),
                        pltpu.VMEM((SB_HG, t, 1), F32)],
        compiler_params=_params(("parallel", "parallel", "arbitrary")),
        name="stick_breaking",
    )(proj3, proj3, proj3)


def _rope256(x, c_ref, sa_ref, sb_ref):
    return (x * c_ref[...]
            + pltpu.roll(x, ROPE_HALF, axis=1) * sa_ref[...]
            + pltpu.roll(x, C_QK - ROPE_HALF, axis=1) * sb_ref[...])


def _mla_q_kernel(cq_ref, g_ref, w_ref, c_ref, sa_ref, sb_ref, o_ref):
    scale = (C_NOPE + C_ROPE) ** -0.5
    x = cq_ref[...].astype(F32)
    xn = (x * _rms_scale(x, Q_LORA) * g_ref[...]).astype(BF16)
    q = jnp.dot(xn, w_ref[...], preferred_element_type=F32)
    for h in range(HEADS):
        cols = slice(h * C_QK, (h + 1) * C_QK)
        o_ref[:, cols] = (_rope256(q[:, cols], c_ref, sa_ref, sb_ref)
                          * scale).astype(o_ref.dtype)


VT_ONES = 16
VT_ROWS = HEAD_DIM + VT_ONES


def _mla_kv_kernel(ckv_ref, kr_ref, g_ref, wk_ref, wvt_ref, c_ref, sa_ref,
                   sb_ref, k_ref, vt_ref):
    tm = ckv_ref.shape[0]
    x = ckv_ref[...].astype(F32)
    xn = (x * _rms_scale(x, KV_LORA) * g_ref[...]).astype(BF16)
    kn = jnp.dot(xn, wk_ref[...], preferred_element_type=F32)
    kr = _rope256(kr_ref[...].astype(F32), c_ref, sa_ref, sb_ref)
    for h in range(HEADS):
        cols = slice(h * C_QK, (h + 1) * C_QK)
        k_ref[:, cols] = (kn[:, cols] + kr).astype(k_ref.dtype)
    vt = lax.dot_general(wvt_ref[...], xn, _NT, preferred_element_type=F32)
    for h in range(HEADS):
        vt_ref[h * VT_ROWS:h * VT_ROWS + HEAD_DIM, :] = (
            vt[h * HEAD_DIM:(h + 1) * HEAD_DIM, :].astype(vt_ref.dtype))
        vt_ref[h * VT_ROWS + HEAD_DIM:(h + 1) * VT_ROWS, :] = jnp.ones(
            (VT_ONES, tm), vt_ref.dtype)


def _mla_project(proj, g_q, g_kv, w_uq_p, w_k_p, w_vt_p, tabs, b, s, tm=512):
    m = proj.shape[0]
    tiles = s // tm
    tab_spec = pl.BlockSpec((tm, C_QK), lambda i: (i, 0))
    q_cat = pl.pallas_call(
        _mla_q_kernel,
        grid=(m // tm,),
        in_specs=[pl.BlockSpec((tm, Q_LORA), lambda i: (i, TAIL_CQ // Q_LORA)),
                  pl.BlockSpec((1, Q_LORA), lambda i: (0, 0)),
                  pl.BlockSpec((Q_LORA, HEADS * C_QK), lambda i: (0, 0)),
                  tab_spec, tab_spec, tab_spec],
        out_specs=pl.BlockSpec((tm, HEADS * C_QK), lambda i: (i, 0)),
        out_shape=jax.ShapeDtypeStruct((m, HEADS * C_QK), BF16),
        compiler_params=_params(("parallel",)),
        name="mla_q",
    )(proj, g_q.reshape(1, Q_LORA), w_uq_p, *tabs)
    k_cat, vt = pl.pallas_call(
        _mla_kv_kernel,
        grid=(m // tm,),
        in_specs=[pl.BlockSpec((tm, KV_LORA), lambda i: (i, TAIL_CKV // KV_LORA)),
                  pl.BlockSpec((tm, C_QK), lambda i: (i, TAIL_KR // C_QK)),
                  pl.BlockSpec((1, KV_LORA), lambda i: (0, 0)),
                  pl.BlockSpec((KV_LORA, HEADS * C_QK), lambda i: (0, 0)),
                  pl.BlockSpec((C_WIDTH, KV_LORA), lambda i: (0, 0)),
                  tab_spec, tab_spec, tab_spec],
        out_specs=[pl.BlockSpec((tm, HEADS * C_QK), lambda i: (i, 0)),
                   pl.BlockSpec((None, HEADS * VT_ROWS, tm),
                                lambda i: (i // tiles, 0, i % tiles))],
        out_shape=[jax.ShapeDtypeStruct((m, HEADS * C_QK), BF16),
                   jax.ShapeDtypeStruct((b, HEADS * VT_ROWS, s), BF16)],
        compiler_params=_params(("parallel",)),
        name="mla_kv",
    )(proj, proj, g_kv.reshape(1, KV_LORA), w_k_p, w_vt_p, *tabs)
    return q_cat, k_cat, vt


MLA_T = 256
MLA_W = 512
MLA_HG = 2


def _mla_attn_kernel(q_ref, k_ref, vt_ref, o_ref, m_ref, acc_ref):
    t, w = MLA_T, MLA_W
    i = pl.program_id(2)
    k_off = lax.broadcasted_iota(jnp.int32, (w, t), 0)
    q_pos = i * t + lax.broadcasted_iota(jnp.int32, (w, t), 1)

    m_ref[...] = jnp.full_like(m_ref, -jnp.inf)
    acc_ref[...] = jnp.zeros_like(acc_ref)

    def scores(j):
        ks = pl.ds(pl.multiple_of(j * w, w), w)
        return tuple(
            lax.dot_general(k_ref[0, ks, h * C_QK:(h + 1) * C_QK],
                            q_ref[0, :, h * C_QK:(h + 1) * C_QK], _NT,
                            preferred_element_type=F32)
            for h in range(MLA_HG))

    def softmax_pv(j, s_all, masked):
        start = pl.multiple_of(j * w, w)
        ks = pl.ds(start, w)
        for h in range(MLA_HG):
            s = s_all[h]
            if masked:
                s = jnp.where((start + k_off) <= q_pos, s, -jnp.inf)
            m_old = m_ref[h]
            m_new = jnp.maximum(m_old, jnp.max(s, axis=0, keepdims=True))
            alpha = jnp.exp(m_old - m_new)
            p = jnp.exp(s - m_new)
            acc_ref[h] = alpha * acc_ref[h] + jnp.dot(
                vt_ref[0, h * VT_ROWS:(h + 1) * VT_ROWS, ks], p.astype(BF16),
                preferred_element_type=F32)
            m_ref[h] = m_new

    last = (i * t) // w

    def body(j, s_cur):
        s_next = scores(j + 1)
        softmax_pv(j, s_cur, False)
        return s_next

    s_last = lax.fori_loop(0, last, body, scores(0))
    softmax_pv(last, s_last, True)
    for h in range(MLA_HG):
        acc = acc_ref[h]
        out_t = acc[:HEAD_DIM] / acc[HEAD_DIM:HEAD_DIM + 1]
        o_ref[0, :, h * HEAD_DIM:(h + 1) * HEAD_DIM] = out_t.T.astype(o_ref.dtype)


def _mla_attention(q_cat, k_cat, vt, b, s):
    t, hg = MLA_T, MLA_HG
    q3 = q_cat.reshape(b, s, HEADS * C_QK)
    k3 = k_cat.reshape(b, s, HEADS * C_QK)
    return pl.pallas_call(
        _mla_attn_kernel,
        grid=(b, HEADS // hg, s // t),
        in_specs=[pl.BlockSpec((1, t, hg * C_QK), lambda bi, g, i: (bi, i, g)),
                  pl.BlockSpec((1, s, hg * C_QK), lambda bi, g, i: (bi, 0, g)),
                  pl.BlockSpec((1, hg * VT_ROWS, s), lambda bi, g, i: (bi, g, 0))],
        out_specs=pl.BlockSpec((1, t, hg * HEAD_DIM), lambda bi, g, i: (bi, i, g)),
        out_shape=jax.ShapeDtypeStruct((b, s, C_WIDTH), BF16),
        scratch_shapes=[pltpu.VMEM((hg, 1, t), F32),
                        pltpu.VMEM((hg, VT_ROWS, t), F32)],
        compiler_params=_params(("parallel", "parallel", "arbitrary")),
        name="mla_attention",
    )(q3, k3, vt)


def _gate_kernel(ya_ref, yb_ref, zb_ref, yc_ref, zc_ref, g_ref, o_ref):
    o_ref[:, :A_WIDTH] = ya_ref[...]
    lo = A_WIDTH
    for y_ref, z_ref, width in ((yb_ref, zb_ref, B_WIDTH), (yc_ref, zc_ref, C_WIDTH)):
        y = y_ref[...].astype(F32)
        z = z_ref[...].astype(F32)
        g = g_ref[:, lo:lo + width]
        o_ref[:, lo:lo + width] = (y * _rms_scale(y, width) * g
                                   * _silu(z)).astype(o_ref.dtype)
        lo += width


def _gate(ya, yb, yc, main, tail, g_out, tm=512):
    m = ya.shape[0]
    return pl.pallas_call(
        _gate_kernel,
        grid=(m // tm,),
        in_specs=[pl.BlockSpec((tm, A_WIDTH), lambda i: (i, 0)),
                  pl.BlockSpec((tm, B_WIDTH), lambda i: (i, 0)),
                  pl.BlockSpec((tm, B_WIDTH), lambda i: (i, OFF_ZB // B_WIDTH)),
                  pl.BlockSpec((tm, C_WIDTH), lambda i: (i, 0)),
                  pl.BlockSpec((tm, C_WIDTH), lambda i: (i, TAIL_ZC // C_WIDTH)),
                  pl.BlockSpec((1, D_MODEL), lambda i: (0, 0))],
        out_specs=pl.BlockSpec((tm, D_MODEL), lambda i: (i, 0)),
        out_shape=jax.ShapeDtypeStruct((m, D_MODEL), BF16),
        compiler_params=_params(("parallel",)),
        name="gate",
    )(ya, yb, main, yc, tail, g_out.reshape(1, D_MODEL))


def _prep_w_uq(w):
    w = w.reshape(Q_LORA, HEADS, C_NOPE + C_ROPE)
    w = jnp.pad(w, ((0, 0), (0, 0), (0, C_QK - C_NOPE - C_ROPE)))
    return w.reshape(Q_LORA, HEADS * C_QK).astype(BF16)


def _prep_w_ukv(w):
    w = w.reshape(KV_LORA, HEADS, C_NOPE + HEAD_DIM)
    wk = jnp.pad(w[:, :, :C_NOPE], ((0, 0), (0, 0), (0, C_QK - C_NOPE)))
    wv_t = w[:, :, C_NOPE:].reshape(KV_LORA, C_WIDTH).T
    return wk.reshape(KV_LORA, HEADS * C_QK).astype(BF16), wv_t.astype(BF16)


def _rope_tables(positions):
    inv_freq = 1.0 / (ROPE_THETA ** (jnp.arange(0, C_ROPE, 2, dtype=F32) / C_ROPE))
    ang = positions.astype(F32).reshape(-1, 1) * inv_freq
    cos, sin = jnp.cos(ang), jnp.sin(ang)
    m = ang.shape[0]
    pad = C_QK - C_NOPE - C_ROPE
    c = jnp.concatenate([jnp.ones((m, C_NOPE), F32), cos, cos,
                         jnp.zeros((m, pad), F32)], axis=1)
    sa = jnp.concatenate([jnp.zeros((m, C_NOPE + ROPE_HALF), F32), sin,
                          jnp.zeros((m, pad), F32)], axis=1)
    sb = jnp.concatenate([jnp.zeros((m, C_NOPE), F32), -sin,
                          jnp.zeros((m, ROPE_HALF + pad), F32)], axis=1)
    return c, sa, sb


def kernel(x, positions, g_pre, w_in, a_g_v, a_w_s, a_b_s, c_g_q, c_g_kv,
           c_w_uq, c_w_ukv, g_out, w_out, g_final):
    b, s, d = x.shape
    m = b * s
    depth = w_in.shape[0]
    tabs = _rope_tables(positions)
    w_in_t = jnp.swapaxes(w_in, 1, 2)
    h = x.reshape(m, d)
    for l in range(depth):
        hn = _rmsnorm(h, g_pre[l], BF16)
        main = _in_proj_main(hn, w_in_t, l, tm=1024, tn=512)
        tail = _in_proj_tail(hn, w_in_t, l, tm=1024, tk=512)
        ya = _gmlp(main, a_g_v[l], a_w_s[l], a_b_s[l], g_out[l, :A_WIDTH])
        yb = _stick_breaking(main.reshape(b, s, D_MAIN)).reshape(m, B_WIDTH)
        w_k_p, w_vt_p = _prep_w_ukv(c_w_ukv[l])
        q_cat, k_cat, vt = _mla_project(tail, c_g_q[l], c_g_kv[l],
                                        _prep_w_uq(c_w_uq[l]), w_k_p, w_vt_p,
                                        tabs, b, s)
        yc = _mla_attention(q_cat, k_cat, vt, b, s).reshape(m, C_WIDTH)
        y = _gate(ya, yb, yc, main, tail, g_out[l])
        h = _matmul_residual(y, w_out, l, h, tm=1024, tn=512)
    return _rmsnorm(h, g_final, F32).reshape(b, s, d)
```

```python
import functools
import math

import jax
import jax.numpy as jnp
from jax import lax
from jax.experimental import pallas as pl
from jax.experimental.pallas import tpu as pltpu

F32 = jnp.float32
BF16 = jnp.bfloat16

D_MODEL = 4096
A_GROUPS = 8
A_CH = 128
A_WIDTH = A_GROUPS * A_CH
CHUNK = 128
HEADS = 12
HEAD_DIM = 128
B_WIDTH = HEADS * HEAD_DIM
C_NOPE = 128
C_ROPE = 64
C_WIDTH = HEADS * HEAD_DIM
Q_LORA = 768
KV_LORA = 512
ROPE_THETA = 10000.0
EPS = 1e-6

C_QK = 256
ROPE_HALF = C_ROPE // 2

OFF_U = 0
OFF_V = OFF_U + A_WIDTH
OFF_ZA = OFF_V + A_WIDTH
OFF_QB = OFF_ZA + A_WIDTH
OFF_KB = OFF_QB + B_WIDTH
OFF_VB = OFF_KB + B_WIDTH
OFF_ZB = OFF_VB + B_WIDTH
D_MAIN = OFF_ZB + B_WIDTH
REF_CQ = D_MAIN
REF_CKV = REF_CQ + Q_LORA
REF_KR = REF_CKV + KV_LORA
REF_ZC = REF_KR + C_ROPE
TAIL_ZC = 0
TAIL_CKV = TAIL_ZC + C_WIDTH
TAIL_KR = TAIL_CKV + KV_LORA
TAIL_CQ = TAIL_KR + C_QK
D_TAIL = TAIL_CQ + Q_LORA

VMEM_LIMIT_BYTES = 56 * 1024 * 1024


def _params(sem):
    return pltpu.CompilerParams(dimension_semantics=sem,
                                vmem_limit_bytes=VMEM_LIMIT_BYTES)


def _gelu(x):
    c = math.sqrt(2.0 / math.pi)
    return x * (0.5 * (1.0 + jnp.tanh(c * (x + 0.044715 * (x * x * x)))))


def _silu(z):
    return z / (1.0 + jnp.exp(-z))


def _rms_scale(x, width):
    return lax.rsqrt(jnp.sum(x * x, axis=-1, keepdims=True) * (1.0 / width) + EPS)


def _rmsnorm_kernel(x_ref, g_ref, o_ref):
    x = x_ref[...]
    o_ref[...] = (x * _rms_scale(x, x.shape[-1]) * g_ref[...]).astype(o_ref.dtype)


def _rmsnorm(x, g, out_dtype, tm=256):
    m, d = x.shape
    return pl.pallas_call(
        _rmsnorm_kernel,
        grid=(m // tm,),
        in_specs=[pl.BlockSpec((tm, d), lambda i: (i, 0)),
                  pl.BlockSpec((1, d), lambda i: (0, 0))],
        out_specs=pl.BlockSpec((tm, d), lambda i: (i, 0)),
        out_shape=jax.ShapeDtypeStruct((m, d), out_dtype),
        compiler_params=_params(("parallel",)),
        name="rmsnorm",
    )(x, g.reshape(1, d))


_NT = (((1,), (1,)), ((), ()))


def _in_proj_main_kernel(x_ref, wt_ref, o_ref):
    o_ref[...] = lax.dot_general(x_ref[...], wt_ref[...].astype(BF16), _NT,
                                 preferred_element_type=F32).astype(o_ref.dtype)


def _in_proj_main(x, wt, layer, tm, tn):
    m, k = x.shape
    return pl.pallas_call(
        _in_proj_main_kernel,
        grid=(m // tm, D_MAIN // tn),
        in_specs=[pl.BlockSpec((tm, k), lambda i, j: (i, 0)),
                  pl.BlockSpec((None, tn, k), lambda i, j: (layer, j, 0))],
        out_specs=pl.BlockSpec((tm, tn), lambda i, j: (i, j)),
        out_shape=jax.ShapeDtypeStruct((m, D_MAIN), BF16),
        compiler_params=_params(("parallel", "parallel")),
        name="in_proj_main",
    )(x, wt)


TAIL_PART_ROWS = 576
TAIL_PARTS = 5
TAIL_COPIES = (
    (TAIL_ZC, REF_ZC - D_MAIN, C_WIDTH),
    (TAIL_CKV, REF_CKV - D_MAIN, KV_LORA),
    (TAIL_KR + C_NOPE, REF_KR - D_MAIN, C_ROPE),
    (TAIL_CQ, REF_CQ - D_MAIN, Q_LORA),
)
TAIL_ZERO_ROWS = ((TAIL_KR, C_NOPE),
                  (TAIL_KR + C_NOPE + C_ROPE, C_QK - C_NOPE - C_ROPE))


def _in_proj_tail_kernel(x_ref, *refs):
    w_refs = refs[:TAIL_PARTS]
    o_ref, wp_ref, acc_ref = refs[TAIL_PARTS:]
    kk = pl.program_id(1)

    @pl.when(kk == 0)
    def _():
        acc_ref[...] = jnp.zeros_like(acc_ref)

    for dst, src, rows in TAIL_COPIES:
        done = 0
        while done < rows:
            part, lo = divmod(src + done, TAIL_PART_ROWS)
            n = min(rows - done, TAIL_PART_ROWS - lo)
            wp_ref[dst + done:dst + done + n, :] = (
                w_refs[part][lo:lo + n, :].astype(BF16))
            done += n
    for dst, rows in TAIL_ZERO_ROWS:
        wp_ref[dst:dst + rows, :] = jnp.zeros((rows, wp_ref.shape[1]), BF16)

    acc_ref[...] += lax.dot_general(x_ref[...], wp_ref[...], _NT,
                                    preferred_element_type=F32)

    @pl.when(kk == pl.num_programs(1) - 1)
    def _():
        o_ref[...] = acc_ref[...].astype(o_ref.dtype)


def _in_proj_tail(x, wt, layer, tm, tk):
    m, k = x.shape
    first = D_MAIN // TAIL_PART_ROWS
    w_specs = [pl.BlockSpec((None, TAIL_PART_ROWS, tk),
                            lambda i, kk, p=p: (layer, first + p, kk))
               for p in range(TAIL_PARTS)]
    return pl.pallas_call(
        _in_proj_tail_kernel,
        grid=(m // tm, k // tk),
        in_specs=[pl.BlockSpec((tm, tk), lambda i, kk: (i, kk))] + w_specs,
        out_specs=pl.BlockSpec((tm, D_TAIL), lambda i, kk: (i, 0)),
        out_shape=jax.ShapeDtypeStruct((m, D_TAIL), BF16),
        scratch_shapes=[pltpu.VMEM((D_TAIL, tk), BF16),
                        pltpu.VMEM((tm, D_TAIL), F32)],
        compiler_params=_params(("parallel", "arbitrary")),
        name="in_proj_tail",
    )(x, *([wt] * TAIL_PARTS))


def _matmul_residual_kernel(x_ref, w_ref, r_ref, o_ref):
    o_ref[...] = r_ref[...] + jnp.dot(x_ref[...], w_ref[...].astype(BF16),
                                      preferred_element_type=F32)


def _matmul_residual(x, w, layer, res, tm, tn):
    m, k = x.shape
    n = w.shape[2]
    return pl.pallas_call(
        _matmul_residual_kernel,
        grid=(m // tm, n // tn),
        in_specs=[pl.BlockSpec((tm, k), lambda i, j: (i, 0)),
                  pl.BlockSpec((None, k, tn), lambda i, j: (layer, 0, j)),
                  pl.BlockSpec((tm, tn), lambda i, j: (i, j))],
        out_specs=pl.BlockSpec((tm, tn), lambda i, j: (i, j)),
        out_shape=jax.ShapeDtypeStruct((m, n), F32),
        compiler_params=_params(("parallel", "parallel")),
        name="out_proj",
    )(x, w, res)


def _gmlp_kernel(u_ref, v_ref, z_ref, gv_ref, w_ref, bias_ref, go_ref, o_ref,
                 y_ref):
    tm = u_ref.shape[0]
    t_idx = lax.broadcasted_iota(jnp.int32, (CHUNK, CHUNK), 0)
    s_idx = lax.broadcasted_iota(jnp.int32, (CHUNK, CHUNK), 1)
    causal = s_idx <= t_idx

    def chunk(c, carry):
        rows = pl.ds(pl.multiple_of(c * CHUNK, CHUNK), CHUNK)
        ssq = jnp.zeros((CHUNK, 1), F32)
        for g in range(A_GROUPS):
            cols = slice(g * A_CH, (g + 1) * A_CH)
            vg = _gelu(v_ref[rows, cols].astype(F32))
            vn = vg * _rms_scale(vg, A_CH) * gv_ref[:, cols]
            w = jnp.where(causal, w_ref[g], 0.0).astype(BF16)
            sv = jnp.dot(w, vn.astype(BF16), preferred_element_type=F32)
            sv = sv + bias_ref[:, cols]
            y = _gelu(u_ref[rows, cols].astype(F32)) * sv
            ssq = ssq + jnp.sum(y * y, axis=-1, keepdims=True)
            y_ref[:, cols] = y
        r = lax.rsqrt(ssq * (1.0 / A_WIDTH) + EPS)
        z = z_ref[rows, :].astype(F32)
        o_ref[rows, :] = (y_ref[...] * r * go_ref[...] * _silu(z)).astype(o_ref.dtype)
        return carry

    lax.fori_loop(0, tm // CHUNK, chunk, 0)


def _gmlp(proj, g_v, w_s, b_s, g_out_a, tm=512):
    m = proj.shape[0]
    bias = jnp.repeat(b_s.T, A_CH, axis=1)
    col = lambda off: (lambda i: (i, off // A_WIDTH))
    return pl.pallas_call(
        _gmlp_kernel,
        grid=(m // tm,),
        in_specs=[pl.BlockSpec((tm, A_WIDTH), col(OFF_U)),
                  pl.BlockSpec((tm, A_WIDTH), col(OFF_V)),
                  pl.BlockSpec((tm, A_WIDTH), col(OFF_ZA)),
                  pl.BlockSpec((1, A_WIDTH), lambda i: (0, 0)),
                  pl.BlockSpec((A_GROUPS, CHUNK, CHUNK), lambda i: (0, 0, 0)),
                  pl.BlockSpec((CHUNK, A_WIDTH), lambda i: (0, 0)),
                  pl.BlockSpec((1, A_WIDTH), lambda i: (0, 0))],
        out_specs=pl.BlockSpec((tm, A_WIDTH), lambda i: (i, 0)),
        out_shape=jax.ShapeDtypeStruct((m, A_WIDTH), BF16),
        scratch_shapes=[pltpu.VMEM((CHUNK, A_WIDTH), F32)],
        compiler_params=_params(("parallel",)),
        name="gmlp",
    )(proj, proj, proj, g_v.reshape(1, A_WIDTH), w_s, bias,
      g_out_a.reshape(1, A_WIDTH))


SB_T = 128
SB_SUB = 128
SB_NSUB = 3
SB_W = SB_NSUB * SB_SUB
SB_HG = 6
SB_ZERO_WEIGHT_LOG = -105.0


def _sb_kernel(q_ref, k_ref, v_ref, o_ref, acc_ref, carry_ref):
    t, w, sub = SB_T, SB_W, SB_SUB
    i = pl.program_id(2)
    scale = HEAD_DIM ** -0.5
    q_pos = i * t + lax.broadcasted_iota(jnp.int32, (t, w), 0)
    k_off = lax.broadcasted_iota(jnp.int32, (t, w), 1)
    row = lax.broadcasted_iota(jnp.int32, (sub, sub), 0)
    col = lax.broadcasted_iota(jnp.int32, (sub, sub), 1)
    suffix = (row > col).astype(BF16)

    acc_ref[...] = jnp.zeros_like(acc_ref)
    carry_ref[...] = jnp.zeros_like(carry_ref)

    def window(start, bound):
        valid = (start + k_off) < jnp.minimum(q_pos, bound)
        ks = pl.ds(pl.multiple_of(start, sub), w)
        log_beta, log_keep, pieces = [], [], []
        for h in range(SB_HG):
            hc = slice(h * HEAD_DIM, (h + 1) * HEAD_DIM)
            z = lax.dot_general(q_ref[0, :, hc], k_ref[0, ks, hc],
                                (((1,), (1,)), ((), ())),
                                preferred_element_type=F32) * scale
            sp = jnp.maximum(z, 0.0) + jnp.log(1.0 + jnp.exp(-jnp.abs(z)))
            lk = jnp.where(valid, -sp, 0.0)
            log_beta.append(z - sp)
            log_keep.append(lk)
            hi = lk.astype(BF16)
            lo = (lk - hi.astype(F32)).astype(BF16)
            for c in range(SB_NSUB):
                pieces += [hi[:, c * sub:(c + 1) * sub], lo[:, c * sub:(c + 1) * sub]]
        local = jnp.dot(jnp.concatenate(pieces, axis=0), suffix,
                        preferred_element_type=F32)
        worst = None
        for h in range(SB_HG):
            hc = slice(h * HEAD_DIM, (h + 1) * HEAD_DIM)
            lk = log_keep[h]
            running = carry_ref[h]
            after = [None] * SB_NSUB
            for c in reversed(range(SB_NSUB)):
                base = (h * SB_NSUB + c) * 2 * t
                after[c] = (local[base:base + t] + local[base + t:base + 2 * t]
                            + running)
                running = running + jnp.sum(lk[:, c * sub:(c + 1) * sub],
                                            axis=-1, keepdims=True)
            a = jnp.where(valid,
                          jnp.exp(log_beta[h] + jnp.concatenate(after, axis=1)),
                          0.0)
            acc_ref[h] += jnp.dot(a.astype(BF16), v_ref[0, ks, hc],
                                  preferred_element_type=F32)
            carry_ref[h] = running
            top = jnp.max(running)
            worst = top if worst is None else jnp.maximum(worst, top)
        return worst

    first = jnp.maximum(i + 1 - SB_NSUB, 0) * sub
    worst = window(first, jnp.int32(2 ** 30))

    def more(state):
        start, worst = state
        return jnp.logical_and(start > 0, worst > SB_ZERO_WEIGHT_LOG)

    def step(state):
        start, _ = state
        nxt = jnp.maximum(start - w, 0)
        return nxt, window(nxt, start)

    lax.while_loop(more, step, (first, worst))
    for h in range(SB_HG):
        o_ref[0, :, h * HEAD_DIM:(h + 1) * HEAD_DIM] = acc_ref[h].astype(o_ref.dtype)


def _stick_breaking(proj3):
    b, s, _ = proj3.shape
    t, gw = SB_T, SB_HG * HEAD_DIM
    blk = lambda off: off // gw
    return pl.pallas_call(
        _sb_kernel,
        grid=(b, HEADS // SB_HG, s // t),
        in_specs=[
            pl.BlockSpec((1, t, gw), lambda bi, g, i: (bi, i, blk(OFF_QB) + g)),
            pl.BlockSpec((1, s, gw), lambda bi, g, i: (bi, 0, blk(OFF_KB) + g)),
            pl.BlockSpec((1, s, gw), lambda bi, g, i: (bi, 0, blk(OFF_VB) + g)),
        ],
        out_specs=pl.BlockSpec((1, t, gw), lambda bi, g, i: (bi, i, g)),
        out_shape=jax.ShapeDtypeStruct((b, s, B_WIDTH), BF16),
        scratch_shapes=[pltpu.VMEM((SB_HG, t, HEAD_DIM), F32),
                        pltpu.VMEM((SB_HG, t, 1), F32)],
        compiler_params=_params(("parallel", "parallel", "arbitrary")),
        name="stick_breaking",
    )(proj3, proj3, proj3)


def _rope256(x, c_ref, sa_ref, sb_ref):
    return (x * c_ref[...]
            + pltpu.roll(x, ROPE_HALF, axis=1) * sa_ref[...]
            + pltpu.roll(x, C_QK - ROPE_HALF, axis=1) * sb_ref[...])


def _mla_q_kernel(cq_ref, g_ref, w_ref, c_ref, sa_ref, sb_ref, o_ref):
    scale = (C_NOPE + C_ROPE) ** -0.5 * math.log2(math.e)
    x = cq_ref[...].astype(F32)
    xn = (x * _rms_scale(x, Q_LORA) * g_ref[...]).astype(BF16)
    q = jnp.dot(xn, w_ref[...], preferred_element_type=F32)
    for h in range(HEADS):
        cols = slice(h * C_QK, (h + 1) * C_QK)
        o_ref[:, cols] = (_rope256(q[:, cols], c_ref, sa_ref, sb_ref)
                          * scale).astype(o_ref.dtype)


VT_ONES = 16
VT_ROWS = HEAD_DIM + VT_ONES


def _mla_kv_kernel(ckv_ref, kr_ref, g_ref, wk_ref, wvt_ref, c_ref, sa_ref,
                   sb_ref, k_ref, vt_ref):
    tm = ckv_ref.shape[0]
    x = ckv_ref[...].astype(F32)
    xn = (x * _rms_scale(x, KV_LORA) * g_ref[...]).astype(BF16)
    kn = jnp.dot(xn, wk_ref[...], preferred_element_type=F32)
    kr = _rope256(kr_ref[...].astype(F32), c_ref, sa_ref, sb_ref)
    for h in range(HEADS):
        cols = slice(h * C_QK, (h + 1) * C_QK)
        k_ref[:, cols] = (kn[:, cols] + kr).astype(k_ref.dtype)
    vt = lax.dot_general(wvt_ref[...], xn, _NT, preferred_element_type=F32)
    for h in range(HEADS):
        vt_ref[h * VT_ROWS:h * VT_ROWS + HEAD_DIM, :] = (
            vt[h * HEAD_DIM:(h + 1) * HEAD_DIM, :].astype(vt_ref.dtype))
        vt_ref[h * VT_ROWS + HEAD_DIM:(h + 1) * VT_ROWS, :] = jnp.ones(
            (VT_ONES, tm), vt_ref.dtype)


def _mla_project(proj, g_q, g_kv, w_uq_p, w_k_p, w_vt_p, tabs, b, s, tm=512):
    m = proj.shape[0]
    tiles = s // tm
    tab_spec = pl.BlockSpec((tm, C_QK), lambda i: (i, 0))
    q_cat = pl.pallas_call(
        _mla_q_kernel,
        grid=(m // tm,),
        in_specs=[pl.BlockSpec((tm, Q_LORA), lambda i: (i, TAIL_CQ // Q_LORA)),
                  pl.BlockSpec((1, Q_LORA), lambda i: (0, 0)),
                  pl.BlockSpec((Q_LORA, HEADS * C_QK), lambda i: (0, 0)),
                  tab_spec, tab_spec, tab_spec],
        out_specs=pl.BlockSpec((tm, HEADS * C_QK), lambda i: (i, 0)),
        out_shape=jax.ShapeDtypeStruct((m, HEADS * C_QK), BF16),
        compiler_params=_params(("parallel",)),
        name="mla_q",
    )(proj, g_q.reshape(1, Q_LORA), w_uq_p, *tabs)
    k_cat, vt = pl.pallas_call(
        _mla_kv_kernel,
        grid=(m // tm,),
        in_specs=[pl.BlockSpec((tm, KV_LORA), lambda i: (i, TAIL_CKV // KV_LORA)),
                  pl.BlockSpec((tm, C_QK), lambda i: (i, TAIL_KR // C_QK)),
                  pl.BlockSpec((1, KV_LORA), lambda i: (0, 0)),
                  pl.BlockSpec((KV_LORA, HEADS * C_QK), lambda i: (0, 0)),
                  pl.BlockSpec((C_WIDTH, KV_LORA), lambda i: (0, 0)),
                  tab_spec, tab_spec, tab_spec],
        out_specs=[pl.BlockSpec((tm, HEADS * C_QK), lambda i: (i, 0)),
                   pl.BlockSpec((None, HEADS * VT_ROWS, tm),
                                lambda i: (i // tiles, 0, i % tiles))],
        out_shape=[jax.ShapeDtypeStruct((m, HEADS * C_QK), BF16),
                   jax.ShapeDtypeStruct((b, HEADS * VT_ROWS, s), BF16)],
        compiler_params=_params(("parallel",)),
        name="mla_kv",
    )(proj, proj, g_kv.reshape(1, KV_LORA), w_k_p, w_vt_p, *tabs)
    return q_cat, k_cat, vt


MLA_T = 256
MLA_W = 512
MLA_HG = 4


def _mla_attn_kernel(q_ref, k_ref, vt_ref, o_ref, m_ref, acc_ref, s_ref,
                     smax_ref):
    t, w = MLA_T, MLA_W
    i = pl.program_id(2)
    k_off = lax.broadcasted_iota(jnp.int32, (w, t), 0)
    q_pos = i * t + lax.broadcasted_iota(jnp.int32, (w, t), 1)

    m_ref[...] = jnp.full_like(m_ref, -jnp.inf)
    acc_ref[...] = jnp.zeros_like(acc_ref)

    def scores(j, slot):
        ks = pl.ds(pl.multiple_of(j * w, w), w)
        for h in range(MLA_HG):
            s = lax.dot_general(k_ref[0, ks, h * C_QK:(h + 1) * C_QK],
                                q_ref[0, :, h * C_QK:(h + 1) * C_QK], _NT,
                                preferred_element_type=F32)
            s_ref[slot, h] = s
            smax_ref[slot, h] = jnp.max(s, axis=0, keepdims=True)

    def softmax_pv(j, slot, masked):
        start = pl.multiple_of(j * w, w)
        ks = pl.ds(start, w)
        for h in range(MLA_HG):
            s = s_ref[slot, h]
            if masked:
                s = jnp.where((start + k_off) <= q_pos, s, -jnp.inf)
                s_max = jnp.max(s, axis=0, keepdims=True)
            else:
                s_max = smax_ref[slot, h]
            m_old = m_ref[h]
            m_new = jnp.maximum(m_old, s_max)
            alpha = jnp.exp2(m_old - m_new)
            p = jnp.exp2(s - m_new)
            acc_ref[h] = alpha * acc_ref[h] + jnp.dot(
                vt_ref[0, h * VT_ROWS:(h + 1) * VT_ROWS, ks], p.astype(BF16),
                preferred_element_type=F32)
            m_ref[h] = m_new

    last = (i * t) // w

    scores(0, 0)

    def pair(jj, carry):
        j = 2 * jj
        scores(j + 1, 1)
        softmax_pv(j, 0, False)
        scores(j + 2, 0)
        softmax_pv(j + 1, 1, False)
        return carry

    lax.fori_loop(0, last // 2, pair, 0)

    @pl.when(last % 2 == 0)
    def _():
        softmax_pv(last, 0, True)

    @pl.when(last % 2 == 1)
    def _():
        scores(last, 1)
        softmax_pv(last - 1, 0, False)
        softmax_pv(last, 1, True)
    for h in range(MLA_HG):
        acc = acc_ref[h]
        out_t = acc[:HEAD_DIM] / acc[HEAD_DIM:HEAD_DIM + 1]
        o_ref[0, :, h * HEAD_DIM:(h + 1) * HEAD_DIM] = out_t.T.astype(o_ref.dtype)


def _mla_attention(q_cat, k_cat, vt, b, s):
    t, hg = MLA_T, MLA_HG
    q3 = q_cat.reshape(b, s, HEADS * C_QK)
    k3 = k_cat.reshape(b, s, HEADS * C_QK)
    return pl.pallas_call(
        _mla_attn_kernel,
        grid=(b, HEADS // hg, s // t),
        in_specs=[pl.BlockSpec((1, t, hg * C_QK), lambda bi, g, i: (bi, i, g)),
                  pl.BlockSpec((1, s, hg * C_QK), lambda bi, g, i: (bi, 0, g)),
                  pl.BlockSpec((1, hg * VT_ROWS, s), lambda bi, g, i: (bi, g, 0))],
        out_specs=pl.BlockSpec((1, t, hg * HEAD_DIM), lambda bi, g, i: (bi, i, g)),
        out_shape=jax.ShapeDtypeStruct((b, s, C_WIDTH), BF16),
        scratch_shapes=[pltpu.VMEM((hg, 1, t), F32),
                        pltpu.VMEM((hg, VT_ROWS, t), F32),
                        pltpu.VMEM((2, hg, MLA_W, t), F32),
                        pltpu.VMEM((2, hg, 1, t), F32)],
        compiler_params=_params(("parallel", "parallel", "arbitrary")),
        name="mla_attention",
    )(q3, k3, vt)


def _gate_kernel(ya_ref, yb_ref, zb_ref, yc_ref, zc_ref, g_ref, o_ref):
    o_ref[:, :A_WIDTH] = ya_ref[...]
    lo = A_WIDTH
    for y_ref, z_ref, width in ((yb_ref, zb_ref, B_WIDTH), (yc_ref, zc_ref, C_WIDTH)):
        y = y_ref[...].astype(F32)
        z = z_ref[...].astype(F32)
        g = g_ref[:, lo:lo + width]
        o_ref[:, lo:lo + width] = (y * _rms_scale(y, width) * g
                                   * _silu(z)).astype(o_ref.dtype)
        lo += width


def _gate(ya, yb, yc, main, tail, g_out, tm=512):
    m = ya.shape[0]
    return pl.pallas_call(
        _gate_kernel,
        grid=(m // tm,),
        in_specs=[pl.BlockSpec((tm, A_WIDTH), lambda i: (i, 0)),
                  pl.BlockSpec((tm, B_WIDTH), lambda i: (i, 0)),
                  pl.BlockSpec((tm, B_WIDTH), lambda i: (i, OFF_ZB // B_WIDTH)),
                  pl.BlockSpec((tm, C_WIDTH), lambda i: (i, 0)),
                  pl.BlockSpec((tm, C_WIDTH), lambda i: (i, TAIL_ZC // C_WIDTH)),
                  pl.BlockSpec((1, D_MODEL), lambda i: (0, 0))],
        out_specs=pl.BlockSpec((tm, D_MODEL), lambda i: (i, 0)),
        out_shape=jax.ShapeDtypeStruct((m, D_MODEL), BF16),
        compiler_params=_params(("parallel",)),
        name="gate",
    )(ya, yb, main, yc, tail, g_out.reshape(1, D_MODEL))


def _prep_w_uq(w):
    w = w.reshape(Q_LORA, HEADS, C_NOPE + C_ROPE)
    w = jnp.pad(w, ((0, 0), (0, 0), (0, C_QK - C_NOPE - C_ROPE)))
    return w.reshape(Q_LORA, HEADS * C_QK).astype(BF16)


def _prep_w_ukv(w):
    w = w.reshape(KV_LORA, HEADS, C_NOPE + HEAD_DIM)
    wk = jnp.pad(w[:, :, :C_NOPE], ((0, 0), (0, 0), (0, C_QK - C_NOPE)))
    wv_t = w[:, :, C_NOPE:].reshape(KV_LORA, C_WIDTH).T
    return wk.reshape(KV_LORA, HEADS * C_QK).astype(BF16), wv_t.astype(BF16)


def _rope_tables(positions):
    inv_freq = 1.0 / (ROPE_THETA ** (jnp.arange(0, C_ROPE, 2, dtype=F32) / C_ROPE))
    ang = positions.astype(F32).reshape(-1, 1) * inv_freq
    cos, sin = jnp.cos(ang), jnp.sin(ang)
    m = ang.shape[0]
    pad = C_QK - C_NOPE - C_ROPE
    c = jnp.concatenate([jnp.ones((m, C_NOPE), F32), cos, cos,
                         jnp.zeros((m, pad), F32)], axis=1)
    sa = jnp.concatenate([jnp.zeros((m, C_NOPE + ROPE_HALF), F32), sin,
                          jnp.zeros((m, pad), F32)], axis=1)
    sb = jnp.concatenate([jnp.zeros((m, C_NOPE), F32), -sin,
                          jnp.zeros((m, ROPE_HALF + pad), F32)], axis=1)
    return c, sa, sb


def kernel(x, positions, g_pre, w_in, a_g_v, a_w_s, a_b_s, c_g_q, c_g_kv,
           c_w_uq, c_w_ukv, g_out, w_out, g_final):
    b, s, d = x.shape
    m = b * s
    depth = w_in.shape[0]
    tabs = _rope_tables(positions)
    w_in_t = jnp.swapaxes(w_in, 1, 2)
    h = x.reshape(m, d)
    for l in range(depth):
        hn = _rmsnorm(h, g_pre[l], BF16)
        main = _in_proj_main(hn, w_in_t, l, tm=1024, tn=512)
        tail = _in_proj_tail(hn, w_in_t, l, tm=1024, tk=512)
        ya = _gmlp(main, a_g_v[l], a_w_s[l], a_b_s[l], g_out[l, :A_WIDTH])
        yb = _stick_breaking(main.reshape(b, s, D_MAIN)).reshape(m, B_WIDTH)
        w_k_p, w_vt_p = _prep_w_ukv(c_w_ukv[l])
        q_cat, k_cat, vt = _mla_project(tail, c_g_q[l], c_g_kv[l],
                                        _prep_w_uq(c_w_uq[l]), w_k_p, w_vt_p,
                                        tabs, b, s)
        yc = _mla_attention(q_cat, k_cat, vt, b, s).reshape(m, C_WIDTH)
        y = _gate(ya, yb, yc, main, tail, g_out[l])
        h = _matmul_residual(y, w_out, l, h, tm=1024, tn=512)
    return _rmsnorm(h, g_final, F32).reshape(b, s, d)
```

```python
import functools
import math

import jax
import jax.numpy as jnp
from jax import lax
from jax.experimental import pallas as pl
from jax.experimental.pallas import tpu as pltpu

F32 = jnp.float32
BF16 = jnp.bfloat16

D_MODEL = 4096
A_GROUPS = 8
A_CH = 128
A_WIDTH = A_GROUPS * A_CH
CHUNK = 128
HEADS = 12
HEAD_DIM = 128
B_WIDTH = HEADS * HEAD_DIM
C_NOPE = 128
C_ROPE = 64
C_WIDTH = HEADS * HEAD_DIM
Q_LORA = 768
KV_LORA = 512
ROPE_THETA = 10000.0
EPS = 1e-6

C_QK = 256
ROPE_HALF = C_ROPE // 2
ROPE_TILE = 128
ROPE_X2 = ROPE_TILE // 2

OFF_U = 0
OFF_V = OFF_U + A_WIDTH
OFF_ZA = OFF_V + A_WIDTH
OFF_QB = OFF_ZA + A_WIDTH
OFF_KB = OFF_QB + B_WIDTH
OFF_VB = OFF_KB + B_WIDTH
OFF_ZB = OFF_VB + B_WIDTH
D_MAIN = OFF_ZB + B_WIDTH
REF_CQ = D_MAIN
REF_CKV = REF_CQ + Q_LORA
REF_KR = REF_CKV + KV_LORA
REF_ZC = REF_KR + C_ROPE
TAIL_ZC = 0
TAIL_CKV = TAIL_ZC + C_WIDTH
TAIL_KR = TAIL_CKV + KV_LORA
TAIL_CQ = TAIL_KR + C_QK
D_TAIL = TAIL_CQ + Q_LORA

VMEM_LIMIT_BYTES = 56 * 1024 * 1024


def _params(sem):
    return pltpu.CompilerParams(dimension_semantics=sem,
                                vmem_limit_bytes=VMEM_LIMIT_BYTES)


def _gelu(x):
    c = math.sqrt(2.0 / math.pi)
    return x * (0.5 * (1.0 + jnp.tanh(c * (x + 0.044715 * (x * x * x)))))


def _silu(z):
    return z / (1.0 + jnp.exp(-z))


def _rms_scale(x, width):
    return lax.rsqrt(jnp.sum(x * x, axis=-1, keepdims=True) * (1.0 / width) + EPS)


def _rmsnorm_kernel(x_ref, g_ref, o_ref):
    x = x_ref[...]
    o_ref[...] = (x * _rms_scale(x, x.shape[-1]) * g_ref[...]).astype(o_ref.dtype)


def _rmsnorm(x, g, out_dtype, tm=512):
    m, d = x.shape
    return pl.pallas_call(
        _rmsnorm_kernel,
        grid=(m // tm,),
        in_specs=[pl.BlockSpec((tm, d), lambda i: (i, 0)),
                  pl.BlockSpec((1, d), lambda i: (0, 0))],
        out_specs=pl.BlockSpec((tm, d), lambda i: (i, 0)),
        out_shape=jax.ShapeDtypeStruct((m, d), out_dtype),
        compiler_params=_params(("parallel",)),
        name="rmsnorm",
    )(x, g.reshape(1, d))


_NT = (((1,), (1,)), ((), ()))


def _in_proj_main_kernel(x_ref, wt_ref, o_ref):
    o_ref[...] = lax.dot_general(x_ref[...], wt_ref[...].astype(BF16), _NT,
                                 preferred_element_type=F32).astype(o_ref.dtype)


def _in_proj_main(x, wt, layer, tm, tn):
    m, k = x.shape
    return pl.pallas_call(
        _in_proj_main_kernel,
        grid=(m // tm, D_MAIN // tn),
        in_specs=[pl.BlockSpec((tm, k), lambda i, j: (i, 0)),
                  pl.BlockSpec((None, tn, k), lambda i, j: (layer, j, 0))],
        out_specs=pl.BlockSpec((tm, tn), lambda i, j: (i, j)),
        out_shape=jax.ShapeDtypeStruct((m, D_MAIN), BF16),
        compiler_params=_params(("parallel", "parallel")),
        name="in_proj_main",
    )(x, wt)


TAIL_PART_ROWS = 576
TAIL_PARTS = 5
TAIL_COPIES = (
    (TAIL_ZC, REF_ZC - D_MAIN, C_WIDTH),
    (TAIL_CKV, REF_CKV - D_MAIN, KV_LORA),
    (TAIL_KR, REF_KR - D_MAIN, ROPE_HALF),
    (TAIL_KR + ROPE_X2, REF_KR - D_MAIN + ROPE_HALF, ROPE_HALF),
    (TAIL_CQ, REF_CQ - D_MAIN, Q_LORA),
)
TAIL_ZERO_ROWS = ((TAIL_KR + ROPE_HALF, ROPE_X2 - ROPE_HALF),
                  (TAIL_KR + ROPE_X2 + ROPE_HALF, ROPE_X2 - ROPE_HALF),
                  (TAIL_KR + ROPE_TILE, TAIL_CQ - TAIL_KR - ROPE_TILE))


def _in_proj_tail_kernel(x_ref, *refs):
    w_refs = refs[:TAIL_PARTS]
    o_ref, wp_ref, acc_ref = refs[TAIL_PARTS:]
    kk = pl.program_id(1)

    @pl.when(kk == 0)
    def _():
        acc_ref[...] = jnp.zeros_like(acc_ref)

    for dst, src, rows in TAIL_COPIES:
        done = 0
        while done < rows:
            part, lo = divmod(src + done, TAIL_PART_ROWS)
            n = min(rows - done, TAIL_PART_ROWS - lo)
            wp_ref[dst + done:dst + done + n, :] = (
                w_refs[part][lo:lo + n, :].astype(BF16))
            done += n
    for dst, rows in TAIL_ZERO_ROWS:
        wp_ref[dst:dst + rows, :] = jnp.zeros((rows, wp_ref.shape[1]), BF16)

    acc_ref[...] += lax.dot_general(x_ref[...], wp_ref[...], _NT,
                                    preferred_element_type=F32)

    @pl.when(kk == pl.num_programs(1) - 1)
    def _():
        o_ref[...] = acc_ref[...].astype(o_ref.dtype)


def _in_proj_tail(x, wt, layer, tm, tk):
    m, k = x.shape
    first = D_MAIN // TAIL_PART_ROWS
    w_specs = [pl.BlockSpec((None, TAIL_PART_ROWS, tk),
                            lambda i, kk, p=p: (layer, first + p, kk))
               for p in range(TAIL_PARTS)]
    return pl.pallas_call(
        _in_proj_tail_kernel,
        grid=(m // tm, k // tk),
        in_specs=[pl.BlockSpec((tm, tk), lambda i, kk: (i, kk))] + w_specs,
        out_specs=pl.BlockSpec((tm, D_TAIL), lambda i, kk: (i, 0)),
        out_shape=jax.ShapeDtypeStruct((m, D_TAIL), BF16),
        scratch_shapes=[pltpu.VMEM((D_TAIL, tk), BF16),
                        pltpu.VMEM((tm, D_TAIL), F32)],
        compiler_params=_params(("parallel", "arbitrary")),
        name="in_proj_tail",
    )(x, *([wt] * TAIL_PARTS))


def _matmul_residual_kernel(x_ref, w_ref, r_ref, o_ref):
    o_ref[...] = r_ref[...] + jnp.dot(x_ref[...], w_ref[...].astype(BF16),
                                      preferred_element_type=F32)


def _matmul_residual(x, w, layer, res, tm, tn):
    m, k = x.shape
    n = w.shape[2]
    return pl.pallas_call(
        _matmul_residual_kernel,
        grid=(m // tm, n // tn),
        in_specs=[pl.BlockSpec((tm, k), lambda i, j: (i, 0)),
                  pl.BlockSpec((None, k, tn), lambda i, j: (layer, 0, j)),
                  pl.BlockSpec((tm, tn), lambda i, j: (i, j))],
        out_specs=pl.BlockSpec((tm, tn), lambda i, j: (i, j)),
        out_shape=jax.ShapeDtypeStruct((m, n), F32),
        compiler_params=_params(("parallel", "parallel")),
        name="out_proj",
    )(x, w, res)


def _gmlp_kernel(u_ref, v_ref, z_ref, gv_ref, w_ref, bias_ref, go_ref, o_ref,
                 y_ref):
    tm = u_ref.shape[0]
    t_idx = lax.broadcasted_iota(jnp.int32, (CHUNK, CHUNK), 0)
    s_idx = lax.broadcasted_iota(jnp.int32, (CHUNK, CHUNK), 1)
    causal = s_idx <= t_idx

    def chunk(c, carry):
        rows = pl.ds(pl.multiple_of(c * CHUNK, CHUNK), CHUNK)
        ssq = jnp.zeros((CHUNK, 1), F32)
        for g in range(A_GROUPS):
            cols = slice(g * A_CH, (g + 1) * A_CH)
            vg = _gelu(v_ref[rows, cols].astype(F32))
            vn = vg * _rms_scale(vg, A_CH) * gv_ref[:, cols]
            w = jnp.where(causal, w_ref[g], 0.0).astype(BF16)
            sv = jnp.dot(w, vn.astype(BF16), preferred_element_type=F32)
            sv = sv + bias_ref[:, cols]
            y = _gelu(u_ref[rows, cols].astype(F32)) * sv
            ssq = ssq + jnp.sum(y * y, axis=-1, keepdims=True)
            y_ref[:, cols] = y
        r = lax.rsqrt(ssq * (1.0 / A_WIDTH) + EPS)
        z = z_ref[rows, :].astype(F32)
        o_ref[rows, :] = (y_ref[...] * r * go_ref[...] * _silu(z)).astype(o_ref.dtype)
        return carry

    lax.fori_loop(0, tm // CHUNK, chunk, 0)


def _gmlp(proj, g_v, w_s, b_s, g_out_a, tm=512):
    m = proj.shape[0]
    bias = jnp.repeat(b_s.T, A_CH, axis=1)
    col = lambda off: (lambda i: (i, off // A_WIDTH))
    return pl.pallas_call(
        _gmlp_kernel,
        grid=(m // tm,),
        in_specs=[pl.BlockSpec((tm, A_WIDTH), col(OFF_U)),
                  pl.BlockSpec((tm, A_WIDTH), col(OFF_V)),
                  pl.BlockSpec((tm, A_WIDTH), col(OFF_ZA)),
                  pl.BlockSpec((1, A_WIDTH), lambda i: (0, 0)),
                  pl.BlockSpec((A_GROUPS, CHUNK, CHUNK), lambda i: (0, 0, 0)),
                  pl.BlockSpec((CHUNK, A_WIDTH), lambda i: (0, 0)),
                  pl.BlockSpec((1, A_WIDTH), lambda i: (0, 0))],
        out_specs=pl.BlockSpec((tm, A_WIDTH), lambda i: (i, 0)),
        out_shape=jax.ShapeDtypeStruct((m, A_WIDTH), BF16),
        scratch_shapes=[pltpu.VMEM((CHUNK, A_WIDTH), F32)],
        compiler_params=_params(("parallel",)),
        name="gmlp",
    )(proj, proj, proj, g_v.reshape(1, A_WIDTH), w_s, bias,
      g_out_a.reshape(1, A_WIDTH))


SB_T = 128
SB_SUB = 128
SB_NSUB = 3
SB_W = SB_NSUB * SB_SUB
SB_HG = 6
SB_ZERO_WEIGHT_LOG = -105.0


def _sb_kernel(q_ref, k_ref, v_ref, o_ref, acc_ref, carry_ref):
    t, w, sub = SB_T, SB_W, SB_SUB
    i = pl.program_id(2)
    scale = HEAD_DIM ** -0.5
    q_pos = i * t + lax.broadcasted_iota(jnp.int32, (t, w), 0)
    k_off = lax.broadcasted_iota(jnp.int32, (t, w), 1)
    row = lax.broadcasted_iota(jnp.int32, (sub, sub), 0)
    col = lax.broadcasted_iota(jnp.int32, (sub, sub), 1)
    suffix = (row > col).astype(BF16)

    acc_ref[...] = jnp.zeros_like(acc_ref)
    carry_ref[...] = jnp.zeros_like(carry_ref)

    def window(start, bound):
        valid = (start + k_off) < jnp.minimum(q_pos, bound)
        ks = pl.ds(pl.multiple_of(start, sub), w)
        log_beta, log_keep, pieces = [], [], []
        for h in range(SB_HG):
            hc = slice(h * HEAD_DIM, (h + 1) * HEAD_DIM)
            z = lax.dot_general(q_ref[0, :, hc], k_ref[0, ks, hc],
                                (((1,), (1,)), ((), ())),
                                preferred_element_type=F32) * scale
            sp = jnp.maximum(z, 0.0) + jnp.log(1.0 + jnp.exp(-jnp.abs(z)))
            lk = jnp.where(valid, -sp, 0.0)
            log_beta.append(z - sp)
            log_keep.append(lk)
            hi = lk.astype(BF16)
            lo = (lk - hi.astype(F32)).astype(BF16)
            for c in range(SB_NSUB):
                pieces += [hi[:, c * sub:(c + 1) * sub], lo[:, c * sub:(c + 1) * sub]]
        local = jnp.dot(jnp.concatenate(pieces, axis=0), suffix,
                        preferred_element_type=F32)
        worst = None
        for h in range(SB_HG):
            hc = slice(h * HEAD_DIM, (h + 1) * HEAD_DIM)
            lk = log_keep[h]
            running = carry_ref[h]
            after = [None] * SB_NSUB
            for c in reversed(range(SB_NSUB)):
                base = (h * SB_NSUB + c) * 2 * t
                after[c] = (local[base:base + t] + local[base + t:base + 2 * t]
                            + running)
                running = running + jnp.sum(lk[:, c * sub:(c + 1) * sub],
                                            axis=-1, keepdims=True)
            a = jnp.where(valid,
                          jnp.exp(log_beta[h] + jnp.concatenate(after, axis=1)),
                          0.0)
            acc_ref[h] += jnp.dot(a.astype(BF16), v_ref[0, ks, hc],
                                  preferred_element_type=F32)
            carry_ref[h] = running
            top = jnp.max(running)
            worst = top if worst is None else jnp.maximum(worst, top)
        return worst

    first = jnp.maximum(i + 1 - SB_NSUB, 0) * sub
    worst = window(first, jnp.int32(2 ** 30))

    def more(state):
        start, worst = state
        return jnp.logical_and(start > 0, worst > SB_ZERO_WEIGHT_LOG)

    def step(state):
        start, _ = state
        nxt = jnp.maximum(start - w, 0)
        return nxt, window(nxt, start)

    lax.while_loop(more, step, (first, worst))
    for h in range(SB_HG):
        o_ref[0, :, h * HEAD_DIM:(h + 1) * HEAD_DIM] = acc_ref[h].astype(o_ref.dtype)


def _stick_breaking(proj3):
    b, s, _ = proj3.shape
    t, gw = SB_T, SB_HG * HEAD_DIM
    blk = lambda off: off // gw
    return pl.pallas_call(
        _sb_kernel,
        grid=(b, HEADS // SB_HG, s // t),
        in_specs=[
            pl.BlockSpec((1, t, gw), lambda bi, g, i: (bi, i, blk(OFF_QB) + g)),
            pl.BlockSpec((1, s, gw), lambda bi, g, i: (bi, 0, blk(OFF_KB) + g)),
            pl.BlockSpec((1, s, gw), lambda bi, g, i: (bi, 0, blk(OFF_VB) + g)),
        ],
        out_specs=pl.BlockSpec((1, t, gw), lambda bi, g, i: (bi, i, g)),
        out_shape=jax.ShapeDtypeStruct((b, s, B_WIDTH), BF16),
        scratch_shapes=[pltpu.VMEM((SB_HG, t, HEAD_DIM), F32),
                        pltpu.VMEM((SB_HG, t, 1), F32)],
        compiler_params=_params(("parallel", "parallel", "arbitrary")),
        name="stick_breaking",
    )(proj3, proj3, proj3)


def _rope_tile(x, cos_ref, sin_ref):
    return x * cos_ref[...] + pltpu.roll(x, ROPE_X2, axis=1) * sin_ref[...]


def _mla_q_kernel(cq_ref, g_ref, w_ref, cos_ref, sin_ref, o_ref):
    scale = (C_NOPE + C_ROPE) ** -0.5 * math.log2(math.e)
    x = cq_ref[...].astype(F32)
    xn = (x * _rms_scale(x, Q_LORA) * g_ref[...]).astype(BF16)
    q = jnp.dot(xn, w_ref[...], preferred_element_type=F32)
    for h in range(HEADS):
        lo = h * C_QK
        o_ref[:, lo:lo + C_NOPE] = (q[:, lo:lo + C_NOPE] * scale).astype(o_ref.dtype)
        rope = _rope_tile(q[:, lo + C_NOPE:lo + C_QK], cos_ref, sin_ref)
        o_ref[:, lo + C_NOPE:lo + C_QK] = (rope * scale).astype(o_ref.dtype)


VT_ONES = 16
VT_ROWS = HEAD_DIM + VT_ONES


def _mla_kv_kernel(ckv_ref, kr_ref, g_ref, wk_ref, wvt_ref, cos_ref, sin_ref,
                   k_ref, vt_ref):
    tm = ckv_ref.shape[0]
    x = ckv_ref[...].astype(F32)
    xn = (x * _rms_scale(x, KV_LORA) * g_ref[...]).astype(BF16)
    kn = jnp.dot(xn, wk_ref[...], preferred_element_type=F32).astype(k_ref.dtype)
    kr = _rope_tile(kr_ref[...].astype(F32), cos_ref, sin_ref).astype(k_ref.dtype)
    for h in range(HEADS):
        k_ref[:, h * C_QK:h * C_QK + C_NOPE] = kn[:, h * C_NOPE:(h + 1) * C_NOPE]
        k_ref[:, h * C_QK + C_NOPE:(h + 1) * C_QK] = kr
    vt = lax.dot_general(wvt_ref[...], xn, _NT, preferred_element_type=F32)
    for h in range(HEADS):
        vt_ref[h * VT_ROWS:h * VT_ROWS + HEAD_DIM, :] = (
            vt[h * HEAD_DIM:(h + 1) * HEAD_DIM, :].astype(vt_ref.dtype))
        vt_ref[h * VT_ROWS + HEAD_DIM:(h + 1) * VT_ROWS, :] = jnp.ones(
            (VT_ONES, tm), vt_ref.dtype)


def _mla_project(proj, g_q, g_kv, w_uq_p, w_k_p, w_vt_p, tabs, b, s, tm=512):
    m = proj.shape[0]
    tiles = s // tm
    tab_spec = pl.BlockSpec((tm, ROPE_TILE), lambda i: (i, 0))
    q_cat = pl.pallas_call(
        _mla_q_kernel,
        grid=(m // tm,),
        in_specs=[pl.BlockSpec((tm, Q_LORA), lambda i: (i, TAIL_CQ // Q_LORA)),
                  pl.BlockSpec((1, Q_LORA), lambda i: (0, 0)),
                  pl.BlockSpec((Q_LORA, HEADS * C_QK), lambda i: (0, 0)),
                  tab_spec, tab_spec],
        out_specs=pl.BlockSpec((tm, HEADS * C_QK), lambda i: (i, 0)),
        out_shape=jax.ShapeDtypeStruct((m, HEADS * C_QK), BF16),
        compiler_params=_params(("parallel",)),
        name="mla_q",
    )(proj, g_q.reshape(1, Q_LORA), w_uq_p, *tabs)
    k_cat, vt = pl.pallas_call(
        _mla_kv_kernel,
        grid=(m // tm,),
        in_specs=[pl.BlockSpec((tm, KV_LORA), lambda i: (i, TAIL_CKV // KV_LORA)),
                  pl.BlockSpec((tm, ROPE_TILE), lambda i: (i, TAIL_KR // ROPE_TILE)),
                  pl.BlockSpec((1, KV_LORA), lambda i: (0, 0)),
                  pl.BlockSpec((KV_LORA, HEADS * C_NOPE), lambda i: (0, 0)),
                  pl.BlockSpec((C_WIDTH, KV_LORA), lambda i: (0, 0)),
                  tab_spec, tab_spec],
        out_specs=[pl.BlockSpec((tm, HEADS * C_QK), lambda i: (i, 0)),
                   pl.BlockSpec((None, HEADS * VT_ROWS, tm),
                                lambda i: (i // tiles, 0, i % tiles))],
        out_shape=[jax.ShapeDtypeStruct((m, HEADS * C_QK), BF16),
                   jax.ShapeDtypeStruct((b, HEADS * VT_ROWS, s), BF16)],
        compiler_params=_params(("parallel",)),
        name="mla_kv",
    )(proj, proj, g_kv.reshape(1, KV_LORA), w_k_p, w_vt_p, *tabs)
    return q_cat, k_cat, vt


MLA_T = 256
MLA_W = 512
MLA_HG = 4


def _mla_attn_kernel(q_ref, k_ref, vt_ref, o_ref, m_ref, acc_ref, s_ref,
                     smax_ref):
    t, w = MLA_T, MLA_W
    i = pl.program_id(2)
    k_off = lax.broadcasted_iota(jnp.int32, (w, t), 0)
    q_pos = i * t + lax.broadcasted_iota(jnp.int32, (w, t), 1)

    m_ref[...] = jnp.full_like(m_ref, -jnp.inf)
    acc_ref[...] = jnp.zeros_like(acc_ref)

    def scores(j, slot):
        ks = pl.ds(pl.multiple_of(j * w, w), w)
        for h in range(MLA_HG):
            s = lax.dot_general(k_ref[0, ks, h * C_QK:(h + 1) * C_QK],
                                q_ref[0, :, h * C_QK:(h + 1) * C_QK], _NT,
                                preferred_element_type=F32)
            s_ref[slot, h] = s
            smax_ref[slot, h] = jnp.max(s, axis=0, keepdims=True)

    def softmax_pv(j, slot, masked):
        start = pl.multiple_of(j * w, w)
        ks = pl.ds(start, w)
        for h in range(MLA_HG):
            s = s_ref[slot, h]
            if masked:
                s = jnp.where((start + k_off) <= q_pos, s, -jnp.inf)
                s_max = jnp.max(s, axis=0, keepdims=True)
            else:
                s_max = smax_ref[slot, h]
            m_old = m_ref[h]
            m_new = jnp.maximum(m_old, s_max)
            alpha = jnp.exp2(m_old - m_new)
            p = jnp.exp2(s - m_new)
            acc_ref[h] = alpha * acc_ref[h] + jnp.dot(
                vt_ref[0, h * VT_ROWS:(h + 1) * VT_ROWS, ks], p.astype(BF16),
                preferred_element_type=F32)
            m_ref[h] = m_new

    last = (i * t) // w

    scores(0, 0)

    def pair(jj, carry):
        j = 2 * jj
        scores(j + 1, 1)
        softmax_pv(j, 0, False)
        scores(j + 2, 0)
        softmax_pv(j + 1, 1, False)
        return carry

    lax.fori_loop(0, last // 2, pair, 0)

    @pl.when(last % 2 == 0)
    def _():
        softmax_pv(last, 0, True)

    @pl.when(last % 2 == 1)
    def _():
        scores(last, 1)
        softmax_pv(last - 1, 0, False)
        softmax_pv(last, 1, True)
    for h in range(MLA_HG):
        acc = acc_ref[h]
        out_t = acc[:HEAD_DIM] / acc[HEAD_DIM:HEAD_DIM + 1]
        o_ref[0, :, h * HEAD_DIM:(h + 1) * HEAD_DIM] = out_t.T.astype(o_ref.dtype)


def _mla_attention(q_cat, k_cat, vt, b, s):
    t, hg = MLA_T, MLA_HG
    q3 = q_cat.reshape(b, s, HEADS * C_QK)
    k3 = k_cat.reshape(b, s, HEADS * C_QK)
    return pl.pallas_call(
        _mla_attn_kernel,
        grid=(b, HEADS // hg, s // t),
        in_specs=[pl.BlockSpec((1, t, hg * C_QK), lambda bi, g, i: (bi, i, g)),
                  pl.BlockSpec((1, s, hg * C_QK), lambda bi, g, i: (bi, 0, g)),
                  pl.BlockSpec((1, hg * VT_ROWS, s), lambda bi, g, i: (bi, g, 0))],
        out_specs=pl.BlockSpec((1, t, hg * HEAD_DIM), lambda bi, g, i: (bi, i, g)),
        out_shape=jax.ShapeDtypeStruct((b, s, C_WIDTH), BF16),
        scratch_shapes=[pltpu.VMEM((hg, 1, t), F32),
                        pltpu.VMEM((hg, VT_ROWS, t), F32),
                        pltpu.VMEM((2, hg, MLA_W, t), F32),
                        pltpu.VMEM((2, hg, 1, t), F32)],
        compiler_params=_params(("parallel", "parallel", "arbitrary")),
        name="mla_attention",
    )(q3, k3, vt)


GATE_ROWS = 32


def _gate_kernel(ya_ref, yb_ref, zb_ref, yc_ref, zc_ref, g_ref, o_ref):
    o_ref[:, :A_WIDTH] = ya_ref[...]

    def chunk(c, carry):
        rows = pl.ds(pl.multiple_of(c * GATE_ROWS, GATE_ROWS), GATE_ROWS)
        lo = A_WIDTH
        for y_ref, z_ref, width in ((yb_ref, zb_ref, B_WIDTH),
                                    (yc_ref, zc_ref, C_WIDTH)):
            y = y_ref[rows, :].astype(F32)
            z = z_ref[rows, :].astype(F32)
            g = g_ref[:, lo:lo + width]
            o_ref[rows, lo:lo + width] = (y * _rms_scale(y, width) * g
                                          * _silu(z)).astype(o_ref.dtype)
            lo += width
        return carry

    lax.fori_loop(0, ya_ref.shape[0] // GATE_ROWS, chunk, 0)


def _gate(ya, yb, yc, main, tail, g_out, tm=512):
    m = ya.shape[0]
    return pl.pallas_call(
        _gate_kernel,
        grid=(m // tm,),
        in_specs=[pl.BlockSpec((tm, A_WIDTH), lambda i: (i, 0)),
                  pl.BlockSpec((tm, B_WIDTH), lambda i: (i, 0)),
                  pl.BlockSpec((tm, B_WIDTH), lambda i: (i, OFF_ZB // B_WIDTH)),
                  pl.BlockSpec((tm, C_WIDTH), lambda i: (i, 0)),
                  pl.BlockSpec((tm, C_WIDTH), lambda i: (i, TAIL_ZC // C_WIDTH)),
                  pl.BlockSpec((1, D_MODEL), lambda i: (0, 0))],
        out_specs=pl.BlockSpec((tm, D_MODEL), lambda i: (i, 0)),
        out_shape=jax.ShapeDtypeStruct((m, D_MODEL), BF16),
        compiler_params=_params(("parallel",)),
        name="gate",
    )(ya, yb, main, yc, tail, g_out.reshape(1, D_MODEL))


def _prep_w_uq(w):
    w = w.reshape(Q_LORA, HEADS, C_NOPE + C_ROPE)
    gap = jnp.zeros((Q_LORA, HEADS, ROPE_X2 - ROPE_HALF), w.dtype)
    w = jnp.concatenate([w[:, :, :C_NOPE + ROPE_HALF], gap,
                         w[:, :, C_NOPE + ROPE_HALF:], gap], axis=2)
    return w.reshape(Q_LORA, HEADS * C_QK).astype(BF16)


def _prep_w_ukv(w):
    w = w.reshape(KV_LORA, HEADS, C_NOPE + HEAD_DIM)
    wk = w[:, :, :C_NOPE].reshape(KV_LORA, HEADS * C_NOPE)
    wv_t = w[:, :, C_NOPE:].reshape(KV_LORA, C_WIDTH).T
    return wk.astype(BF16), wv_t.astype(BF16)


def _rope_tables(positions):
    inv_freq = 1.0 / (ROPE_THETA ** (jnp.arange(0, C_ROPE, 2, dtype=F32) / C_ROPE))
    ang = positions.astype(F32).reshape(-1, 1) * inv_freq
    cos, sin = jnp.cos(ang), jnp.sin(ang)
    gap = jnp.zeros((ang.shape[0], ROPE_X2 - ROPE_HALF), F32)
    return (jnp.concatenate([cos, gap, cos, gap], axis=1),
            jnp.concatenate([-sin, gap, sin, gap], axis=1))


def kernel(x, positions, g_pre, w_in, a_g_v, a_w_s, a_b_s, c_g_q, c_g_kv,
           c_w_uq, c_w_ukv, g_out, w_out, g_final):
    b, s, d = x.shape
    m = b * s
    depth = w_in.shape[0]
    tabs = _rope_tables(positions)
    w_in_t = jnp.swapaxes(w_in, 1, 2)
    h = x.reshape(m, d)
    for l in range(depth):
        hn = _rmsnorm(h, g_pre[l], BF16)
        main = _in_proj_main(hn, w_in_t, l, tm=2048, tn=256)
        tail = _in_proj_tail(hn, w_in_t, l, tm=1024, tk=512)
        ya = _gmlp(main, a_g_v[l], a_w_s[l], a_b_s[l], g_out[l, :A_WIDTH])
        yb = _stick_breaking(main.reshape(b, s, D_MAIN)).reshape(m, B_WIDTH)
        w_k_p, w_vt_p = _prep_w_ukv(c_w_ukv[l])
        q_cat, k_cat, vt = _mla_project(tail, c_g_q[l], c_g_kv[l],
                                        _prep_w_uq(c_w_uq[l]), w_k_p, w_vt_p,
                                        tabs, b, s)
        yc = _mla_attention(q_cat, k_cat, vt, b, s).reshape(m, C_WIDTH)
        y = _gate(ya, yb, yc, main, tail, g_out[l])
        h = _matmul_residual(y, w_out, l, h, tm=1024, tn=512)
    return _rmsnorm(h, g_final, F32).reshape(b, s, d)
```

```python
import functools
import math

import jax
import jax.numpy as jnp
from jax import lax
from jax.experimental import pallas as pl
from jax.experimental.pallas import tpu as pltpu

F32 = jnp.float32
BF16 = jnp.bfloat16

D_MODEL = 4096
A_GROUPS = 8
A_CH = 128
A_WIDTH = A_GROUPS * A_CH
CHUNK = 128
HEADS = 12
HEAD_DIM = 128
B_WIDTH = HEADS * HEAD_DIM
C_NOPE = 128
C_ROPE = 64
C_WIDTH = HEADS * HEAD_DIM
Q_LORA = 768
KV_LORA = 512
ROPE_THETA = 10000.0
EPS = 1e-6

C_QK = 256
ROPE_HALF = C_ROPE // 2
ROPE_TILE = 128
ROPE_X2 = ROPE_TILE // 2

OFF_U = 0
OFF_V = OFF_U + A_WIDTH
OFF_ZA = OFF_V + A_WIDTH
OFF_QB = OFF_ZA + A_WIDTH
OFF_KB = OFF_QB + B_WIDTH
OFF_VB = OFF_KB + B_WIDTH
OFF_ZB = OFF_VB + B_WIDTH
D_MAIN = OFF_ZB + B_WIDTH
REF_CQ = D_MAIN
REF_CKV = REF_CQ + Q_LORA
REF_KR = REF_CKV + KV_LORA
REF_ZC = REF_KR + C_ROPE
TAIL_ZC = 0
TAIL_CKV = TAIL_ZC + C_WIDTH
TAIL_KR = TAIL_CKV + KV_LORA
TAIL_CQ = TAIL_KR + C_QK
D_TAIL = TAIL_CQ + Q_LORA

VMEM_LIMIT_BYTES = 56 * 1024 * 1024


def _params(sem):
    return pltpu.CompilerParams(dimension_semantics=sem,
                                vmem_limit_bytes=VMEM_LIMIT_BYTES)


def _gelu(x):
    c = math.sqrt(2.0 / math.pi)
    return x * (0.5 * (1.0 + jnp.tanh(c * (x + 0.044715 * (x * x * x)))))


def _silu(z):
    return z / (1.0 + jnp.exp(-z))


def _rms_scale(x, width):
    return lax.rsqrt(jnp.sum(x * x, axis=-1, keepdims=True) * (1.0 / width) + EPS)


def _rmsnorm_kernel(x_ref, g_ref, o_ref):
    x = x_ref[...]
    o_ref[...] = (x * _rms_scale(x, x.shape[-1]) * g_ref[...]).astype(o_ref.dtype)


def _rmsnorm(x, g, out_dtype, tm=512):
    m, d = x.shape
    return pl.pallas_call(
        _rmsnorm_kernel,
        grid=(m // tm,),
        in_specs=[pl.BlockSpec((tm, d), lambda i: (i, 0)),
                  pl.BlockSpec((1, d), lambda i: (0, 0))],
        out_specs=pl.BlockSpec((tm, d), lambda i: (i, 0)),
        out_shape=jax.ShapeDtypeStruct((m, d), out_dtype),
        compiler_params=_params(("parallel",)),
        name="rmsnorm",
    )(x, g.reshape(1, d))


_NT = (((1,), (1,)), ((), ()))


def _in_proj_main_kernel(x_ref, wt_ref, o_ref):
    o_ref[...] = lax.dot_general(x_ref[...], wt_ref[...].astype(BF16), _NT,
                                 preferred_element_type=F32).astype(o_ref.dtype)


def _in_proj_main(x, wt, layer, tm, tn):
    m, k = x.shape
    return pl.pallas_call(
        _in_proj_main_kernel,
        grid=(m // tm, D_MAIN // tn),
        in_specs=[pl.BlockSpec((tm, k), lambda i, j: (i, 0)),
                  pl.BlockSpec((None, tn, k), lambda i, j: (layer, j, 0))],
        out_specs=pl.BlockSpec((tm, tn), lambda i, j: (i, j)),
        out_shape=jax.ShapeDtypeStruct((m, D_MAIN), BF16),
        compiler_params=_params(("parallel", "parallel")),
        name="in_proj_main",
    )(x, wt)


TAIL_PART_ROWS = 576
TAIL_PARTS = 5
TAIL_COPIES = (
    (TAIL_ZC, REF_ZC - D_MAIN, C_WIDTH),
    (TAIL_CKV, REF_CKV - D_MAIN, KV_LORA),
    (TAIL_KR, REF_KR - D_MAIN, ROPE_HALF),
    (TAIL_KR + ROPE_X2, REF_KR - D_MAIN + ROPE_HALF, ROPE_HALF),
    (TAIL_CQ, REF_CQ - D_MAIN, Q_LORA),
)
TAIL_ZERO_ROWS = ((TAIL_KR + ROPE_HALF, ROPE_X2 - ROPE_HALF),
                  (TAIL_KR + ROPE_X2 + ROPE_HALF, ROPE_X2 - ROPE_HALF),
                  (TAIL_KR + ROPE_TILE, TAIL_CQ - TAIL_KR - ROPE_TILE))


def _in_proj_tail_kernel(x_ref, *refs):
    w_refs = refs[:TAIL_PARTS]
    o_ref, wp_ref, acc_ref = refs[TAIL_PARTS:]
    kk = pl.program_id(1)

    @pl.when(kk == 0)
    def _():
        acc_ref[...] = jnp.zeros_like(acc_ref)

    for dst, src, rows in TAIL_COPIES:
        done = 0
        while done < rows:
            part, lo = divmod(src + done, TAIL_PART_ROWS)
            n = min(rows - done, TAIL_PART_ROWS - lo)
            wp_ref[dst + done:dst + done + n, :] = (
                w_refs[part][lo:lo + n, :].astype(BF16))
            done += n
    for dst, rows in TAIL_ZERO_ROWS:
        wp_ref[dst:dst + rows, :] = jnp.zeros((rows, wp_ref.shape[1]), BF16)

    acc_ref[...] += lax.dot_general(x_ref[...], wp_ref[...], _NT,
                                    preferred_element_type=F32)

    @pl.when(kk == pl.num_programs(1) - 1)
    def _():
        o_ref[...] = acc_ref[...].astype(o_ref.dtype)


def _in_proj_tail(x, wt, layer, tm, tk):
    m, k = x.shape
    first = D_MAIN // TAIL_PART_ROWS
    w_specs = [pl.BlockSpec((None, TAIL_PART_ROWS, tk),
                            lambda i, kk, p=p: (layer, first + p, kk))
               for p in range(TAIL_PARTS)]
    return pl.pallas_call(
        _in_proj_tail_kernel,
        grid=(m // tm, k // tk),
        in_specs=[pl.BlockSpec((tm, tk), lambda i, kk: (i, kk))] + w_specs,
        out_specs=pl.BlockSpec((tm, D_TAIL), lambda i, kk: (i, 0)),
        out_shape=jax.ShapeDtypeStruct((m, D_TAIL), BF16),
        scratch_shapes=[pltpu.VMEM((D_TAIL, tk), BF16),
                        pltpu.VMEM((tm, D_TAIL), F32)],
        compiler_params=_params(("parallel", "arbitrary")),
        name="in_proj_tail",
    )(x, *([wt] * TAIL_PARTS))


def _matmul_residual_kernel(x_ref, w_ref, r_ref, o_ref):
    o_ref[...] = r_ref[...] + jnp.dot(x_ref[...], w_ref[...].astype(BF16),
                                      preferred_element_type=F32)


def _matmul_residual(x, w, layer, res, tm, tn):
    m, k = x.shape
    n = w.shape[2]
    return pl.pallas_call(
        _matmul_residual_kernel,
        grid=(m // tm, n // tn),
        in_specs=[pl.BlockSpec((tm, k), lambda i, j: (i, 0)),
                  pl.BlockSpec((None, k, tn), lambda i, j: (layer, 0, j)),
                  pl.BlockSpec((tm, tn), lambda i, j: (i, j))],
        out_specs=pl.BlockSpec((tm, tn), lambda i, j: (i, j)),
        out_shape=jax.ShapeDtypeStruct((m, n), F32),
        compiler_params=_params(("parallel", "parallel")),
        name="out_proj",
    )(x, w, res)


def _gmlp_kernel(u_ref, v_ref, z_ref, gv_ref, w_ref, bias_ref, go_ref, o_ref,
                 y_ref):
    tm = u_ref.shape[0]
    t_idx = lax.broadcasted_iota(jnp.int32, (CHUNK, CHUNK), 0)
    s_idx = lax.broadcasted_iota(jnp.int32, (CHUNK, CHUNK), 1)
    causal = s_idx <= t_idx

    def chunk(c, carry):
        rows = pl.ds(pl.multiple_of(c * CHUNK, CHUNK), CHUNK)
        ssq = jnp.zeros((CHUNK, 1), F32)
        for g in range(A_GROUPS):
            cols = slice(g * A_CH, (g + 1) * A_CH)
            vg = _gelu(v_ref[rows, cols].astype(F32))
            vn = vg * _rms_scale(vg, A_CH) * gv_ref[:, cols]
            w = jnp.where(causal, w_ref[g], 0.0).astype(BF16)
            sv = jnp.dot(w, vn.astype(BF16), preferred_element_type=F32)
            sv = sv + bias_ref[:, cols]
            y = _gelu(u_ref[rows, cols].astype(F32)) * sv
            ssq = ssq + jnp.sum(y * y, axis=-1, keepdims=True)
            y_ref[:, cols] = y
        r = lax.rsqrt(ssq * (1.0 / A_WIDTH) + EPS)
        z = z_ref[rows, :].astype(F32)
        o_ref[rows, :] = (y_ref[...] * r * go_ref[...] * _silu(z)).astype(o_ref.dtype)
        return carry

    lax.fori_loop(0, tm // CHUNK, chunk, 0)


def _gmlp(proj, g_v, w_s, b_s, g_out_a, tm=512):
    m = proj.shape[0]
    bias = jnp.repeat(b_s.T, A_CH, axis=1)
    col = lambda off: (lambda i: (i, off // A_WIDTH))
    return pl.pallas_call(
        _gmlp_kernel,
        grid=(m // tm,),
        in_specs=[pl.BlockSpec((tm, A_WIDTH), col(OFF_U)),
                  pl.BlockSpec((tm, A_WIDTH), col(OFF_V)),
                  pl.BlockSpec((tm, A_WIDTH), col(OFF_ZA)),
                  pl.BlockSpec((1, A_WIDTH), lambda i: (0, 0)),
                  pl.BlockSpec((A_GROUPS, CHUNK, CHUNK), lambda i: (0, 0, 0)),
                  pl.BlockSpec((CHUNK, A_WIDTH), lambda i: (0, 0)),
                  pl.BlockSpec((1, A_WIDTH), lambda i: (0, 0))],
        out_specs=pl.BlockSpec((tm, A_WIDTH), lambda i: (i, 0)),
        out_shape=jax.ShapeDtypeStruct((m, A_WIDTH), BF16),
        scratch_shapes=[pltpu.VMEM((CHUNK, A_WIDTH), F32)],
        compiler_params=_params(("parallel",)),
        name="gmlp",
    )(proj, proj, proj, g_v.reshape(1, A_WIDTH), w_s, bias,
      g_out_a.reshape(1, A_WIDTH))


SB_T = 128
SB_SUB = 128
SB_NSUB = 3
SB_W = SB_NSUB * SB_SUB
SB_HG = 6
SB_STAGE = 6
SB_ZERO_WEIGHT_LOG = -105.0


def _sb_kernel(q_ref, k_ref, v_ref, o_ref, acc_ref, carry_ref):
    t, w, sub = SB_T, SB_W, SB_SUB
    i = pl.program_id(2)
    scale = HEAD_DIM ** -0.5
    q_pos = i * t + lax.broadcasted_iota(jnp.int32, (t, w), 0)
    k_off = lax.broadcasted_iota(jnp.int32, (t, w), 1)
    row = lax.broadcasted_iota(jnp.int32, (2 * sub, sub), 0) % sub
    col = lax.broadcasted_iota(jnp.int32, (2 * sub, sub), 1)
    neg_suffix = jnp.where(row > col, -1.0, 0.0).astype(BF16)

    acc_ref[...] = jnp.zeros_like(acc_ref)
    carry_ref[...] = jnp.zeros_like(carry_ref)

    def window(start, bound):
        valid = (start + k_off) < jnp.minimum(q_pos, bound)
        ks = pl.ds(pl.multiple_of(start, sub), w)
        worst = None
        for h0 in range(0, SB_HG, SB_STAGE):
            worst = stage(range(h0, h0 + SB_STAGE), valid, ks, worst)
        return worst

    def stage(heads, valid, ks, worst):
        log_beta, softplus, pieces = {}, {}, []
        for h in heads:
            hc = slice(h * HEAD_DIM, (h + 1) * HEAD_DIM)
            z = lax.dot_general(q_ref[0, :, hc], k_ref[0, ks, hc],
                                (((1,), (1,)), ((), ())),
                                preferred_element_type=F32) * scale
            sp = jnp.maximum(z, 0.0) + jnp.log(1.0 + jnp.exp(-jnp.abs(z)))
            log_beta[h] = z - sp
            sp = jnp.where(valid, sp, 0.0)
            softplus[h] = sp
            hi = sp.astype(BF16)
            lo = (sp - hi.astype(F32)).astype(BF16)
            for c in range(SB_NSUB):
                pieces.append(jnp.concatenate(
                    [hi[:, c * sub:(c + 1) * sub], lo[:, c * sub:(c + 1) * sub]],
                    axis=1))
        local = jnp.dot(jnp.concatenate(pieces, axis=0), neg_suffix,
                        preferred_element_type=F32)
        for n, h in enumerate(heads):
            hc = slice(h * HEAD_DIM, (h + 1) * HEAD_DIM)
            sp = softplus[h]
            running = carry_ref[h]
            after = [None] * SB_NSUB
            for c in reversed(range(SB_NSUB)):
                base = (n * SB_NSUB + c) * t
                after[c] = local[base:base + t] + running
                running = running - jnp.sum(sp[:, c * sub:(c + 1) * sub],
                                            axis=-1, keepdims=True)
            a = jnp.where(valid,
                          jnp.exp(log_beta[h] + jnp.concatenate(after, axis=1)),
                          0.0)
            acc_ref[h] += jnp.dot(a.astype(BF16), v_ref[0, ks, hc],
                                  preferred_element_type=F32)
            carry_ref[h] = running
            top = jnp.max(running)
            worst = top if worst is None else jnp.maximum(worst, top)
        return worst

    first = jnp.maximum(i + 1 - SB_NSUB, 0) * sub
    worst = window(first, jnp.int32(2 ** 30))

    def more(state):
        start, worst = state
        return jnp.logical_and(start > 0, worst > SB_ZERO_WEIGHT_LOG)

    def step(state):
        start, _ = state
        nxt = jnp.maximum(start - w, 0)
        return nxt, window(nxt, start)

    lax.while_loop(more, step, (first, worst))
    for h in range(SB_HG):
        o_ref[0, :, h * HEAD_DIM:(h + 1) * HEAD_DIM] = acc_ref[h].astype(o_ref.dtype)


def _stick_breaking(proj3):
    b, s, _ = proj3.shape
    t, gw = SB_T, SB_HG * HEAD_DIM
    blk = lambda off: off // gw
    return pl.pallas_call(
        _sb_kernel,
        grid=(b, HEADS // SB_HG, s // t),
        in_specs=[
            pl.BlockSpec((1, t, gw), lambda bi, g, i: (bi, i, blk(OFF_QB) + g)),
            pl.BlockSpec((1, s, gw), lambda bi, g, i: (bi, 0, blk(OFF_KB) + g)),
            pl.BlockSpec((1, s, gw), lambda bi, g, i: (bi, 0, blk(OFF_VB) + g)),
        ],
        out_specs=pl.BlockSpec((1, t, gw), lambda bi, g, i: (bi, i, g)),
        out_shape=jax.ShapeDtypeStruct((b, s, B_WIDTH), BF16),
        scratch_shapes=[pltpu.VMEM((SB_HG, t, HEAD_DIM), F32),
                        pltpu.VMEM((SB_HG, t, 1), F32)],
        compiler_params=_params(("parallel", "parallel", "arbitrary")),
        name="stick_breaking",
    )(proj3, proj3, proj3)


def _rope_tile(x, cos_ref, sin_ref):
    return x * cos_ref[...] + pltpu.roll(x, ROPE_X2, axis=1) * sin_ref[...]


def _mla_q_kernel(cq_ref, g_ref, w_ref, cos_ref, sin_ref, o_ref):
    scale = (C_NOPE + C_ROPE) ** -0.5 * math.log2(math.e)
    x = cq_ref[...].astype(F32)
    xn = (x * _rms_scale(x, Q_LORA) * g_ref[...]).astype(BF16)
    q = jnp.dot(xn, w_ref[...], preferred_element_type=F32)
    for h in range(HEADS):
        lo = h * C_QK
        o_ref[:, lo:lo + C_NOPE] = (q[:, lo:lo + C_NOPE] * scale).astype(o_ref.dtype)
        rope = _rope_tile(q[:, lo + C_NOPE:lo + C_QK], cos_ref, sin_ref)
        o_ref[:, lo + C_NOPE:lo + C_QK] = (rope * scale).astype(o_ref.dtype)


VT_ONES = 16
VT_ROWS = HEAD_DIM + VT_ONES


def _mla_kv_kernel(ckv_ref, kr_ref, g_ref, wk_ref, wvt_ref, cos_ref, sin_ref,
                   k_ref, vt_ref):
    tm = ckv_ref.shape[0]
    x = ckv_ref[...].astype(F32)
    xn = (x * _rms_scale(x, KV_LORA) * g_ref[...]).astype(BF16)
    kn = jnp.dot(xn, wk_ref[...], preferred_element_type=F32).astype(k_ref.dtype)
    kr = _rope_tile(kr_ref[...].astype(F32), cos_ref, sin_ref).astype(k_ref.dtype)
    for h in range(HEADS):
        k_ref[:, h * C_QK:h * C_QK + C_NOPE] = kn[:, h * C_NOPE:(h + 1) * C_NOPE]
        k_ref[:, h * C_QK + C_NOPE:(h + 1) * C_QK] = kr
    vt = lax.dot_general(wvt_ref[...], xn, _NT, preferred_element_type=F32)
    for h in range(HEADS):
        vt_ref[h * VT_ROWS:h * VT_ROWS + HEAD_DIM, :] = (
            vt[h * HEAD_DIM:(h + 1) * HEAD_DIM, :].astype(vt_ref.dtype))
        vt_ref[h * VT_ROWS + HEAD_DIM:(h + 1) * VT_ROWS, :] = jnp.ones(
            (VT_ONES, tm), vt_ref.dtype)


def _mla_project(proj, g_q, g_kv, w_uq_p, w_k_p, w_vt_p, tabs, b, s, tm=512):
    m = proj.shape[0]
    tiles = s // tm
    tab_spec = pl.BlockSpec((tm, ROPE_TILE), lambda i: (i, 0))
    q_cat = pl.pallas_call(
        _mla_q_kernel,
        grid=(m // tm,),
        in_specs=[pl.BlockSpec((tm, Q_LORA), lambda i: (i, TAIL_CQ // Q_LORA)),
                  pl.BlockSpec((1, Q_LORA), lambda i: (0, 0)),
                  pl.BlockSpec((Q_LORA, HEADS * C_QK), lambda i: (0, 0)),
                  tab_spec, tab_spec],
        out_specs=pl.BlockSpec((tm, HEADS * C_QK), lambda i: (i, 0)),
        out_shape=jax.ShapeDtypeStruct((m, HEADS * C_QK), BF16),
        compiler_params=_params(("parallel",)),
        name="mla_q",
    )(proj, g_q.reshape(1, Q_LORA), w_uq_p, *tabs)
    k_cat, vt = pl.pallas_call(
        _mla_kv_kernel,
        grid=(m // tm,),
        in_specs=[pl.BlockSpec((tm, KV_LORA), lambda i: (i, TAIL_CKV // KV_LORA)),
                  pl.BlockSpec((tm, ROPE_TILE), lambda i: (i, TAIL_KR // ROPE_TILE)),
                  pl.BlockSpec((1, KV_LORA), lambda i: (0, 0)),
                  pl.BlockSpec((KV_LORA, HEADS * C_NOPE), lambda i: (0, 0)),
                  pl.BlockSpec((C_WIDTH, KV_LORA), lambda i: (0, 0)),
                  tab_spec, tab_spec],
        out_specs=[pl.BlockSpec((tm, HEADS * C_QK), lambda i: (i, 0)),
                   pl.BlockSpec((None, HEADS * VT_ROWS, tm),
                                lambda i: (i // tiles, 0, i % tiles))],
        out_shape=[jax.ShapeDtypeStruct((m, HEADS * C_QK), BF16),
                   jax.ShapeDtypeStruct((b, HEADS * VT_ROWS, s), BF16)],
        compiler_params=_params(("parallel",)),
        name="mla_kv",
    )(proj, proj, g_kv.reshape(1, KV_LORA), w_k_p, w_vt_p, *tabs)
    return q_cat, k_cat, vt


MLA_T = 256
MLA_W = 512
MLA_HG = 6


def _mla_attn_kernel(q_ref, k_ref, vt_ref, o_ref, m_ref, acc_ref, s_ref,
                     smax_ref):
    t, w = MLA_T, MLA_W
    i = pl.program_id(2)
    k_off = lax.broadcasted_iota(jnp.int32, (w, t), 0)
    q_pos = i * t + lax.broadcasted_iota(jnp.int32, (w, t), 1)

    m_ref[...] = jnp.full_like(m_ref, -jnp.inf)
    acc_ref[...] = jnp.zeros_like(acc_ref)

    def scores(j, slot):
        ks = pl.ds(pl.multiple_of(j * w, w), w)
        for h in range(MLA_HG):
            s = lax.dot_general(k_ref[0, ks, h * C_QK:(h + 1) * C_QK],
                                q_ref[0, :, h * C_QK:(h + 1) * C_QK], _NT,
                                preferred_element_type=F32)
            s_ref[slot, h] = s
            smax_ref[slot, h] = jnp.max(s, axis=0, keepdims=True)

    def softmax_pv(j, slot, masked):
        start = pl.multiple_of(j * w, w)
        ks = pl.ds(start, w)
        for h in range(MLA_HG):
            s = s_ref[slot, h]
            if masked:
                s = jnp.where((start + k_off) <= q_pos, s, -jnp.inf)
                s_max = jnp.max(s, axis=0, keepdims=True)
            else:
                s_max = smax_ref[slot, h]
            m_old = m_ref[h]
            m_new = jnp.maximum(m_old, s_max)
            alpha = jnp.exp2(m_old - m_new)
            p = jnp.exp2(s - m_new)
            acc_ref[h] = alpha * acc_ref[h] + jnp.dot(
                vt_ref[0, h * VT_ROWS:(h + 1) * VT_ROWS, ks], p.astype(BF16),
                preferred_element_type=F32)
            m_ref[h] = m_new

    last = (i * t) // w

    scores(0, 0)

    def pair(jj, carry):
        j = 2 * jj
        scores(j + 1, 1)
        softmax_pv(j, 0, False)
        scores(j + 2, 0)
        softmax_pv(j + 1, 1, False)
        return carry

    lax.fori_loop(0, last // 2, pair, 0)

    @pl.when(last % 2 == 0)
    def _():
        softmax_pv(last, 0, True)

    @pl.when(last % 2 == 1)
    def _():
        scores(last, 1)
        softmax_pv(last - 1, 0, False)
        softmax_pv(last, 1, True)
    for h in range(MLA_HG):
        acc = acc_ref[h]
        out_t = acc[:HEAD_DIM] / acc[HEAD_DIM:HEAD_DIM + 1]
        o_ref[0, :, h * HEAD_DIM:(h + 1) * HEAD_DIM] = out_t.T.astype(o_ref.dtype)


def _mla_attention(q_cat, k_cat, vt, b, s):
    t, hg = MLA_T, MLA_HG
    q3 = q_cat.reshape(b, s, HEADS * C_QK)
    k3 = k_cat.reshape(b, s, HEADS * C_QK)
    return pl.pallas_call(
        _mla_attn_kernel,
        grid=(b, HEADS // hg, s // t),
        in_specs=[pl.BlockSpec((1, t, hg * C_QK), lambda bi, g, i: (bi, i, g)),
                  pl.BlockSpec((1, s, hg * C_QK), lambda bi, g, i: (bi, 0, g)),
                  pl.BlockSpec((1, hg * VT_ROWS, s), lambda bi, g, i: (bi, g, 0))],
        out_specs=pl.BlockSpec((1, t, hg * HEAD_DIM), lambda bi, g, i: (bi, i, g)),
        out_shape=jax.ShapeDtypeStruct((b, s, C_WIDTH), BF16),
        scratch_shapes=[pltpu.VMEM((hg, 1, t), F32),
                        pltpu.VMEM((hg, VT_ROWS, t), F32),
                        pltpu.VMEM((2, hg, MLA_W, t), F32),
                        pltpu.VMEM((2, hg, 1, t), F32)],
        compiler_params=_params(("parallel", "parallel", "arbitrary")),
        name="mla_attention",
    )(q3, k3, vt)


GATE_ROWS = 32


def _gate_kernel(ya_ref, yb_ref, zb_ref, yc_ref, zc_ref, g_ref, o_ref):
    o_ref[:, :A_WIDTH] = ya_ref[...]

    def chunk(c, carry):
        rows = pl.ds(pl.multiple_of(c * GATE_ROWS, GATE_ROWS), GATE_ROWS)
        lo = A_WIDTH
        for y_ref, z_ref, width in ((yb_ref, zb_ref, B_WIDTH),
                                    (yc_ref, zc_ref, C_WIDTH)):
            y = y_ref[rows, :].astype(F32)
            z = z_ref[rows, :].astype(F32)
            g = g_ref[:, lo:lo + width]
            o_ref[rows, lo:lo + width] = (y * _rms_scale(y, width) * g
                                          * _silu(z)).astype(o_ref.dtype)
            lo += width
        return carry

    lax.fori_loop(0, ya_ref.shape[0] // GATE_ROWS, chunk, 0)


def _gate(ya, yb, yc, main, tail, g_out, tm=512):
    m = ya.shape[0]
    return pl.pallas_call(
        _gate_kernel,
        grid=(m // tm,),
        in_specs=[pl.BlockSpec((tm, A_WIDTH), lambda i: (i, 0)),
                  pl.BlockSpec((tm, B_WIDTH), lambda i: (i, 0)),
                  pl.BlockSpec((tm, B_WIDTH), lambda i: (i, OFF_ZB // B_WIDTH)),
                  pl.BlockSpec((tm, C_WIDTH), lambda i: (i, 0)),
                  pl.BlockSpec((tm, C_WIDTH), lambda i: (i, TAIL_ZC // C_WIDTH)),
                  pl.BlockSpec((1, D_MODEL), lambda i: (0, 0))],
        out_specs=pl.BlockSpec((tm, D_MODEL), lambda i: (i, 0)),
        out_shape=jax.ShapeDtypeStruct((m, D_MODEL), BF16),
        compiler_params=_params(("parallel",)),
        name="gate",
    )(ya, yb, main, yc, tail, g_out.reshape(1, D_MODEL))


def _prep_w_uq(w):
    w = w.reshape(Q_LORA, HEADS, C_NOPE + C_ROPE)
    gap = jnp.zeros((Q_LORA, HEADS, ROPE_X2 - ROPE_HALF), w.dtype)
    w = jnp.concatenate([w[:, :, :C_NOPE + ROPE_HALF], gap,
                         w[:, :, C_NOPE + ROPE_HALF:], gap], axis=2)
    return w.reshape(Q_LORA, HEADS * C_QK).astype(BF16)


def _prep_w_ukv(w):
    w = w.reshape(KV_LORA, HEADS, C_NOPE + HEAD_DIM)
    wk = w[:, :, :C_NOPE].reshape(KV_LORA, HEADS * C_NOPE)
    wv_t = w[:, :, C_NOPE:].reshape(KV_LORA, C_WIDTH).T
    return wk.astype(BF16), wv_t.astype(BF16)


def _rope_tables(positions):
    inv_freq = 1.0 / (ROPE_THETA ** (jnp.arange(0, C_ROPE, 2, dtype=F32) / C_ROPE))
    ang = positions.astype(F32).reshape(-1, 1) * inv_freq
    cos, sin = jnp.cos(ang), jnp.sin(ang)
    gap = jnp.zeros((ang.shape[0], ROPE_X2 - ROPE_HALF), F32)
    return (jnp.concatenate([cos, gap, cos, gap], axis=1),
            jnp.concatenate([-sin, gap, sin, gap], axis=1))


def kernel(x, positions, g_pre, w_in, a_g_v, a_w_s, a_b_s, c_g_q, c_g_kv,
           c_w_uq, c_w_ukv, g_out, w_out, g_final):
    b, s, d = x.shape
    m = b * s
    depth = w_in.shape[0]
    tabs = _rope_tables(positions)
    w_in_t = jnp.swapaxes(w_in, 1, 2)
    h = x.reshape(m, d)
    for l in range(depth):
        hn = _rmsnorm(h, g_pre[l], BF16)
        main = _in_proj_main(hn, w_in_t, l, tm=1024, tn=512)
        tail = _in_proj_tail(hn, w_in_t, l, tm=1024, tk=512)
        ya = _gmlp(main, a_g_v[l], a_w_s[l], a_b_s[l], g_out[l, :A_WIDTH])
        yb = _stick_breaking(main.reshape(b, s, D_MAIN)).reshape(m, B_WIDTH)
        w_k_p, w_vt_p = _prep_w_ukv(c_w_ukv[l])
        q_cat, k_cat, vt = _mla_project(tail, c_g_q[l], c_g_kv[l],
                                        _prep_w_uq(c_w_uq[l]), w_k_p, w_vt_p,
                                        tabs, b, s)
        yc = _mla_attention(q_cat, k_cat, vt, b, s).reshape(m, C_WIDTH)
        y = _gate(ya, yb, yc, main, tail, g_out[l])
        h = _matmul_residual(y, w_out, l, h, tm=1024, tn=512)
    return _rmsnorm(h, g_final, F32).reshape(b, s, d)
```

```python
import functools
import math

import jax
import jax.numpy as jnp
from jax import lax
from jax.experimental import pallas as pl
from jax.experimental.pallas import tpu as pltpu

F32 = jnp.float32
BF16 = jnp.bfloat16

D_MODEL = 4096
A_GROUPS = 8
A_CH = 128
A_WIDTH = A_GROUPS * A_CH
CHUNK = 128
HEADS = 12
HEAD_DIM = 128
B_WIDTH = HEADS * HEAD_DIM
C_NOPE = 128
C_ROPE = 64
C_WIDTH = HEADS * HEAD_DIM
Q_LORA = 768
KV_LORA = 512
ROPE_THETA = 10000.0
EPS = 1e-6

C_QK = 256
ROPE_HALF = C_ROPE // 2
ROPE_TILE = 128
ROPE_X2 = ROPE_TILE // 2

OFF_U = 0
OFF_V = OFF_U + A_WIDTH
OFF_ZA = OFF_V + A_WIDTH
OFF_QB = OFF_ZA + A_WIDTH
OFF_KB = OFF_QB + B_WIDTH
OFF_VB = OFF_KB + B_WIDTH
OFF_ZB = OFF_VB + B_WIDTH
D_MAIN = OFF_ZB + B_WIDTH
REF_CQ = D_MAIN
REF_CKV = REF_CQ + Q_LORA
REF_KR = REF_CKV + KV_LORA
REF_ZC = REF_KR + C_ROPE
TAIL_ZC = 0
TAIL_CKV = TAIL_ZC + C_WIDTH
TAIL_KR = TAIL_CKV + KV_LORA
TAIL_CQ = TAIL_KR + C_QK
D_TAIL = TAIL_CQ + Q_LORA

VMEM_LIMIT_BYTES = 56 * 1024 * 1024


def _params(sem):
    return pltpu.CompilerParams(dimension_semantics=sem,
                                vmem_limit_bytes=VMEM_LIMIT_BYTES)


def _gelu(x):
    c = math.sqrt(2.0 / math.pi)
    return x * (0.5 * (1.0 + jnp.tanh(c * (x + 0.044715 * (x * x * x)))))


def _silu(z):
    return z / (1.0 + jnp.exp(-z))


def _rms_scale(x, width):
    return lax.rsqrt(jnp.sum(x * x, axis=-1, keepdims=True) * (1.0 / width) + EPS)


def _rmsnorm_kernel(x_ref, g_ref, o_ref):
    x = x_ref[...]
    o_ref[...] = (x * _rms_scale(x, x.shape[-1]) * g_ref[...]).astype(o_ref.dtype)


def _rmsnorm(x, g, out_dtype, tm=512):
    m, d = x.shape
    return pl.pallas_call(
        _rmsnorm_kernel,
        grid=(m // tm,),
        in_specs=[pl.BlockSpec((tm, d), lambda i: (i, 0)),
                  pl.BlockSpec((1, d), lambda i: (0, 0))],
        out_specs=pl.BlockSpec((tm, d), lambda i: (i, 0)),
        out_shape=jax.ShapeDtypeStruct((m, d), out_dtype),
        compiler_params=_params(("parallel",)),
        name="rmsnorm",
    )(x, g.reshape(1, d))


_NT = (((1,), (1,)), ((), ()))


def _in_proj_main_kernel(x_ref, wt_ref, o_ref):
    o_ref[...] = lax.dot_general(x_ref[...], wt_ref[...].astype(BF16), _NT,
                                 preferred_element_type=F32).astype(o_ref.dtype)


def _in_proj_main(x, wt, layer, tm, tn):
    m, k = x.shape
    return pl.pallas_call(
        _in_proj_main_kernel,
        grid=(m // tm, D_MAIN // tn),
        in_specs=[pl.BlockSpec((tm, k), lambda i, j: (i, 0)),
                  pl.BlockSpec((None, tn, k), lambda i, j: (layer, j, 0))],
        out_specs=pl.BlockSpec((tm, tn), lambda i, j: (i, j)),
        out_shape=jax.ShapeDtypeStruct((m, D_MAIN), BF16),
        compiler_params=_params(("parallel", "parallel")),
        name="in_proj_main",
    )(x, wt)


TAIL_PART_ROWS = 576
TAIL_PARTS = 5
TAIL_COPIES = (
    (TAIL_ZC, REF_ZC - D_MAIN, C_WIDTH),
    (TAIL_CKV, REF_CKV - D_MAIN, KV_LORA),
    (TAIL_KR, REF_KR - D_MAIN, ROPE_HALF),
    (TAIL_KR + ROPE_X2, REF_KR - D_MAIN + ROPE_HALF, ROPE_HALF),
    (TAIL_CQ, REF_CQ - D_MAIN, Q_LORA),
)
TAIL_ZERO_ROWS = ((TAIL_KR + ROPE_HALF, ROPE_X2 - ROPE_HALF),
                  (TAIL_KR + ROPE_X2 + ROPE_HALF, ROPE_X2 - ROPE_HALF),
                  (TAIL_KR + ROPE_TILE, TAIL_CQ - TAIL_KR - ROPE_TILE))


def _in_proj_tail_kernel(x_ref, *refs):
    w_refs = refs[:TAIL_PARTS]
    o_ref, wp_even_ref, wp_odd_ref, acc_ref = refs[TAIL_PARTS:]
    s = pl.program_id(1)
    nk = pl.num_programs(1) - 1

    def gather(wp_ref):
        for dst, src, rows in TAIL_COPIES:
            done = 0
            while done < rows:
                part, lo = divmod(src + done, TAIL_PART_ROWS)
                n = min(rows - done, TAIL_PART_ROWS - lo)
                wp_ref[dst + done:dst + done + n, :] = (
                    w_refs[part][lo:lo + n, :].astype(BF16))
                done += n
        for dst, rows in TAIL_ZERO_ROWS:
            wp_ref[dst:dst + rows, :] = jnp.zeros((rows, wp_ref.shape[1]), BF16)

    @pl.when(s == 0)
    def _():
        acc_ref[...] = jnp.zeros_like(acc_ref)
        gather(wp_even_ref)

    for odd_step, wp_cur, wp_next in ((1, wp_even_ref, wp_odd_ref),
                                      (0, wp_odd_ref, wp_even_ref)):
        @pl.when(jnp.logical_and(s >= 1, s % 2 == odd_step))
        def _(wp_cur=wp_cur, wp_next=wp_next):
            gather(wp_next)
            acc_ref[...] += lax.dot_general(x_ref[...], wp_cur[...], _NT,
                                            preferred_element_type=F32)

    @pl.when(s == nk)
    def _():
        o_ref[...] = acc_ref[...].astype(o_ref.dtype)


def _in_proj_tail(x, wt, layer, tm, tk):
    m, k = x.shape
    nk = k // tk
    first = D_MAIN // TAIL_PART_ROWS
    w_specs = [pl.BlockSpec((None, TAIL_PART_ROWS, tk),
                            lambda i, s, p=p: (layer, first + p,
                                               jnp.minimum(s, nk - 1)))
               for p in range(TAIL_PARTS)]
    return pl.pallas_call(
        _in_proj_tail_kernel,
        grid=(m // tm, nk + 1),
        in_specs=[pl.BlockSpec((tm, tk),
                               lambda i, s: (i, jnp.maximum(s - 1, 0)))] + w_specs,
        out_specs=pl.BlockSpec((tm, D_TAIL), lambda i, s: (i, 0)),
        out_shape=jax.ShapeDtypeStruct((m, D_TAIL), BF16),
        scratch_shapes=[pltpu.VMEM((D_TAIL, tk), BF16),
                        pltpu.VMEM((D_TAIL, tk), BF16),
                        pltpu.VMEM((tm, D_TAIL), F32)],
        compiler_params=_params(("parallel", "arbitrary")),
        name="in_proj_tail",
    )(x, *([wt] * TAIL_PARTS))


def _matmul_residual_kernel(x_ref, w_ref, r_ref, o_ref):
    o_ref[...] = r_ref[...] + jnp.dot(x_ref[...], w_ref[...].astype(BF16),
                                      preferred_element_type=F32)


def _matmul_residual(x, w, layer, res, tm, tn):
    m, k = x.shape
    n = w.shape[2]
    return pl.pallas_call(
        _matmul_residual_kernel,
        grid=(m // tm, n // tn),
        in_specs=[pl.BlockSpec((tm, k), lambda i, j: (i, 0)),
                  pl.BlockSpec((None, k, tn), lambda i, j: (layer, 0, j)),
                  pl.BlockSpec((tm, tn), lambda i, j: (i, j))],
        out_specs=pl.BlockSpec((tm, tn), lambda i, j: (i, j)),
        out_shape=jax.ShapeDtypeStruct((m, n), F32),
        compiler_params=_params(("parallel", "parallel")),
        name="out_proj",
    )(x, w, res)


def _gmlp_kernel(u_ref, v_ref, z_ref, gv_ref, w_ref, bias_ref, go_ref, o_ref,
                 y_ref):
    tm = u_ref.shape[0]
    t_idx = lax.broadcasted_iota(jnp.int32, (CHUNK, CHUNK), 0)
    s_idx = lax.broadcasted_iota(jnp.int32, (CHUNK, CHUNK), 1)
    causal = s_idx <= t_idx

    def chunk(c, carry):
        rows = pl.ds(pl.multiple_of(c * CHUNK, CHUNK), CHUNK)
        ssq = jnp.zeros((CHUNK, 1), F32)
        for g in range(A_GROUPS):
            cols = slice(g * A_CH, (g + 1) * A_CH)
            vg = _gelu(v_ref[rows, cols].astype(F32))
            vn = vg * _rms_scale(vg, A_CH) * gv_ref[:, cols]
            w = jnp.where(causal, w_ref[g], 0.0).astype(BF16)
            sv = jnp.dot(w, vn.astype(BF16), preferred_element_type=F32)
            sv = sv + bias_ref[:, cols]
            y = _gelu(u_ref[rows, cols].astype(F32)) * sv
            ssq = ssq + jnp.sum(y * y, axis=-1, keepdims=True)
            y_ref[:, cols] = y
        r = lax.rsqrt(ssq * (1.0 / A_WIDTH) + EPS)
        z = z_ref[rows, :].astype(F32)
        o_ref[rows, :] = (y_ref[...] * r * go_ref[...] * _silu(z)).astype(o_ref.dtype)
        return carry

    lax.fori_loop(0, tm // CHUNK, chunk, 0)


def _gmlp(proj, g_v, w_s, b_s, g_out_a, tm=512):
    m = proj.shape[0]
    bias = jnp.repeat(b_s.T, A_CH, axis=1)
    col = lambda off: (lambda i: (i, off // A_WIDTH))
    return pl.pallas_call(
        _gmlp_kernel,
        grid=(m // tm,),
        in_specs=[pl.BlockSpec((tm, A_WIDTH), col(OFF_U)),
                  pl.BlockSpec((tm, A_WIDTH), col(OFF_V)),
                  pl.BlockSpec((tm, A_WIDTH), col(OFF_ZA)),
                  pl.BlockSpec((1, A_WIDTH), lambda i: (0, 0)),
                  pl.BlockSpec((A_GROUPS, CHUNK, CHUNK), lambda i: (0, 0, 0)),
                  pl.BlockSpec((CHUNK, A_WIDTH), lambda i: (0, 0)),
                  pl.BlockSpec((1, A_WIDTH), lambda i: (0, 0))],
        out_specs=pl.BlockSpec((tm, A_WIDTH), lambda i: (i, 0)),
        out_shape=jax.ShapeDtypeStruct((m, A_WIDTH), BF16),
        scratch_shapes=[pltpu.VMEM((CHUNK, A_WIDTH), F32)],
        compiler_params=_params(("parallel",)),
        name="gmlp",
    )(proj, proj, proj, g_v.reshape(1, A_WIDTH), w_s, bias,
      g_out_a.reshape(1, A_WIDTH))


SB_T = 128
SB_SUB = 128
SB_NSUB = 3
SB_W = SB_NSUB * SB_SUB
SB_HG = 6
SB_STAGE = 6
SB_ZERO_WEIGHT_LOG = -105.0


def _sb_kernel(q_ref, k_ref, v_ref, o_ref, acc_ref, carry_ref):
    t, w, sub = SB_T, SB_W, SB_SUB
    i = pl.program_id(2)
    scale = HEAD_DIM ** -0.5
    q_pos = i * t + lax.broadcasted_iota(jnp.int32, (t, w), 0)
    k_off = lax.broadcasted_iota(jnp.int32, (t, w), 1)
    row = lax.broadcasted_iota(jnp.int32, (2 * sub, sub), 0) % sub
    col = lax.broadcasted_iota(jnp.int32, (2 * sub, sub), 1)
    neg_suffix = jnp.where(row > col, -1.0, 0.0).astype(BF16)

    acc_ref[...] = jnp.zeros_like(acc_ref)
    carry_ref[...] = jnp.zeros_like(carry_ref)

    def window(start, bound):
        valid = (start + k_off) < jnp.minimum(q_pos, bound)
        ks = pl.ds(pl.multiple_of(start, sub), w)
        worst = None
        for h0 in range(0, SB_HG, SB_STAGE):
            worst = stage(range(h0, h0 + SB_STAGE), valid, ks, worst)
        return worst

    def stage(heads, valid, ks, worst):
        log_beta, softplus, pieces = {}, {}, []
        for h in heads:
            hc = slice(h * HEAD_DIM, (h + 1) * HEAD_DIM)
            z = lax.dot_general(q_ref[0, :, hc], k_ref[0, ks, hc],
                                (((1,), (1,)), ((), ())),
                                preferred_element_type=F32) * scale
            sp = jnp.maximum(z, 0.0) + jnp.log(1.0 + jnp.exp(-jnp.abs(z)))
            log_beta[h] = z - sp
            sp = jnp.where(valid, sp, 0.0)
            softplus[h] = sp
            hi = sp.astype(BF16)
            lo = (sp - hi.astype(F32)).astype(BF16)
            for c in range(SB_NSUB):
                pieces.append(jnp.concatenate(
                    [hi[:, c * sub:(c + 1) * sub], lo[:, c * sub:(c + 1) * sub]],
                    axis=1))
        local = jnp.dot(jnp.concatenate(pieces, axis=0), neg_suffix,
                        preferred_element_type=F32)
        for n, h in enumerate(heads):
            hc = slice(h * HEAD_DIM, (h + 1) * HEAD_DIM)
            sp = softplus[h]
            running = carry_ref[h]
            after = [None] * SB_NSUB
            for c in reversed(range(SB_NSUB)):
                base = (n * SB_NSUB + c) * t
                after[c] = local[base:base + t] + running
                running = running - jnp.sum(sp[:, c * sub:(c + 1) * sub],
                                            axis=-1, keepdims=True)
            a = jnp.where(valid,
                          jnp.exp(log_beta[h] + jnp.concatenate(after, axis=1)),
                          0.0)
            acc_ref[h] += jnp.dot(a.astype(BF16), v_ref[0, ks, hc],
                                  preferred_element_type=F32)
            carry_ref[h] = running
            top = jnp.max(running)
            worst = top if worst is None else jnp.maximum(worst, top)
        return worst

    first = jnp.maximum(i + 1 - SB_NSUB, 0) * sub
    worst = window(first, jnp.int32(2 ** 30))

    def more(state):
        start, worst = state
        return jnp.logical_and(start > 0, worst > SB_ZERO_WEIGHT_LOG)

    def step(state):
        start, _ = state
        nxt = jnp.maximum(start - w, 0)
        return nxt, window(nxt, start)

    lax.while_loop(more, step, (first, worst))
    for h in range(SB_HG):
        o_ref[0, :, h * HEAD_DIM:(h + 1) * HEAD_DIM] = acc_ref[h].astype(o_ref.dtype)


def _stick_breaking(proj3):
    b, s, _ = proj3.shape
    t, gw = SB_T, SB_HG * HEAD_DIM
    blk = lambda off: off // gw
    return pl.pallas_call(
        _sb_kernel,
        grid=(b, HEADS // SB_HG, s // t),
        in_specs=[
            pl.BlockSpec((1, t, gw), lambda bi, g, i: (bi, i, blk(OFF_QB) + g)),
            pl.BlockSpec((1, s, gw), lambda bi, g, i: (bi, 0, blk(OFF_KB) + g)),
            pl.BlockSpec((1, s, gw), lambda bi, g, i: (bi, 0, blk(OFF_VB) + g)),
        ],
        out_specs=pl.BlockSpec((1, t, gw), lambda bi, g, i: (bi, i, g)),
        out_shape=jax.ShapeDtypeStruct((b, s, B_WIDTH), BF16),
        scratch_shapes=[pltpu.VMEM((SB_HG, t, HEAD_DIM), F32),
                        pltpu.VMEM((SB_HG, t, 1), F32)],
        compiler_params=_params(("parallel", "parallel", "arbitrary")),
        name="stick_breaking",
    )(proj3, proj3, proj3)


def _rope_tile(x, cos_ref, sin_ref):
    return x * cos_ref[...] + pltpu.roll(x, ROPE_X2, axis=1) * sin_ref[...]


def _mla_q_kernel(cq_ref, g_ref, w_ref, cos_ref, sin_ref, o_ref):
    scale = (C_NOPE + C_ROPE) ** -0.5 * math.log2(math.e)
    x = cq_ref[...].astype(F32)
    xn = (x * _rms_scale(x, Q_LORA) * g_ref[...]).astype(BF16)
    q = jnp.dot(xn, w_ref[...], preferred_element_type=F32)
    for h in range(HEADS):
        lo = h * C_QK
        o_ref[:, lo:lo + C_NOPE] = (q[:, lo:lo + C_NOPE] * scale).astype(o_ref.dtype)
        rope = _rope_tile(q[:, lo + C_NOPE:lo + C_QK], cos_ref, sin_ref)
        o_ref[:, lo + C_NOPE:lo + C_QK] = (rope * scale).astype(o_ref.dtype)


VT_ONES = 16
VT_ROWS = HEAD_DIM + VT_ONES


def _mla_kv_kernel(ckv_ref, kr_ref, g_ref, wk_ref, wvt_ref, cos_ref, sin_ref,
                   k_ref, vt_ref):
    tm = ckv_ref.shape[0]
    x = ckv_ref[...].astype(F32)
    xn = (x * _rms_scale(x, KV_LORA) * g_ref[...]).astype(BF16)
    kn = jnp.dot(xn, wk_ref[...], preferred_element_type=F32).astype(k_ref.dtype)
    kr = _rope_tile(kr_ref[...].astype(F32), cos_ref, sin_ref).astype(k_ref.dtype)
    for h in range(HEADS):
        k_ref[:, h * C_QK:h * C_QK + C_NOPE] = kn[:, h * C_NOPE:(h + 1) * C_NOPE]
        k_ref[:, h * C_QK + C_NOPE:(h + 1) * C_QK] = kr
    vt = lax.dot_general(wvt_ref[...], xn, _NT, preferred_element_type=F32)
    for h in range(HEADS):
        vt_ref[h * VT_ROWS:h * VT_ROWS + HEAD_DIM, :] = (
            vt[h * HEAD_DIM:(h + 1) * HEAD_DIM, :].astype(vt_ref.dtype))
        vt_ref[h * VT_ROWS + HEAD_DIM:(h + 1) * VT_ROWS, :] = jnp.ones(
            (VT_ONES, tm), vt_ref.dtype)


def _mla_project(proj, g_q, g_kv, w_uq_p, w_k_p, w_vt_p, tabs, b, s, tm=512):
    m = proj.shape[0]
    tiles = s // tm
    tab_spec = pl.BlockSpec((tm, ROPE_TILE), lambda i: (i, 0))
    q_cat = pl.pallas_call(
        _mla_q_kernel,
        grid=(m // tm,),
        in_specs=[pl.BlockSpec((tm, Q_LORA), lambda i: (i, TAIL_CQ // Q_LORA)),
                  pl.BlockSpec((1, Q_LORA), lambda i: (0, 0)),
                  pl.BlockSpec((Q_LORA, HEADS * C_QK), lambda i: (0, 0)),
                  tab_spec, tab_spec],
        out_specs=pl.BlockSpec((tm, HEADS * C_QK), lambda i: (i, 0)),
        out_shape=jax.ShapeDtypeStruct((m, HEADS * C_QK), BF16),
        compiler_params=_params(("parallel",)),
        name="mla_q",
    )(proj, g_q.reshape(1, Q_LORA), w_uq_p, *tabs)
    k_cat, vt = pl.pallas_call(
        _mla_kv_kernel,
        grid=(m // tm,),
        in_specs=[pl.BlockSpec((tm, KV_LORA), lambda i: (i, TAIL_CKV // KV_LORA)),
                  pl.BlockSpec((tm, ROPE_TILE), lambda i: (i, TAIL_KR // ROPE_TILE)),
                  pl.BlockSpec((1, KV_LORA), lambda i: (0, 0)),
                  pl.BlockSpec((KV_LORA, HEADS * C_NOPE), lambda i: (0, 0)),
                  pl.BlockSpec((C_WIDTH, KV_LORA), lambda i: (0, 0)),
                  tab_spec, tab_spec],
        out_specs=[pl.BlockSpec((tm, HEADS * C_QK), lambda i: (i, 0)),
                   pl.BlockSpec((None, HEADS * VT_ROWS, tm),
                                lambda i: (i // tiles, 0, i % tiles))],
        out_shape=[jax.ShapeDtypeStruct((m, HEADS * C_QK), BF16),
                   jax.ShapeDtypeStruct((b, HEADS * VT_ROWS, s), BF16)],
        compiler_params=_params(("parallel",)),
        name="mla_kv",
    )(proj, proj, g_kv.reshape(1, KV_LORA), w_k_p, w_vt_p, *tabs)
    return q_cat, k_cat, vt


MLA_T = 256
MLA_W = 512
MLA_HG = 6


def _mla_attn_kernel(q_ref, k_ref, vt_ref, o_ref, m_ref, acc_ref, s_ref,
                     smax_ref):
    t, w = MLA_T, MLA_W
    i = pl.program_id(2)
    k_off = lax.broadcasted_iota(jnp.int32, (w, t), 0)
    q_pos = i * t + lax.broadcasted_iota(jnp.int32, (w, t), 1)

    m_ref[...] = jnp.full_like(m_ref, -jnp.inf)
    acc_ref[...] = jnp.zeros_like(acc_ref)

    def scores(j, slot):
        ks = pl.ds(pl.multiple_of(j * w, w), w)
        for h in range(MLA_HG):
            s = lax.dot_general(k_ref[0, ks, h * C_QK:(h + 1) * C_QK],
                                q_ref[0, :, h * C_QK:(h + 1) * C_QK], _NT,
                                preferred_element_type=F32)
            s_ref[slot, h] = s
            smax_ref[slot, h] = jnp.max(s, axis=0, keepdims=True)

    def softmax_pv(j, slot, masked):
        start = pl.multiple_of(j * w, w)
        ks = pl.ds(start, w)
        for h in range(MLA_HG):
            s = s_ref[slot, h]
            if masked:
                s = jnp.where((start + k_off) <= q_pos, s, -jnp.inf)
                s_max = jnp.max(s, axis=0, keepdims=True)
            else:
                s_max = smax_ref[slot, h]
            m_old = m_ref[h]
            m_new = jnp.maximum(m_old, s_max)
            alpha = jnp.exp2(m_old - m_new)
            p = jnp.exp2(s - m_new)
            acc_ref[h] = alpha * acc_ref[h] + jnp.dot(
                vt_ref[0, h * VT_ROWS:(h + 1) * VT_ROWS, ks], p.astype(BF16),
                preferred_element_type=F32)
            m_ref[h] = m_new

    last = (i * t) // w

    scores(0, 0)

    def pair(jj, carry):
        j = 2 * jj
        scores(j + 1, 1)
        softmax_pv(j, 0, False)
        scores(j + 2, 0)
        softmax_pv(j + 1, 1, False)
        return carry

    lax.fori_loop(0, last // 2, pair, 0)

    @pl.when(last % 2 == 0)
    def _():
        softmax_pv(last, 0, True)

    @pl.when(last % 2 == 1)
    def _():
        scores(last, 1)
        softmax_pv(last - 1, 0, False)
        softmax_pv(last, 1, True)
    for h in range(MLA_HG):
        acc = acc_ref[h]
        out_t = acc[:HEAD_DIM] / acc[HEAD_DIM:HEAD_DIM + 1]
        o_ref[0, :, h * HEAD_DIM:(h + 1) * HEAD_DIM] = out_t.T.astype(o_ref.dtype)


def _mla_attention(q_cat, k_cat, vt, b, s):
    t, hg = MLA_T, MLA_HG
    q3 = q_cat.reshape(b, s, HEADS * C_QK)
    k3 = k_cat.reshape(b, s, HEADS * C_QK)
    return pl.pallas_call(
        _mla_attn_kernel,
        grid=(b, HEADS // hg, s // t),
        in_specs=[pl.BlockSpec((1, t, hg * C_QK), lambda bi, g, i: (bi, i, g)),
                  pl.BlockSpec((1, s, hg * C_QK), lambda bi, g, i: (bi, 0, g)),
                  pl.BlockSpec((1, hg * VT_ROWS, s), lambda bi, g, i: (bi, g, 0))],
        out_specs=pl.BlockSpec((1, t, hg * HEAD_DIM), lambda bi, g, i: (bi, i, g)),
        out_shape=jax.ShapeDtypeStruct((b, s, C_WIDTH), BF16),
        scratch_shapes=[pltpu.VMEM((hg, 1, t), F32),
                        pltpu.VMEM((hg, VT_ROWS, t), F32),
                        pltpu.VMEM((2, hg, MLA_W, t), F32),
                        pltpu.VMEM((2, hg, 1, t), F32)],
        compiler_params=_params(("parallel", "parallel", "arbitrary")),
        name="mla_attention",
    )(q3, k3, vt)


GATE_ROWS = 32


def _gate_kernel(ya_ref, yb_ref, zb_ref, yc_ref, zc_ref, g_ref, o_ref):
    o_ref[:, :A_WIDTH] = ya_ref[...]

    def chunk(c, carry):
        rows = pl.ds(pl.multiple_of(c * GATE_ROWS, GATE_ROWS), GATE_ROWS)
        lo = A_WIDTH
        for y_ref, z_ref, width in ((yb_ref, zb_ref, B_WIDTH),
                                    (yc_ref, zc_ref, C_WIDTH)):
            y = y_ref[rows, :].astype(F32)
            z = z_ref[rows, :].astype(F32)
            g = g_ref[:, lo:lo + width]
            o_ref[rows, lo:lo + width] = (y * _rms_scale(y, width) * g
                                          * _silu(z)).astype(o_ref.dtype)
            lo += width
        return carry

    lax.fori_loop(0, ya_ref.shape[0] // GATE_ROWS, chunk, 0)


def _gate(ya, yb, yc, main, tail, g_out, tm=512):
    m = ya.shape[0]
    return pl.pallas_call(
        _gate_kernel,
        grid=(m // tm,),
        in_specs=[pl.BlockSpec((tm, A_WIDTH), lambda i: (i, 0)),
                  pl.BlockSpec((tm, B_WIDTH), lambda i: (i, 0)),
                  pl.BlockSpec((tm, B_WIDTH), lambda i: (i, OFF_ZB // B_WIDTH)),
                  pl.BlockSpec((tm, C_WIDTH), lambda i: (i, 0)),
                  pl.BlockSpec((tm, C_WIDTH), lambda i: (i, TAIL_ZC // C_WIDTH)),
                  pl.BlockSpec((1, D_MODEL), lambda i: (0, 0))],
        out_specs=pl.BlockSpec((tm, D_MODEL), lambda i: (i, 0)),
        out_shape=jax.ShapeDtypeStruct((m, D_MODEL), BF16),
        compiler_params=_params(("parallel",)),
        name="gate",
    )(ya, yb, main, yc, tail, g_out.reshape(1, D_MODEL))


def _prep_w_uq(w):
    w = w.reshape(Q_LORA, HEADS, C_NOPE + C_ROPE)
    gap = jnp.zeros((Q_LORA, HEADS, ROPE_X2 - ROPE_HALF), w.dtype)
    w = jnp.concatenate([w[:, :, :C_NOPE + ROPE_HALF], gap,
                         w[:, :, C_NOPE + ROPE_HALF:], gap], axis=2)
    return w.reshape(Q_LORA, HEADS * C_QK).astype(BF16)


def _prep_w_ukv(w):
    w = w.reshape(KV_LORA, HEADS, C_NOPE + HEAD_DIM)
    wk = w[:, :, :C_NOPE].reshape(KV_LORA, HEADS * C_NOPE)
    wv_t = w[:, :, C_NOPE:].reshape(KV_LORA, C_WIDTH).T
    return wk.astype(BF16), wv_t.astype(BF16)


def _rope_tables(positions):
    inv_freq = 1.0 / (ROPE_THETA ** (jnp.arange(0, C_ROPE, 2, dtype=F32) / C_ROPE))
    ang = positions.astype(F32).reshape(-1, 1) * inv_freq
    cos, sin = jnp.cos(ang), jnp.sin(ang)
    gap = jnp.zeros((ang.shape[0], ROPE_X2 - ROPE_HALF), F32)
    return (jnp.concatenate([cos, gap, cos, gap], axis=1),
            jnp.concatenate([-sin, gap, sin, gap], axis=1))


def kernel(x, positions, g_pre, w_in, a_g_v, a_w_s, a_b_s, c_g_q, c_g_kv,
           c_w_uq, c_w_ukv, g_out, w_out, g_final):
    b, s, d = x.shape
    m = b * s
    depth = w_in.shape[0]
    tabs = _rope_tables(positions)
    w_in_t = jnp.swapaxes(w_in, 1, 2)
    h = x.reshape(m, d)
    for l in range(depth):
        hn = _rmsnorm(h, g_pre[l], BF16)
        main = _in_proj_main(hn, w_in_t, l, tm=1024, tn=512)
        tail = _in_proj_tail(hn, w_in_t, l, tm=1024, tk=512)
        ya = _gmlp(main, a_g_v[l], a_w_s[l], a_b_s[l], g_out[l, :A_WIDTH])
        yb = _stick_breaking(main.reshape(b, s, D_MAIN)).reshape(m, B_WIDTH)
        w_k_p, w_vt_p = _prep_w_ukv(c_w_ukv[l])
        q_cat, k_cat, vt = _mla_project(tail, c_g_q[l], c_g_kv[l],
                                        _prep_w_uq(c_w_uq[l]), w_k_p, w_vt_p,
                                        tabs, b, s)
        yc = _mla_attention(q_cat, k_cat, vt, b, s).reshape(m, C_WIDTH)
        y = _gate(ya, yb, yc, main, tail, g_out[l])
        h = _matmul_residual(y, w_out, l, h, tm=1024, tn=512)
    return _rmsnorm(h, g_final, F32).reshape(b, s, d)
```

```python
import functools
import math

import jax
import jax.numpy as jnp
from jax import lax
from jax.experimental import pallas as pl
from jax.experimental.pallas import tpu as pltpu

F32 = jnp.float32
BF16 = jnp.bfloat16

D_MODEL = 4096
A_GROUPS = 8
A_CH = 128
A_WIDTH = A_GROUPS * A_CH
CHUNK = 128
HEADS = 12
HEAD_DIM = 128
B_WIDTH = HEADS * HEAD_DIM
C_NOPE = 128
C_ROPE = 64
C_WIDTH = HEADS * HEAD_DIM
Q_LORA = 768
KV_LORA = 512
ROPE_THETA = 10000.0
EPS = 1e-6

C_QK = 256
ROPE_HALF = C_ROPE // 2
ROPE_TILE = 128
ROPE_X2 = ROPE_TILE // 2

OFF_U = 0
OFF_V = OFF_U + A_WIDTH
OFF_ZA = OFF_V + A_WIDTH
OFF_QB = OFF_ZA + A_WIDTH
OFF_KB = OFF_QB + B_WIDTH
OFF_VB = OFF_KB + B_WIDTH
OFF_ZB = OFF_VB + B_WIDTH
D_MAIN = OFF_ZB + B_WIDTH
REF_CQ = D_MAIN
REF_CKV = REF_CQ + Q_LORA
REF_KR = REF_CKV + KV_LORA
REF_ZC = REF_KR + C_ROPE
TAIL_ZC = 0
TAIL_CKV = TAIL_ZC + C_WIDTH
TAIL_KR = TAIL_CKV + KV_LORA
TAIL_CQ = TAIL_KR + C_QK
D_TAIL = TAIL_CQ + Q_LORA

VMEM_LIMIT_BYTES = 56 * 1024 * 1024


def _params(sem):
    return pltpu.CompilerParams(dimension_semantics=sem,
                                vmem_limit_bytes=VMEM_LIMIT_BYTES)


def _gelu(x):
    c = math.sqrt(2.0 / math.pi)
    return x * (0.5 * (1.0 + jnp.tanh(c * (x + 0.044715 * (x * x * x)))))


def _silu(z):
    return z / (1.0 + jnp.exp(-z))


def _rms_scale(x, width):
    return lax.rsqrt(jnp.sum(x * x, axis=-1, keepdims=True) * (1.0 / width) + EPS)


def _rmsnorm_kernel(x_ref, g_ref, o_ref):
    x = x_ref[...]
    o_ref[...] = (x * _rms_scale(x, x.shape[-1]) * g_ref[...]).astype(o_ref.dtype)


def _rmsnorm(x, g, out_dtype, tm=512):
    m, d = x.shape
    return pl.pallas_call(
        _rmsnorm_kernel,
        grid=(m // tm,),
        in_specs=[pl.BlockSpec((tm, d), lambda i: (i, 0)),
                  pl.BlockSpec((1, d), lambda i: (0, 0))],
        out_specs=pl.BlockSpec((tm, d), lambda i: (i, 0)),
        out_shape=jax.ShapeDtypeStruct((m, d), out_dtype),
        compiler_params=_params(("parallel",)),
        name="rmsnorm",
    )(x, g.reshape(1, d))


_NT = (((1,), (1,)), ((), ()))


def _in_proj_main_kernel(x_ref, wt_ref, o_ref, wb_ref):
    @pl.when(pl.program_id(1) == 0)
    def _():
        wb_ref[...] = wt_ref[...].astype(BF16)

    o_ref[...] = lax.dot_general(x_ref[...], wb_ref[...], _NT,
                                 preferred_element_type=F32).astype(o_ref.dtype)


def _in_proj_main(x, wt, layer, tm, tn):
    m, k = x.shape
    return pl.pallas_call(
        _in_proj_main_kernel,
        grid=(D_MAIN // tn, m // tm),
        in_specs=[pl.BlockSpec((tm, k), lambda j, i: (i, 0)),
                  pl.BlockSpec((None, tn, k), lambda j, i: (layer, j, 0))],
        out_specs=pl.BlockSpec((tm, tn), lambda j, i: (i, j)),
        out_shape=jax.ShapeDtypeStruct((m, D_MAIN), BF16),
        scratch_shapes=[pltpu.VMEM((tn, k), BF16)],
        compiler_params=_params(("parallel", "arbitrary")),
        name="in_proj_main",
    )(x, wt)


TAIL_PART_ROWS = 576
TAIL_PARTS = 5
TAIL_COPIES = (
    (TAIL_ZC, REF_ZC - D_MAIN, C_WIDTH),
    (TAIL_CKV, REF_CKV - D_MAIN, KV_LORA),
    (TAIL_KR, REF_KR - D_MAIN, ROPE_HALF),
    (TAIL_KR + ROPE_X2, REF_KR - D_MAIN + ROPE_HALF, ROPE_HALF),
    (TAIL_CQ, REF_CQ - D_MAIN, Q_LORA),
)
TAIL_ZERO_ROWS = ((TAIL_KR + ROPE_HALF, ROPE_X2 - ROPE_HALF),
                  (TAIL_KR + ROPE_X2 + ROPE_HALF, ROPE_X2 - ROPE_HALF),
                  (TAIL_KR + ROPE_TILE, TAIL_CQ - TAIL_KR - ROPE_TILE))


def _in_proj_tail_kernel(x_ref, *refs):
    w_refs = refs[:TAIL_PARTS]
    o_ref, wp_ref, acc_ref = refs[TAIL_PARTS:]
    kk = pl.program_id(1)

    @pl.when(kk == 0)
    def _():
        acc_ref[...] = jnp.zeros_like(acc_ref)

    for dst, src, rows in TAIL_COPIES:
        done = 0
        while done < rows:
            part, lo = divmod(src + done, TAIL_PART_ROWS)
            n = min(rows - done, TAIL_PART_ROWS - lo)
            wp_ref[dst + done:dst + done + n, :] = (
                w_refs[part][lo:lo + n, :].astype(BF16))
            done += n
    for dst, rows in TAIL_ZERO_ROWS:
        wp_ref[dst:dst + rows, :] = jnp.zeros((rows, wp_ref.shape[1]), BF16)

    acc_ref[...] += lax.dot_general(x_ref[...], wp_ref[...], _NT,
                                    preferred_element_type=F32)

    @pl.when(kk == pl.num_programs(1) - 1)
    def _():
        o_ref[...] = acc_ref[...].astype(o_ref.dtype)


def _in_proj_tail(x, wt, layer, tm, tk):
    m, k = x.shape
    first = D_MAIN // TAIL_PART_ROWS
    w_specs = [pl.BlockSpec((None, TAIL_PART_ROWS, tk),
                            lambda i, kk, p=p: (layer, first + p, kk))
               for p in range(TAIL_PARTS)]
    return pl.pallas_call(
        _in_proj_tail_kernel,
        grid=(m // tm, k // tk),
        in_specs=[pl.BlockSpec((tm, tk), lambda i, kk: (i, kk))] + w_specs,
        out_specs=pl.BlockSpec((tm, D_TAIL), lambda i, kk: (i, 0)),
        out_shape=jax.ShapeDtypeStruct((m, D_TAIL), BF16),
        scratch_shapes=[pltpu.VMEM((D_TAIL, tk), BF16),
                        pltpu.VMEM((tm, D_TAIL), F32)],
        compiler_params=_params(("parallel", "arbitrary")),
        name="in_proj_tail",
    )(x, *([wt] * TAIL_PARTS))


def _matmul_residual_kernel(x_ref, w_ref, r_ref, o_ref, wb_ref):
    @pl.when(pl.program_id(1) == 0)
    def _():
        wb_ref[...] = w_ref[...].astype(BF16)

    o_ref[...] = r_ref[...] + jnp.dot(x_ref[...], wb_ref[...],
                                      preferred_element_type=F32)


def _matmul_residual(x, w, layer, res, tm, tn):
    m, k = x.shape
    n = w.shape[2]
    return pl.pallas_call(
        _matmul_residual_kernel,
        grid=(n // tn, m // tm),
        in_specs=[pl.BlockSpec((tm, k), lambda j, i: (i, 0)),
                  pl.BlockSpec((None, k, tn), lambda j, i: (layer, 0, j)),
                  pl.BlockSpec((tm, tn), lambda j, i: (i, j))],
        out_specs=pl.BlockSpec((tm, tn), lambda j, i: (i, j)),
        out_shape=jax.ShapeDtypeStruct((m, n), F32),
        scratch_shapes=[pltpu.VMEM((k, tn), BF16)],
        compiler_params=_params(("parallel", "arbitrary")),
        name="out_proj",
    )(x, w, res)


def _gmlp_kernel(u_ref, v_ref, z_ref, gv_ref, w_ref, bias_ref, go_ref, o_ref,
                 y_ref):
    tm = u_ref.shape[0]
    t_idx = lax.broadcasted_iota(jnp.int32, (CHUNK, CHUNK), 0)
    s_idx = lax.broadcasted_iota(jnp.int32, (CHUNK, CHUNK), 1)
    causal = s_idx <= t_idx

    def chunk(c, carry):
        rows = pl.ds(pl.multiple_of(c * CHUNK, CHUNK), CHUNK)
        ssq = jnp.zeros((CHUNK, 1), F32)
        for g in range(A_GROUPS):
            cols = slice(g * A_CH, (g + 1) * A_CH)
            vg = _gelu(v_ref[rows, cols].astype(F32))
            vn = vg * _rms_scale(vg, A_CH) * gv_ref[:, cols]
            w = jnp.where(causal, w_ref[g], 0.0).astype(BF16)
            sv = jnp.dot(w, vn.astype(BF16), preferred_element_type=F32)
            sv = sv + bias_ref[:, cols]
            y = _gelu(u_ref[rows, cols].astype(F32)) * sv
            ssq = ssq + jnp.sum(y * y, axis=-1, keepdims=True)
            y_ref[:, cols] = y
        r = lax.rsqrt(ssq * (1.0 / A_WIDTH) + EPS)
        z = z_ref[rows, :].astype(F32)
        o_ref[rows, :] = (y_ref[...] * r * go_ref[...] * _silu(z)).astype(o_ref.dtype)
        return carry

    lax.fori_loop(0, tm // CHUNK, chunk, 0)


def _gmlp(proj, g_v, w_s, b_s, g_out_a, tm=512):
    m = proj.shape[0]
    bias = jnp.repeat(b_s.T, A_CH, axis=1)
    col = lambda off: (lambda i: (i, off // A_WIDTH))
    return pl.pallas_call(
        _gmlp_kernel,
        grid=(m // tm,),
        in_specs=[pl.BlockSpec((tm, A_WIDTH), col(OFF_U)),
                  pl.BlockSpec((tm, A_WIDTH), col(OFF_V)),
                  pl.BlockSpec((tm, A_WIDTH), col(OFF_ZA)),
                  pl.BlockSpec((1, A_WIDTH), lambda i: (0, 0)),
                  pl.BlockSpec((A_GROUPS, CHUNK, CHUNK), lambda i: (0, 0, 0)),
                  pl.BlockSpec((CHUNK, A_WIDTH), lambda i: (0, 0)),
                  pl.BlockSpec((1, A_WIDTH), lambda i: (0, 0))],
        out_specs=pl.BlockSpec((tm, A_WIDTH), lambda i: (i, 0)),
        out_shape=jax.ShapeDtypeStruct((m, A_WIDTH), BF16),
        scratch_shapes=[pltpu.VMEM((CHUNK, A_WIDTH), F32)],
        compiler_params=_params(("parallel",)),
        name="gmlp",
    )(proj, proj, proj, g_v.reshape(1, A_WIDTH), w_s, bias,
      g_out_a.reshape(1, A_WIDTH))


SB_T = 128
SB_SUB = 128
SB_NSUB = 3
SB_W = SB_NSUB * SB_SUB
SB_HG = 6
SB_STAGE = 6
SB_ZERO_WEIGHT_LOG = -105.0


def _sb_kernel(q_ref, k_ref, v_ref, o_ref, acc_ref, carry_ref):
    t, w, sub = SB_T, SB_W, SB_SUB
    i = pl.program_id(2)
    scale = HEAD_DIM ** -0.5
    q_pos = i * t + lax.broadcasted_iota(jnp.int32, (t, w), 0)
    k_off = lax.broadcasted_iota(jnp.int32, (t, w), 1)
    row = lax.broadcasted_iota(jnp.int32, (2 * sub, sub), 0) % sub
    col = lax.broadcasted_iota(jnp.int32, (2 * sub, sub), 1)
    neg_suffix = jnp.where(row > col, -1.0, 0.0).astype(BF16)

    acc_ref[...] = jnp.zeros_like(acc_ref)
    carry_ref[...] = jnp.zeros_like(carry_ref)

    def window(start, bound):
        valid = (start + k_off) < jnp.minimum(q_pos, bound)
        ks = pl.ds(pl.multiple_of(start, sub), w)
        worst = None
        for h0 in range(0, SB_HG, SB_STAGE):
            worst = stage(range(h0, h0 + SB_STAGE), valid, ks, worst)
        return worst

    def stage(heads, valid, ks, worst):
        log_beta, softplus, pieces = {}, {}, []
        for h in heads:
            hc = slice(h * HEAD_DIM, (h + 1) * HEAD_DIM)
            z = lax.dot_general(q_ref[0, :, hc], k_ref[0, ks, hc],
                                (((1,), (1,)), ((), ())),
                                preferred_element_type=F32) * scale
            sp = jnp.maximum(z, 0.0) + jnp.log(1.0 + jnp.exp(-jnp.abs(z)))
            log_beta[h] = z - sp
            sp = jnp.where(valid, sp, 0.0)
            softplus[h] = sp
            hi = sp.astype(BF16)
            lo = (sp - hi.astype(F32)).astype(BF16)
            for c in range(SB_NSUB):
                pieces.append(jnp.concatenate(
                    [hi[:, c * sub:(c + 1) * sub], lo[:, c * sub:(c + 1) * sub]],
                    axis=1))
        local = jnp.dot(jnp.concatenate(pieces, axis=0), neg_suffix,
                        preferred_element_type=F32)
        for n, h in enumerate(heads):
            hc = slice(h * HEAD_DIM, (h + 1) * HEAD_DIM)
            sp = softplus[h]
            running = carry_ref[h]
            after = [None] * SB_NSUB
            for c in reversed(range(SB_NSUB)):
                base = (n * SB_NSUB + c) * t
                after[c] = local[base:base + t] + running
                running = running - jnp.sum(sp[:, c * sub:(c + 1) * sub],
                                            axis=-1, keepdims=True)
            a = jnp.where(valid,
                          jnp.exp(log_beta[h] + jnp.concatenate(after, axis=1)),
                          0.0)
            acc_ref[h] += jnp.dot(a.astype(BF16), v_ref[0, ks, hc],
                                  preferred_element_type=F32)
            carry_ref[h] = running
            top = jnp.max(running)
            worst = top if worst is None else jnp.maximum(worst, top)
        return worst

    first = jnp.maximum(i + 1 - SB_NSUB, 0) * sub
    worst = window(first, jnp.int32(2 ** 30))

    def more(state):
        start, worst = state
        return jnp.logical_and(start > 0, worst > SB_ZERO_WEIGHT_LOG)

    def step(state):
        start, _ = state
        nxt = jnp.maximum(start - w, 0)
        return nxt, window(nxt, start)

    lax.while_loop(more, step, (first, worst))
    for h in range(SB_HG):
        o_ref[0, :, h * HEAD_DIM:(h + 1) * HEAD_DIM] = acc_ref[h].astype(o_ref.dtype)


def _stick_breaking(proj3):
    b, s, _ = proj3.shape
    t, gw = SB_T, SB_HG * HEAD_DIM
    blk = lambda off: off // gw
    return pl.pallas_call(
        _sb_kernel,
        grid=(b, HEADS // SB_HG, s // t),
        in_specs=[
            pl.BlockSpec((1, t, gw), lambda bi, g, i: (bi, i, blk(OFF_QB) + g)),
            pl.BlockSpec((1, s, gw), lambda bi, g, i: (bi, 0, blk(OFF_KB) + g)),
            pl.BlockSpec((1, s, gw), lambda bi, g, i: (bi, 0, blk(OFF_VB) + g)),
        ],
        out_specs=pl.BlockSpec((1, t, gw), lambda bi, g, i: (bi, i, g)),
        out_shape=jax.ShapeDtypeStruct((b, s, B_WIDTH), BF16),
        scratch_shapes=[pltpu.VMEM((SB_HG, t, HEAD_DIM), F32),
                        pltpu.VMEM((SB_HG, t, 1), F32)],
        compiler_params=_params(("parallel", "parallel", "arbitrary")),
        name="stick_breaking",
    )(proj3, proj3, proj3)


def _rope_tile(x, cos_ref, sin_ref):
    return x * cos_ref[...] + pltpu.roll(x, ROPE_X2, axis=1) * sin_ref[...]


def _mla_q_kernel(cq_ref, g_ref, w_ref, cos_ref, sin_ref, o_ref):
    scale = (C_NOPE + C_ROPE) ** -0.5 * math.log2(math.e)
    x = cq_ref[...].astype(F32)
    xn = (x * _rms_scale(x, Q_LORA) * g_ref[...]).astype(BF16)
    q = jnp.dot(xn, w_ref[...], preferred_element_type=F32)
    for h in range(HEADS):
        lo = h * C_QK
        o_ref[:, lo:lo + C_NOPE] = (q[:, lo:lo + C_NOPE] * scale).astype(o_ref.dtype)
        rope = _rope_tile(q[:, lo + C_NOPE:lo + C_QK], cos_ref, sin_ref)
        o_ref[:, lo + C_NOPE:lo + C_QK] = (rope * scale).astype(o_ref.dtype)


VT_ONES = 16
VT_ROWS = HEAD_DIM + VT_ONES


def _mla_kv_kernel(ckv_ref, kr_ref, g_ref, wk_ref, wvt_ref, cos_ref, sin_ref,
                   k_ref, vt_ref):
    tm = ckv_ref.shape[0]
    x = ckv_ref[...].astype(F32)
    xn = (x * _rms_scale(x, KV_LORA) * g_ref[...]).astype(BF16)
    kn = jnp.dot(xn, wk_ref[...], preferred_element_type=F32).astype(k_ref.dtype)
    kr = _rope_tile(kr_ref[...].astype(F32), cos_ref, sin_ref).astype(k_ref.dtype)
    for h in range(HEADS):
        k_ref[:, h * C_QK:h * C_QK + C_NOPE] = kn[:, h * C_NOPE:(h + 1) * C_NOPE]
        k_ref[:, h * C_QK + C_NOPE:(h + 1) * C_QK] = kr
    vt = lax.dot_general(wvt_ref[...], xn, _NT, preferred_element_type=F32)
    for h in range(HEADS):
        vt_ref[h * VT_ROWS:h * VT_ROWS + HEAD_DIM, :] = (
            vt[h * HEAD_DIM:(h + 1) * HEAD_DIM, :].astype(vt_ref.dtype))
        vt_ref[h * VT_ROWS + HEAD_DIM:(h + 1) * VT_ROWS, :] = jnp.ones(
            (VT_ONES, tm), vt_ref.dtype)


def _mla_project(proj, g_q, g_kv, w_uq_p, w_k_p, w_vt_p, tabs, b, s, tm=512):
    m = proj.shape[0]
    tiles = s // tm
    tab_spec = pl.BlockSpec((tm, ROPE_TILE), lambda i: (i, 0))
    q_cat = pl.pallas_call(
        _mla_q_kernel,
        grid=(m // tm,),
        in_specs=[pl.BlockSpec((tm, Q_LORA), lambda i: (i, TAIL_CQ // Q_LORA)),
                  pl.BlockSpec((1, Q_LORA), lambda i: (0, 0)),
                  pl.BlockSpec((Q_LORA, HEADS * C_QK), lambda i: (0, 0)),
                  tab_spec, tab_spec],
        out_specs=pl.BlockSpec((tm, HEADS * C_QK), lambda i: (i, 0)),
        out_shape=jax.ShapeDtypeStruct((m, HEADS * C_QK), BF16),
        compiler_params=_params(("parallel",)),
        name="mla_q",
    )(proj, g_q.reshape(1, Q_LORA), w_uq_p, *tabs)
    k_cat, vt = pl.pallas_call(
        _mla_kv_kernel,
        grid=(m // tm,),
        in_specs=[pl.BlockSpec((tm, KV_LORA), lambda i: (i, TAIL_CKV // KV_LORA)),
                  pl.BlockSpec((tm, ROPE_TILE), lambda i: (i, TAIL_KR // ROPE_TILE)),
                  pl.BlockSpec((1, KV_LORA), lambda i: (0, 0)),
                  pl.BlockSpec((KV_LORA, HEADS * C_NOPE), lambda i: (0, 0)),
                  pl.BlockSpec((C_WIDTH, KV_LORA), lambda i: (0, 0)),
                  tab_spec, tab_spec],
        out_specs=[pl.BlockSpec((tm, HEADS * C_QK), lambda i: (i, 0)),
                   pl.BlockSpec((None, HEADS * VT_ROWS, tm),
                                lambda i: (i // tiles, 0, i % tiles))],
        out_shape=[jax.ShapeDtypeStruct((m, HEADS * C_QK), BF16),
                   jax.ShapeDtypeStruct((b, HEADS * VT_ROWS, s), BF16)],
        compiler_params=_params(("parallel",)),
        name="mla_kv",
    )(proj, proj, g_kv.reshape(1, KV_LORA), w_k_p, w_vt_p, *tabs)
    return q_cat, k_cat, vt


MLA_T = 256
MLA_W = 512
MLA_HG = 6


def _mla_attn_kernel(q_ref, k_ref, vt_ref, o_ref, m_ref, acc_ref, s_ref,
                     smax_ref):
    t, w = MLA_T, MLA_W
    i = pl.program_id(2)
    k_off = lax.broadcasted_iota(jnp.int32, (w, t), 0)
    q_pos = i * t + lax.broadcasted_iota(jnp.int32, (w, t), 1)

    m_ref[...] = jnp.full_like(m_ref, -jnp.inf)
    acc_ref[...] = jnp.zeros_like(acc_ref)

    def scores(j, slot):
        ks = pl.ds(pl.multiple_of(j * w, w), w)
        for h in range(MLA_HG):
            s = lax.dot_general(k_ref[0, ks, h * C_QK:(h + 1) * C_QK],
                                q_ref[0, :, h * C_QK:(h + 1) * C_QK], _NT,
                                preferred_element_type=F32)
            s_ref[slot, h] = s
            smax_ref[slot, h] = jnp.max(s, axis=0, keepdims=True)

    def softmax_pv(j, slot, masked):
        start = pl.multiple_of(j * w, w)
        ks = pl.ds(start, w)
        for h in range(MLA_HG):
            s = s_ref[slot, h]
            if masked:
                s = jnp.where((start + k_off) <= q_pos, s, -jnp.inf)
                s_max = jnp.max(s, axis=0, keepdims=True)
            else:
                s_max = smax_ref[slot, h]
            m_old = m_ref[h]
            m_new = jnp.maximum(m_old, s_max)
            alpha = jnp.exp2(m_old - m_new)
            p = jnp.exp2(s - m_new)
            acc_ref[h] = alpha * acc_ref[h] + jnp.dot(
                vt_ref[0, h * VT_ROWS:(h + 1) * VT_ROWS, ks], p.astype(BF16),
                preferred_element_type=F32)
            m_ref[h] = m_new

    last = (i * t) // w

    scores(0, 0)

    def pair(jj, carry):
        j = 2 * jj
        scores(j + 1, 1)
        softmax_pv(j, 0, False)
        scores(j + 2, 0)
        softmax_pv(j + 1, 1, False)
        return carry

    lax.fori_loop(0, last // 2, pair, 0)

    @pl.when(last % 2 == 0)
    def _():
        softmax_pv(last, 0, True)

    @pl.when(last % 2 == 1)
    def _():
        scores(last, 1)
        softmax_pv(last - 1, 0, False)
        softmax_pv(last, 1, True)
    for h in range(MLA_HG):
        acc = acc_ref[h]
        out_t = acc[:HEAD_DIM] / acc[HEAD_DIM:HEAD_DIM + 1]
        o_ref[0, :, h * HEAD_DIM:(h + 1) * HEAD_DIM] = out_t.T.astype(o_ref.dtype)


def _mla_attention(q_cat, k_cat, vt, b, s):
    t, hg = MLA_T, MLA_HG
    q3 = q_cat.reshape(b, s, HEADS * C_QK)
    k3 = k_cat.reshape(b, s, HEADS * C_QK)
    return pl.pallas_call(
        _mla_attn_kernel,
        grid=(b, HEADS // hg, s // t),
        in_specs=[pl.BlockSpec((1, t, hg * C_QK), lambda bi, g, i: (bi, i, g)),
                  pl.BlockSpec((1, s, hg * C_QK), lambda bi, g, i: (bi, 0, g)),
                  pl.BlockSpec((1, hg * VT_ROWS, s), lambda bi, g, i: (bi, g, 0))],
        out_specs=pl.BlockSpec((1, t, hg * HEAD_DIM), lambda bi, g, i: (bi, i, g)),
        out_shape=jax.ShapeDtypeStruct((b, s, C_WIDTH), BF16),
        scratch_shapes=[pltpu.VMEM((hg, 1, t), F32),
                        pltpu.VMEM((hg, VT_ROWS, t), F32),
                        pltpu.VMEM((2, hg, MLA_W, t), F32),
                        pltpu.VMEM((2, hg, 1, t), F32)],
        compiler_params=_params(("parallel", "parallel", "arbitrary")),
        name="mla_attention",
    )(q3, k3, vt)


GATE_ROWS = 32


def _gate_kernel(ya_ref, yb_ref, zb_ref, yc_ref, zc_ref, g_ref, o_ref):
    o_ref[:, :A_WIDTH] = ya_ref[...]

    def chunk(c, carry):
        rows = pl.ds(pl.multiple_of(c * GATE_ROWS, GATE_ROWS), GATE_ROWS)
        lo = A_WIDTH
        for y_ref, z_ref, width in ((yb_ref, zb_ref, B_WIDTH),
                                    (yc_ref, zc_ref, C_WIDTH)):
            y = y_ref[rows, :].astype(F32)
            z = z_ref[rows, :].astype(F32)
            g = g_ref[:, lo:lo + width]
            o_ref[rows, lo:lo + width] = (y * _rms_scale(y, width) * g
                                          * _silu(z)).astype(o_ref.dtype)
            lo += width
        return carry

    lax.fori_loop(0, ya_ref.shape[0] // GATE_ROWS, chunk, 0)


def _gate(ya, yb, yc, main, tail, g_out, tm=512):
    m = ya.shape[0]
    return pl.pallas_call(
        _gate_kernel,
        grid=(m // tm,),
        in_specs=[pl.BlockSpec((tm, A_WIDTH), lambda i: (i, 0)),
                  pl.BlockSpec((tm, B_WIDTH), lambda i: (i, 0)),
                  pl.BlockSpec((tm, B_WIDTH), lambda i: (i, OFF_ZB // B_WIDTH)),
                  pl.BlockSpec((tm, C_WIDTH), lambda i: (i, 0)),
                  pl.BlockSpec((tm, C_WIDTH), lambda i: (i, TAIL_ZC // C_WIDTH)),
                  pl.BlockSpec((1, D_MODEL), lambda i: (0, 0))],
        out_specs=pl.BlockSpec((tm, D_MODEL), lambda i: (i, 0)),
        out_shape=jax.ShapeDtypeStruct((m, D_MODEL), BF16),
        compiler_params=_params(("parallel",)),
        name="gate",
    )(ya, yb, main, yc, tail, g_out.reshape(1, D_MODEL))


def _prep_w_uq(w):
    w = w.reshape(Q_LORA, HEADS, C_NOPE + C_ROPE)
    gap = jnp.zeros((Q_LORA, HEADS, ROPE_X2 - ROPE_HALF), w.dtype)
    w = jnp.concatenate([w[:, :, :C_NOPE + ROPE_HALF], gap,
                         w[:, :, C_NOPE + ROPE_HALF:], gap], axis=2)
    return w.reshape(Q_LORA, HEADS * C_QK).astype(BF16)


def _prep_w_ukv(w):
    w = w.reshape(KV_LORA, HEADS, C_NOPE + HEAD_DIM)
    wk = w[:, :, :C_NOPE].reshape(KV_LORA, HEADS * C_NOPE)
    wv_t = w[:, :, C_NOPE:].reshape(KV_LORA, C_WIDTH).T
    return wk.astype(BF16), wv_t.astype(BF16)


def _rope_tables(positions):
    inv_freq = 1.0 / (ROPE_THETA ** (jnp.arange(0, C_ROPE, 2, dtype=F32) / C_ROPE))
    ang = positions.astype(F32).reshape(-1, 1) * inv_freq
    cos, sin = jnp.cos(ang), jnp.sin(ang)
    gap = jnp.zeros((ang.shape[0], ROPE_X2 - ROPE_HALF), F32)
    return (jnp.concatenate([cos, gap, cos, gap], axis=1),
            jnp.concatenate([-sin, gap, sin, gap], axis=1))


def kernel(x, positions, g_pre, w_in, a_g_v, a_w_s, a_b_s, c_g_q, c_g_kv,
           c_w_uq, c_w_ukv, g_out, w_out, g_final):
    b, s, d = x.shape
    m = b * s
    depth = w_in.shape[0]
    tabs = _rope_tables(positions)
    w_in_t = jnp.swapaxes(w_in, 1, 2)
    h = x.reshape(m, d)
    for l in range(depth):
        hn = _rmsnorm(h, g_pre[l], BF16)
        main = _in_proj_main(hn, w_in_t, l, tm=1024, tn=512)
        tail = _in_proj_tail(hn, w_in_t, l, tm=1024, tk=512)
        ya = _gmlp(main, a_g_v[l], a_w_s[l], a_b_s[l], g_out[l, :A_WIDTH])
        yb = _stick_breaking(main.reshape(b, s, D_MAIN)).reshape(m, B_WIDTH)
        w_k_p, w_vt_p = _prep_w_ukv(c_w_ukv[l])
        q_cat, k_cat, vt = _mla_project(tail, c_g_q[l], c_g_kv[l],
                                        _prep_w_uq(c_w_uq[l]), w_k_p, w_vt_p,
                                        tabs, b, s)
        yc = _mla_attention(q_cat, k_cat, vt, b, s).reshape(m, C_WIDTH)
        y = _gate(ya, yb, yc, main, tail, g_out[l])
        h = _matmul_residual(y, w_out, l, h, tm=1024, tn=512)
    return _rmsnorm(h, g_final, F32).reshape(b, s, d)
```

```python
import math

import jax
import jax.numpy as jnp
from jax import lax
from jax.experimental import pallas as pl
from jax.experimental.pallas import tpu as pltpu

F32 = jnp.float32
BF16 = jnp.bfloat16

D_MODEL = 4096
A_GROUPS = 8
A_CH = 128
A_WIDTH = A_GROUPS * A_CH
CHUNK = 128
HEADS = 12
HEAD_DIM = 128
B_WIDTH = HEADS * HEAD_DIM
C_NOPE = 128
C_ROPE = 64
C_WIDTH = HEADS * HEAD_DIM
Q_LORA = 768
KV_LORA = 512
ROPE_THETA = 10000.0
EPS = 1e-6

C_QK = 256
ROPE_HALF = C_ROPE // 2
ROPE_TILE = 128
ROPE_X2 = ROPE_TILE // 2

OFF_U = 0
OFF_V = OFF_U + A_WIDTH
OFF_ZA = OFF_V + A_WIDTH
OFF_QB = OFF_ZA + A_WIDTH
OFF_KB = OFF_QB + B_WIDTH
OFF_VB = OFF_KB + B_WIDTH
OFF_ZB = OFF_VB + B_WIDTH
D_MAIN = OFF_ZB + B_WIDTH
REF_CQ = D_MAIN
REF_CKV = REF_CQ + Q_LORA
REF_KR = REF_CKV + KV_LORA
REF_ZC = REF_KR + C_ROPE
TAIL_ZC = 0
TAIL_CKV = TAIL_ZC + C_WIDTH
TAIL_KR = TAIL_CKV + KV_LORA
TAIL_CQ = TAIL_KR + C_QK
D_TAIL = TAIL_CQ + Q_LORA

V7X_VMEM_BYTES = 64 * 1024 * 1024
VMEM_LIMIT_BYTES = V7X_VMEM_BYTES * 7 // 8

NORM_TM = 512
PROJ_TM = 1024
PROJ_TN = 512
TAIL_TK = 512
MIXER_TM = 512


def _params(sem):
    return pltpu.CompilerParams(dimension_semantics=sem,
                                vmem_limit_bytes=VMEM_LIMIT_BYTES)


def _gelu(x):
    c = math.sqrt(2.0 / math.pi)
    return x * (0.5 * (1.0 + jnp.tanh(c * (x + 0.044715 * (x * x * x)))))


def _silu(z):
    return z / (1.0 + jnp.exp(-z))


def _rms_scale(x, width):
    return lax.rsqrt(jnp.sum(x * x, axis=-1, keepdims=True) * (1.0 / width) + EPS)


def _rmsnorm_kernel(x_ref, g_ref, o_ref):
    x = x_ref[...]
    o_ref[...] = (x * _rms_scale(x, x.shape[-1]) * g_ref[...]).astype(o_ref.dtype)


def _rmsnorm(x, g, out_dtype, tm=NORM_TM):
    m, d = x.shape
    assert m % tm == 0, (m, tm)
    return pl.pallas_call(
        _rmsnorm_kernel,
        grid=(m // tm,),
        in_specs=[pl.BlockSpec((tm, d), lambda i: (i, 0)),
                  pl.BlockSpec((1, d), lambda i: (0, 0))],
        out_specs=pl.BlockSpec((tm, d), lambda i: (i, 0)),
        out_shape=jax.ShapeDtypeStruct((m, d), out_dtype),
        compiler_params=_params(("parallel",)),
        name="rmsnorm",
    )(x, g.reshape(1, d))


_NT = (((1,), (1,)), ((), ()))


def _in_proj_main_kernel(x_ref, wt_ref, o_ref):
    o_ref[...] = lax.dot_general(x_ref[...], wt_ref[...].astype(BF16), _NT,
                                 preferred_element_type=F32).astype(o_ref.dtype)


def _in_proj_main(x, wt, layer, tm=PROJ_TM, tn=PROJ_TN):
    m, k = x.shape
    assert m % tm == 0 and D_MAIN % tn == 0 and wt.shape[2] == k, (m, tm, tn)
    return pl.pallas_call(
        _in_proj_main_kernel,
        grid=(m // tm, D_MAIN // tn),
        in_specs=[pl.BlockSpec((tm, k), lambda i, j: (i, 0)),
                  pl.BlockSpec((None, tn, k), lambda i, j: (layer, j, 0))],
        out_specs=pl.BlockSpec((tm, tn), lambda i, j: (i, j)),
        out_shape=jax.ShapeDtypeStruct((m, D_MAIN), BF16),
        compiler_params=_params(("parallel", "parallel")),
        name="in_proj_main",
    )(x, wt)


TAIL_PART_ROWS = 576
TAIL_PARTS = 5
TAIL_COPIES = (
    (TAIL_ZC, REF_ZC - D_MAIN, C_WIDTH),
    (TAIL_CKV, REF_CKV - D_MAIN, KV_LORA),
    (TAIL_KR, REF_KR - D_MAIN, ROPE_HALF),
    (TAIL_KR + ROPE_X2, REF_KR - D_MAIN + ROPE_HALF, ROPE_HALF),
    (TAIL_CQ, REF_CQ - D_MAIN, Q_LORA),
)
TAIL_ZERO_ROWS = ((TAIL_KR + ROPE_HALF, ROPE_X2 - ROPE_HALF),
                  (TAIL_KR + ROPE_X2 + ROPE_HALF, ROPE_X2 - ROPE_HALF),
                  (TAIL_KR + ROPE_TILE, TAIL_CQ - TAIL_KR - ROPE_TILE))


def _in_proj_tail_kernel(x_ref, *refs):
    w_refs = refs[:TAIL_PARTS]
    o_ref, wp_ref, acc_ref = refs[TAIL_PARTS:]
    kk = pl.program_id(1)

    @pl.when(kk == 0)
    def _():
        acc_ref[...] = jnp.zeros_like(acc_ref)

    for dst, src, rows in TAIL_COPIES:
        done = 0
        while done < rows:
            part, lo = divmod(src + done, TAIL_PART_ROWS)
            n = min(rows - done, TAIL_PART_ROWS - lo)
            wp_ref[dst + done:dst + done + n, :] = (
                w_refs[part][lo:lo + n, :].astype(BF16))
            done += n
    for dst, rows in TAIL_ZERO_ROWS:
        wp_ref[dst:dst + rows, :] = jnp.zeros((rows, wp_ref.shape[1]), BF16)

    acc_ref[...] += lax.dot_general(x_ref[...], wp_ref[...], _NT,
                                    preferred_element_type=F32)

    @pl.when(kk == pl.num_programs(1) - 1)
    def _():
        o_ref[...] = acc_ref[...].astype(o_ref.dtype)


def _in_proj_tail(x, wt, layer, tm=PROJ_TM, tk=TAIL_TK):
    m, k = x.shape
    assert m % tm == 0 and k % tk == 0 and D_MAIN % TAIL_PART_ROWS == 0
    assert wt.shape[1] == D_MAIN + TAIL_PARTS * TAIL_PART_ROWS, wt.shape
    first = D_MAIN // TAIL_PART_ROWS
    w_specs = [pl.BlockSpec((None, TAIL_PART_ROWS, tk),
                            lambda i, kk, p=p: (layer, first + p, kk))
               for p in range(TAIL_PARTS)]
    return pl.pallas_call(
        _in_proj_tail_kernel,
        grid=(m // tm, k // tk),
        in_specs=[pl.BlockSpec((tm, tk), lambda i, kk: (i, kk))] + w_specs,
        out_specs=pl.BlockSpec((tm, D_TAIL), lambda i, kk: (i, 0)),
        out_shape=jax.ShapeDtypeStruct((m, D_TAIL), BF16),
        scratch_shapes=[pltpu.VMEM((D_TAIL, tk), BF16),
                        pltpu.VMEM((tm, D_TAIL), F32)],
        compiler_params=_params(("parallel", "arbitrary")),
        name="in_proj_tail",
    )(x, *([wt] * TAIL_PARTS))


def _matmul_residual_kernel(x_ref, w_ref, r_ref, o_ref):
    o_ref[...] = r_ref[...] + jnp.dot(x_ref[...], w_ref[...].astype(BF16),
                                      preferred_element_type=F32)


def _matmul_residual(x, w, layer, res, tm=PROJ_TM, tn=PROJ_TN):
    m, k = x.shape
    n = w.shape[2]
    assert m % tm == 0 and n % tn == 0 and w.shape[1] == k, (m, k, n)
    return pl.pallas_call(
        _matmul_residual_kernel,
        grid=(m // tm, n // tn),
        in_specs=[pl.BlockSpec((tm, k), lambda i, j: (i, 0)),
                  pl.BlockSpec((None, k, tn), lambda i, j: (layer, 0, j)),
                  pl.BlockSpec((tm, tn), lambda i, j: (i, j))],
        out_specs=pl.BlockSpec((tm, tn), lambda i, j: (i, j)),
        out_shape=jax.ShapeDtypeStruct((m, n), F32),
        compiler_params=_params(("parallel", "parallel")),
        name="out_proj",
    )(x, w, res)


def _gmlp_kernel(u_ref, v_ref, z_ref, gv_ref, w_ref, bias_ref, go_ref, o_ref,
                 y_ref):
    tm = u_ref.shape[0]
    t_idx = lax.broadcasted_iota(jnp.int32, (CHUNK, CHUNK), 0)
    s_idx = lax.broadcasted_iota(jnp.int32, (CHUNK, CHUNK), 1)
    causal = s_idx <= t_idx

    def chunk(c, carry):
        rows = pl.ds(pl.multiple_of(c * CHUNK, CHUNK), CHUNK)
        ssq = jnp.zeros((CHUNK, 1), F32)
        for g in range(A_GROUPS):
            cols = slice(g * A_CH, (g + 1) * A_CH)
            vg = _gelu(v_ref[rows, cols].astype(F32))
            vn = vg * _rms_scale(vg, A_CH) * gv_ref[:, cols]
            w = jnp.where(causal, w_ref[g], 0.0).astype(BF16)
            sv = jnp.dot(w, vn.astype(BF16), preferred_element_type=F32)
            sv = sv + bias_ref[:, cols]
            y = _gelu(u_ref[rows, cols].astype(F32)) * sv
            ssq = ssq + jnp.sum(y * y, axis=-1, keepdims=True)
            y_ref[:, cols] = y
        r = lax.rsqrt(ssq * (1.0 / A_WIDTH) + EPS)
        z = z_ref[rows, :].astype(F32)
        o_ref[rows, :] = (y_ref[...] * r * go_ref[...] * _silu(z)).astype(o_ref.dtype)
        return carry

    lax.fori_loop(0, tm // CHUNK, chunk, 0)


def _gmlp(proj, g_v, w_s, b_s, g_out_a, tm=MIXER_TM):
    m = proj.shape[0]
    assert m % tm == 0 and tm % CHUNK == 0, (m, tm)
    bias = jnp.repeat(b_s.T, A_CH, axis=1)
    col = lambda off: (lambda i: (i, off // A_WIDTH))
    return pl.pallas_call(
        _gmlp_kernel,
        grid=(m // tm,),
        in_specs=[pl.BlockSpec((tm, A_WIDTH), col(OFF_U)),
                  pl.BlockSpec((tm, A_WIDTH), col(OFF_V)),
                  pl.BlockSpec((tm, A_WIDTH), col(OFF_ZA)),
                  pl.BlockSpec((1, A_WIDTH), lambda i: (0, 0)),
                  pl.BlockSpec((A_GROUPS, CHUNK, CHUNK), lambda i: (0, 0, 0)),
                  pl.BlockSpec((CHUNK, A_WIDTH), lambda i: (0, 0)),
                  pl.BlockSpec((1, A_WIDTH), lambda i: (0, 0))],
        out_specs=pl.BlockSpec((tm, A_WIDTH), lambda i: (i, 0)),
        out_shape=jax.ShapeDtypeStruct((m, A_WIDTH), BF16),
        scratch_shapes=[pltpu.VMEM((CHUNK, A_WIDTH), F32)],
        compiler_params=_params(("parallel",)),
        name="gmlp",
    )(proj, proj, proj, g_v.reshape(1, A_WIDTH), w_s, bias,
      g_out_a.reshape(1, A_WIDTH))


SB_T = 128
SB_SUB = 128
SB_NSUB = 3
SB_W = SB_NSUB * SB_SUB
SB_HG = 6
SB_STAGE = 6
SB_ZERO_WEIGHT_LOG = -105.0


def _sb_kernel(q_ref, k_ref, v_ref, o_ref, acc_ref, carry_ref):
    t, w, sub = SB_T, SB_W, SB_SUB
    i = pl.program_id(2)
    scale = HEAD_DIM ** -0.5
    q_pos = i * t + lax.broadcasted_iota(jnp.int32, (t, w), 0)
    k_off = lax.broadcasted_iota(jnp.int32, (t, w), 1)
    row = lax.broadcasted_iota(jnp.int32, (2 * sub, sub), 0) % sub
    col = lax.broadcasted_iota(jnp.int32, (2 * sub, sub), 1)
    neg_suffix = jnp.where(row > col, -1.0, 0.0).astype(BF16)

    acc_ref[...] = jnp.zeros_like(acc_ref)
    carry_ref[...] = jnp.zeros_like(carry_ref)

    def window(start, bound):
        valid = (start + k_off) < jnp.minimum(q_pos, bound)
        ks = pl.ds(pl.multiple_of(start, sub), w)
        worst = None
        for h0 in range(0, SB_HG, SB_STAGE):
            worst = stage(range(h0, h0 + SB_STAGE), valid, ks, worst)
        return worst

    def stage(heads, valid, ks, worst):
        log_beta, softplus, pieces = {}, {}, []
        for h in heads:
            hc = slice(h * HEAD_DIM, (h + 1) * HEAD_DIM)
            z = lax.dot_general(q_ref[0, :, hc], k_ref[0, ks, hc],
                                (((1,), (1,)), ((), ())),
                                preferred_element_type=F32) * scale
            sp = jnp.maximum(z, 0.0) + jnp.log(1.0 + jnp.exp(-jnp.abs(z)))
            log_beta[h] = z - sp
            sp = jnp.where(valid, sp, 0.0)
            softplus[h] = sp
            hi = sp.astype(BF16)
            lo = (sp - hi.astype(F32)).astype(BF16)
            for c in range(SB_NSUB):
                pieces.append(jnp.concatenate(
                    [hi[:, c * sub:(c + 1) * sub], lo[:, c * sub:(c + 1) * sub]],
                    axis=1))
        local = jnp.dot(jnp.concatenate(pieces, axis=0), neg_suffix,
                        preferred_element_type=F32)
        for n, h in enumerate(heads):
            hc = slice(h * HEAD_DIM, (h + 1) * HEAD_DIM)
            sp = softplus[h]
            running = carry_ref[h]
            after = [None] * SB_NSUB
            for c in reversed(range(SB_NSUB)):
                base = (n * SB_NSUB + c) * t
                after[c] = local[base:base + t] + running
                running = running - jnp.sum(sp[:, c * sub:(c + 1) * sub],
                                            axis=-1, keepdims=True)
            a = jnp.where(valid,
                          jnp.exp(log_beta[h] + jnp.concatenate(after, axis=1)),
                          0.0)
            acc_ref[h] += jnp.dot(a.astype(BF16), v_ref[0, ks, hc],
                                  preferred_element_type=F32)
            carry_ref[h] = running
            top = jnp.max(running)
            worst = top if worst is None else jnp.maximum(worst, top)
        return worst

    first = jnp.maximum(i + 1 - SB_NSUB, 0) * sub
    worst = window(first, jnp.int32(2 ** 30))

    def more(state):
        start, worst = state
        return jnp.logical_and(start > 0, worst > SB_ZERO_WEIGHT_LOG)

    def step(state):
        start, _ = state
        nxt = jnp.maximum(start - w, 0)
        return nxt, window(nxt, start)

    lax.while_loop(more, step, (first, worst))
    for h in range(SB_HG):
        o_ref[0, :, h * HEAD_DIM:(h + 1) * HEAD_DIM] = acc_ref[h].astype(o_ref.dtype)


def _stick_breaking(proj3):
    b, s, _ = proj3.shape
    assert s % SB_T == 0 and s >= SB_W and SB_T == SB_SUB and HEADS % SB_HG == 0
    t, gw = SB_T, SB_HG * HEAD_DIM
    blk = lambda off: off // gw
    return pl.pallas_call(
        _sb_kernel,
        grid=(b, HEADS // SB_HG, s // t),
        in_specs=[
            pl.BlockSpec((1, t, gw), lambda bi, g, i: (bi, i, blk(OFF_QB) + g)),
            pl.BlockSpec((1, s, gw), lambda bi, g, i: (bi, 0, blk(OFF_KB) + g)),
            pl.BlockSpec((1, s, gw), lambda bi, g, i: (bi, 0, blk(OFF_VB) + g)),
        ],
        out_specs=pl.BlockSpec((1, t, gw), lambda bi, g, i: (bi, i, g)),
        out_shape=jax.ShapeDtypeStruct((b, s, B_WIDTH), BF16),
        scratch_shapes=[pltpu.VMEM((SB_HG, t, HEAD_DIM), F32),
                        pltpu.VMEM((SB_HG, t, 1), F32)],
        compiler_params=_params(("parallel", "parallel", "arbitrary")),
        name="stick_breaking",
    )(proj3, proj3, proj3)


def _rope_tile(x, cos_ref, sin_ref):
    return x * cos_ref[...] + pltpu.roll(x, ROPE_X2, axis=1) * sin_ref[...]


def _mla_q_kernel(cq_ref, g_ref, w_ref, cos_ref, sin_ref, o_ref):
    scale = (C_NOPE + C_ROPE) ** -0.5 * math.log2(math.e)
    x = cq_ref[...].astype(F32)
    xn = (x * _rms_scale(x, Q_LORA) * g_ref[...]).astype(BF16)
    q = jnp.dot(xn, w_ref[...], preferred_element_type=F32)
    for h in range(HEADS):
        lo = h * C_QK
        o_ref[:, lo:lo + C_NOPE] = (q[:, lo:lo + C_NOPE] * scale).astype(o_ref.dtype)
        rope = _rope_tile(q[:, lo + C_NOPE:lo + C_QK], cos_ref, sin_ref)
        o_ref[:, lo + C_NOPE:lo + C_QK] = (rope * scale).astype(o_ref.dtype)


VT_ONES = 16
VT_ROWS = HEAD_DIM + VT_ONES


def _mla_kv_kernel(ckv_ref, kr_ref, g_ref, wk_ref, wvt_ref, cos_ref, sin_ref,
                   k_ref, vt_ref):
    tm = ckv_ref.shape[0]
    x = ckv_ref[...].astype(F32)
    xn = (x * _rms_scale(x, KV_LORA) * g_ref[...]).astype(BF16)
    kn = jnp.dot(xn, wk_ref[...], preferred_element_type=F32).astype(k_ref.dtype)
    kr = _rope_tile(kr_ref[...].astype(F32), cos_ref, sin_ref).astype(k_ref.dtype)
    for h in range(HEADS):
        k_ref[:, h * C_QK:h * C_QK + C_NOPE] = kn[:, h * C_NOPE:(h + 1) * C_NOPE]
        k_ref[:, h * C_QK + C_NOPE:(h + 1) * C_QK] = kr
    vt = lax.dot_general(wvt_ref[...], xn, _NT, preferred_element_type=F32)
    for h in range(HEADS):
        vt_ref[h * VT_ROWS:h * VT_ROWS + HEAD_DIM, :] = (
            vt[h * HEAD_DIM:(h + 1) * HEAD_DIM, :].astype(vt_ref.dtype))
        vt_ref[h * VT_ROWS + HEAD_DIM:(h + 1) * VT_ROWS, :] = jnp.ones(
            (VT_ONES, tm), vt_ref.dtype)


def _mla_project(proj, g_q, g_kv, w_uq_p, w_k_p, w_vt_p, tabs, b, s,
                 tm=MIXER_TM):
    m = proj.shape[0]
    assert m == b * s and s % tm == 0, (m, b, s, tm)
    tiles = s // tm
    tab_spec = pl.BlockSpec((tm, ROPE_TILE), lambda i: (i, 0))
    q_cat = pl.pallas_call(
        _mla_q_kernel,
        grid=(m // tm,),
        in_specs=[pl.BlockSpec((tm, Q_LORA), lambda i: (i, TAIL_CQ // Q_LORA)),
                  pl.BlockSpec((1, Q_LORA), lambda i: (0, 0)),
                  pl.BlockSpec((Q_LORA, HEADS * C_QK), lambda i: (0, 0)),
                  tab_spec, tab_spec],
        out_specs=pl.BlockSpec((tm, HEADS * C_QK), lambda i: (i, 0)),
        out_shape=jax.ShapeDtypeStruct((m, HEADS * C_QK), BF16),
        compiler_params=_params(("parallel",)),
        name="mla_q",
    )(proj, g_q.reshape(1, Q_LORA), w_uq_p, *tabs)
    k_cat, vt = pl.pallas_call(
        _mla_kv_kernel,
        grid=(m // tm,),
        in_specs=[pl.BlockSpec((tm, KV_LORA), lambda i: (i, TAIL_CKV // KV_LORA)),
                  pl.BlockSpec((tm, ROPE_TILE), lambda i: (i, TAIL_KR // ROPE_TILE)),
                  pl.BlockSpec((1, KV_LORA), lambda i: (0, 0)),
                  pl.BlockSpec((KV_LORA, HEADS * C_NOPE), lambda i: (0, 0)),
                  pl.BlockSpec((C_WIDTH, KV_LORA), lambda i: (0, 0)),
                  tab_spec, tab_spec],
        out_specs=[pl.BlockSpec((tm, HEADS * C_QK), lambda i: (i, 0)),
                   pl.BlockSpec((None, HEADS * VT_ROWS, tm),
                                lambda i: (i // tiles, 0, i % tiles))],
        out_shape=[jax.ShapeDtypeStruct((m, HEADS * C_QK), BF16),
                   jax.ShapeDtypeStruct((b, HEADS * VT_ROWS, s), BF16)],
        compiler_params=_params(("parallel",)),
        name="mla_kv",
    )(proj, proj, g_kv.reshape(1, KV_LORA), w_k_p, w_vt_p, *tabs)
    return q_cat, k_cat, vt


MLA_T = 256
MLA_W = 512
MLA_HG = 6


def _mla_attn_kernel(q_ref, k_ref, vt_ref, o_ref, m_ref, acc_ref, s_ref,
                     smax_ref):
    t, w = MLA_T, MLA_W
    i = pl.program_id(2)
    k_off = lax.broadcasted_iota(jnp.int32, (w, t), 0)
    q_pos = i * t + lax.broadcasted_iota(jnp.int32, (w, t), 1)

    m_ref[...] = jnp.full_like(m_ref, -jnp.inf)
    acc_ref[...] = jnp.zeros_like(acc_ref)

    def scores(j, slot):
        ks = pl.ds(pl.multiple_of(j * w, w), w)
        for h in range(MLA_HG):
            s = lax.dot_general(k_ref[0, ks, h * C_QK:(h + 1) * C_QK],
                                q_ref[0, :, h * C_QK:(h + 1) * C_QK], _NT,
                                preferred_element_type=F32)
            s_ref[slot, h] = s
            smax_ref[slot, h] = jnp.max(s, axis=0, keepdims=True)

    def softmax_pv(j, slot, masked):
        start = pl.multiple_of(j * w, w)
        ks = pl.ds(start, w)
        for h in range(MLA_HG):
            s = s_ref[slot, h]
            if masked:
                s = jnp.where((start + k_off) <= q_pos, s, -jnp.inf)
                s_max = jnp.max(s, axis=0, keepdims=True)
            else:
                s_max = smax_ref[slot, h]
            m_old = m_ref[h]
            m_new = jnp.maximum(m_old, s_max)
            alpha = jnp.exp2(m_old - m_new)
            p = jnp.exp2(s - m_new)
            acc_ref[h] = alpha * acc_ref[h] + jnp.dot(
                vt_ref[0, h * VT_ROWS:(h + 1) * VT_ROWS, ks], p.astype(BF16),
                preferred_element_type=F32)
            m_ref[h] = m_new

    last = (i * t) // w

    scores(0, 0)

    def pair(jj, carry):
        j = 2 * jj
        scores(j + 1, 1)
        softmax_pv(j, 0, False)
        scores(j + 2, 0)
        softmax_pv(j + 1, 1, False)
        return carry

    lax.fori_loop(0, last // 2, pair, 0)

    @pl.when(last % 2 == 0)
    def _():
        softmax_pv(last, 0, True)

    @pl.when(last % 2 == 1)
    def _():
        scores(last, 1)
        softmax_pv(last - 1, 0, False)
        softmax_pv(last, 1, True)
    for h in range(MLA_HG):
        acc = acc_ref[h]
        out_t = acc[:HEAD_DIM] / acc[HEAD_DIM:HEAD_DIM + 1]
        o_ref[0, :, h * HEAD_DIM:(h + 1) * HEAD_DIM] = out_t.T.astype(o_ref.dtype)


def _mla_attention(q_cat, k_cat, vt, b, s):
    t, hg = MLA_T, MLA_HG
    assert MLA_W % t == 0 and s % MLA_W == 0 and HEADS % hg == 0
    q3 = q_cat.reshape(b, s, HEADS * C_QK)
    k3 = k_cat.reshape(b, s, HEADS * C_QK)
    return pl.pallas_call(
        _mla_attn_kernel,
        grid=(b, HEADS // hg, s // t),
        in_specs=[pl.BlockSpec((1, t, hg * C_QK), lambda bi, g, i: (bi, i, g)),
                  pl.BlockSpec((1, s, hg * C_QK), lambda bi, g, i: (bi, 0, g)),
                  pl.BlockSpec((1, hg * VT_ROWS, s), lambda bi, g, i: (bi, g, 0))],
        out_specs=pl.BlockSpec((1, t, hg * HEAD_DIM), lambda bi, g, i: (bi, i, g)),
        out_shape=jax.ShapeDtypeStruct((b, s, C_WIDTH), BF16),
        scratch_shapes=[pltpu.VMEM((hg, 1, t), F32),
                        pltpu.VMEM((hg, VT_ROWS, t), F32),
                        pltpu.VMEM((2, hg, MLA_W, t), F32),
                        pltpu.VMEM((2, hg, 1, t), F32)],
        compiler_params=_params(("parallel", "parallel", "arbitrary")),
        name="mla_attention",
    )(q3, k3, vt)


GATE_ROWS = 32


def _gate_kernel(ya_ref, yb_ref, zb_ref, yc_ref, zc_ref, g_ref, o_ref):
    o_ref[:, :A_WIDTH] = ya_ref[...]

    def chunk(c, carry):
        rows = pl.ds(pl.multiple_of(c * GATE_ROWS, GATE_ROWS), GATE_ROWS)
        lo = A_WIDTH
        for y_ref, z_ref, width in ((yb_ref, zb_ref, B_WIDTH),
                                    (yc_ref, zc_ref, C_WIDTH)):
            y = y_ref[rows, :].astype(F32)
            z = z_ref[rows, :].astype(F32)
            g = g_ref[:, lo:lo + width]
            o_ref[rows, lo:lo + width] = (y * _rms_scale(y, width) * g
                                          * _silu(z)).astype(o_ref.dtype)
            lo += width
        return carry

    lax.fori_loop(0, ya_ref.shape[0] // GATE_ROWS, chunk, 0)


def _gate(ya, yb, yc, main, tail, g_out, tm=MIXER_TM):
    m = ya.shape[0]
    assert m % tm == 0 and tm % GATE_ROWS == 0, (m, tm)
    return pl.pallas_call(
        _gate_kernel,
        grid=(m // tm,),
        in_specs=[pl.BlockSpec((tm, A_WIDTH), lambda i: (i, 0)),
                  pl.BlockSpec((tm, B_WIDTH), lambda i: (i, 0)),
                  pl.BlockSpec((tm, B_WIDTH), lambda i: (i, OFF_ZB // B_WIDTH)),
                  pl.BlockSpec((tm, C_WIDTH), lambda i: (i, 0)),
                  pl.BlockSpec((tm, C_WIDTH), lambda i: (i, TAIL_ZC // C_WIDTH)),
                  pl.BlockSpec((1, D_MODEL), lambda i: (0, 0))],
        out_specs=pl.BlockSpec((tm, D_MODEL), lambda i: (i, 0)),
        out_shape=jax.ShapeDtypeStruct((m, D_MODEL), BF16),
        compiler_params=_params(("parallel",)),
        name="gate",
    )(ya, yb, main, yc, tail, g_out.reshape(1, D_MODEL))


def _prep_w_uq(w):
    w = w.reshape(Q_LORA, HEADS, C_NOPE + C_ROPE)
    gap = jnp.zeros((Q_LORA, HEADS, ROPE_X2 - ROPE_HALF), w.dtype)
    w = jnp.concatenate([w[:, :, :C_NOPE + ROPE_HALF], gap,
                         w[:, :, C_NOPE + ROPE_HALF:], gap], axis=2)
    return w.reshape(Q_LORA, HEADS * C_QK).astype(BF16)


def _prep_w_ukv(w):
    w = w.reshape(KV_LORA, HEADS, C_NOPE + HEAD_DIM)
    wk = w[:, :, :C_NOPE].reshape(KV_LORA, HEADS * C_NOPE)
    wv_t = w[:, :, C_NOPE:].reshape(KV_LORA, C_WIDTH).T
    return wk.astype(BF16), wv_t.astype(BF16)


def _rope_tables(positions):
    inv_freq = 1.0 / (ROPE_THETA ** (jnp.arange(0, C_ROPE, 2, dtype=F32) / C_ROPE))
    ang = positions.astype(F32).reshape(-1, 1) * inv_freq
    cos, sin = jnp.cos(ang), jnp.sin(ang)
    gap = jnp.zeros((ang.shape[0], ROPE_X2 - ROPE_HALF), F32)
    return (jnp.concatenate([cos, gap, cos, gap], axis=1),
            jnp.concatenate([-sin, gap, sin, gap], axis=1))


def kernel(x, positions, g_pre, w_in, a_g_v, a_w_s, a_b_s, c_g_q, c_g_kv,
           c_w_uq, c_w_ukv, g_out, w_out, g_final):
    b, s, d = x.shape
    m = b * s
    depth = w_in.shape[0]
    tabs = _rope_tables(positions)
    w_in_t = jnp.swapaxes(w_in, 1, 2)
    h = x.reshape(m, d)
    for l in range(depth):
        hn = _rmsnorm(h, g_pre[l], BF16)
        main = _in_proj_main(hn, w_in_t, l)
        tail = _in_proj_tail(hn, w_in_t, l)
        ya = _gmlp(main, a_g_v[l], a_w_s[l], a_b_s[l], g_out[l, :A_WIDTH])
        yb = _stick_breaking(main.reshape(b, s, D_MAIN)).reshape(m, B_WIDTH)
        w_k_p, w_vt_p = _prep_w_ukv(c_w_ukv[l])
        q_cat, k_cat, vt = _mla_project(tail, c_g_q[l], c_g_kv[l],
                                        _prep_w_uq(c_w_uq[l]), w_k_p, w_vt_p,
                                        tabs, b, s)
        yc = _mla_attention(q_cat, k_cat, vt, b, s).reshape(m, C_WIDTH)
        y = _gate(ya, yb, yc, main, tail, g_out[l])
        h = _matmul_residual(y, w_out, l, h)
    return _rmsnorm(h, g_final, F32).reshape(b, s, d)
```

```python
import math

import jax
import jax.numpy as jnp
from jax import lax
from jax.experimental import pallas as pl
from jax.experimental.pallas import tpu as pltpu

F32 = jnp.float32
BF16 = jnp.bfloat16

D_MODEL = 4096
A_GROUPS = 8
A_CH = 128
A_WIDTH = A_GROUPS * A_CH
CHUNK = 128
HEADS = 12
HEAD_DIM = 128
B_WIDTH = HEADS * HEAD_DIM
C_NOPE = 128
C_ROPE = 64
C_WIDTH = HEADS * HEAD_DIM
Q_LORA = 768
KV_LORA = 512
ROPE_THETA = 10000.0
EPS = 1e-6

C_QK = 256
ROPE_HALF = C_ROPE // 2
ROPE_TILE = 128
ROPE_X2 = ROPE_TILE // 2

OFF_U = 0
OFF_V = OFF_U + A_WIDTH
OFF_ZA = OFF_V + A_WIDTH
OFF_QB = OFF_ZA + A_WIDTH
OFF_KB = OFF_QB + B_WIDTH
OFF_VB = OFF_KB + B_WIDTH
OFF_ZB = OFF_VB + B_WIDTH
D_MAIN = OFF_ZB + B_WIDTH
REF_CQ = D_MAIN
REF_CKV = REF_CQ + Q_LORA
REF_KR = REF_CKV + KV_LORA
REF_ZC = REF_KR + C_ROPE
TAIL_ZC = 0
TAIL_CKV = TAIL_ZC + C_WIDTH
TAIL_KR = TAIL_CKV + KV_LORA
TAIL_CQ = TAIL_KR + C_QK
D_TAIL = TAIL_CQ + Q_LORA

V7X_VMEM_BYTES = 64 * 1024 * 1024
VMEM_LIMIT_BYTES = V7X_VMEM_BYTES * 7 // 8

NORM_TM = 512
PROJ_TM = 1024
PROJ_TN = 512
MAIN_TN = 768
TAIL_TK = 512
MIXER_TM = 512


def _params(sem):
    return pltpu.CompilerParams(dimension_semantics=sem,
                                vmem_limit_bytes=VMEM_LIMIT_BYTES)


def _gelu(x):
    c = math.sqrt(2.0 / math.pi)
    return x * (0.5 * (1.0 + jnp.tanh(c * (x + 0.044715 * (x * x * x)))))


def _silu(z):
    return z / (1.0 + jnp.exp(-z))


def _rms_scale(x, width):
    return lax.rsqrt(jnp.sum(x * x, axis=-1, keepdims=True) * (1.0 / width) + EPS)


def _rmsnorm_kernel(x_ref, g_ref, o_ref):
    x = x_ref[...]
    o_ref[...] = (x * _rms_scale(x, x.shape[-1]) * g_ref[...]).astype(o_ref.dtype)


def _rmsnorm(x, g, out_dtype, tm=NORM_TM):
    m, d = x.shape
    assert m % tm == 0, (m, tm)
    return pl.pallas_call(
        _rmsnorm_kernel,
        grid=(m // tm,),
        in_specs=[pl.BlockSpec((tm, d), lambda i: (i, 0)),
                  pl.BlockSpec((1, d), lambda i: (0, 0))],
        out_specs=pl.BlockSpec((tm, d), lambda i: (i, 0)),
        out_shape=jax.ShapeDtypeStruct((m, d), out_dtype),
        compiler_params=_params(("parallel",)),
        name="rmsnorm",
    )(x, g.reshape(1, d))


_NT = (((1,), (1,)), ((), ()))


def _in_proj_main_kernel(x_ref, wt_ref, o_ref):
    o_ref[...] = lax.dot_general(x_ref[...], wt_ref[...].astype(BF16), _NT,
                                 preferred_element_type=F32).astype(o_ref.dtype)


def _in_proj_main(x, wt, layer, tm=PROJ_TM, tn=MAIN_TN):
    m, k = x.shape
    assert m % tm == 0 and D_MAIN % tn == 0 and wt.shape[2] == k, (m, tm, tn)
    return pl.pallas_call(
        _in_proj_main_kernel,
        grid=(m // tm, D_MAIN // tn),
        in_specs=[pl.BlockSpec((tm, k), lambda i, j: (i, 0)),
                  pl.BlockSpec((None, tn, k), lambda i, j: (layer, j, 0))],
        out_specs=pl.BlockSpec((tm, tn), lambda i, j: (i, j)),
        out_shape=jax.ShapeDtypeStruct((m, D_MAIN), BF16),
        compiler_params=_params(("parallel", "parallel")),
        name="in_proj_main",
    )(x, wt)


TAIL_PART_ROWS = 576
TAIL_PARTS = 5
TAIL_COPIES = (
    (TAIL_ZC, REF_ZC - D_MAIN, C_WIDTH),
    (TAIL_CKV, REF_CKV - D_MAIN, KV_LORA),
    (TAIL_KR, REF_KR - D_MAIN, ROPE_HALF),
    (TAIL_KR + ROPE_X2, REF_KR - D_MAIN + ROPE_HALF, ROPE_HALF),
    (TAIL_CQ, REF_CQ - D_MAIN, Q_LORA),
)
TAIL_ZERO_ROWS = ((TAIL_KR + ROPE_HALF, ROPE_X2 - ROPE_HALF),
                  (TAIL_KR + ROPE_X2 + ROPE_HALF, ROPE_X2 - ROPE_HALF),
                  (TAIL_KR + ROPE_TILE, TAIL_CQ - TAIL_KR - ROPE_TILE))


def _in_proj_tail_kernel(x_ref, *refs):
    w_refs = refs[:TAIL_PARTS]
    o_ref, wp_ref, acc_ref = refs[TAIL_PARTS:]
    kk = pl.program_id(1)

    @pl.when(kk == 0)
    def _():
        acc_ref[...] = jnp.zeros_like(acc_ref)

    for dst, src, rows in TAIL_COPIES:
        done = 0
        while done < rows:
            part, lo = divmod(src + done, TAIL_PART_ROWS)
            n = min(rows - done, TAIL_PART_ROWS - lo)
            wp_ref[dst + done:dst + done + n, :] = (
                w_refs[part][lo:lo + n, :].astype(BF16))
            done += n
    for dst, rows in TAIL_ZERO_ROWS:
        wp_ref[dst:dst + rows, :] = jnp.zeros((rows, wp_ref.shape[1]), BF16)

    acc_ref[...] += lax.dot_general(x_ref[...], wp_ref[...], _NT,
                                    preferred_element_type=F32)

    @pl.when(kk == pl.num_programs(1) - 1)
    def _():
        o_ref[...] = acc_ref[...].astype(o_ref.dtype)


def _in_proj_tail(x, wt, layer, tm=PROJ_TM, tk=TAIL_TK):
    m, k = x.shape
    assert m % tm == 0 and k % tk == 0 and D_MAIN % TAIL_PART_ROWS == 0
    assert wt.shape[1] == D_MAIN + TAIL_PARTS * TAIL_PART_ROWS, wt.shape
    first = D_MAIN // TAIL_PART_ROWS
    w_specs = [pl.BlockSpec((None, TAIL_PART_ROWS, tk),
                            lambda i, kk, p=p: (layer, first + p, kk))
               for p in range(TAIL_PARTS)]
    return pl.pallas_call(
        _in_proj_tail_kernel,
        grid=(m // tm, k // tk),
        in_specs=[pl.BlockSpec((tm, tk), lambda i, kk: (i, kk))] + w_specs,
        out_specs=pl.BlockSpec((tm, D_TAIL), lambda i, kk: (i, 0)),
        out_shape=jax.ShapeDtypeStruct((m, D_TAIL), BF16),
        scratch_shapes=[pltpu.VMEM((D_TAIL, tk), BF16),
                        pltpu.VMEM((tm, D_TAIL), F32)],
        compiler_params=_params(("parallel", "arbitrary")),
        name="in_proj_tail",
    )(x, *([wt] * TAIL_PARTS))


def _matmul_residual_kernel(x_ref, w_ref, r_ref, o_ref):
    o_ref[...] = r_ref[...] + jnp.dot(x_ref[...], w_ref[...].astype(BF16),
                                      preferred_element_type=F32)


def _matmul_residual(x, w, layer, res, tm=PROJ_TM, tn=PROJ_TN):
    m, k = x.shape
    n = w.shape[2]
    assert m % tm == 0 and n % tn == 0 and w.shape[1] == k, (m, k, n)
    return pl.pallas_call(
        _matmul_residual_kernel,
        grid=(m // tm, n // tn),
        in_specs=[pl.BlockSpec((tm, k), lambda i, j: (i, 0)),
                  pl.BlockSpec((None, k, tn), lambda i, j: (layer, 0, j)),
                  pl.BlockSpec((tm, tn), lambda i, j: (i, j))],
        out_specs=pl.BlockSpec((tm, tn), lambda i, j: (i, j)),
        out_shape=jax.ShapeDtypeStruct((m, n), F32),
        compiler_params=_params(("parallel", "parallel")),
        name="out_proj",
    )(x, w, res)


def _gmlp_kernel(u_ref, v_ref, z_ref, gv_ref, w_ref, bias_ref, go_ref, o_ref,
                 y_ref):
    tm = u_ref.shape[0]
    t_idx = lax.broadcasted_iota(jnp.int32, (CHUNK, CHUNK), 0)
    s_idx = lax.broadcasted_iota(jnp.int32, (CHUNK, CHUNK), 1)
    causal = s_idx <= t_idx

    def chunk(c, carry):
        rows = pl.ds(pl.multiple_of(c * CHUNK, CHUNK), CHUNK)
        ssq = jnp.zeros((CHUNK, 1), F32)
        for g in range(A_GROUPS):
            cols = slice(g * A_CH, (g + 1) * A_CH)
            vg = _gelu(v_ref[rows, cols].astype(F32))
            vn = vg * _rms_scale(vg, A_CH) * gv_ref[:, cols]
            w = jnp.where(causal, w_ref[g], 0.0).astype(BF16)
            sv = jnp.dot(w, vn.astype(BF16), preferred_element_type=F32)
            sv = sv + bias_ref[:, cols]
            y = _gelu(u_ref[rows, cols].astype(F32)) * sv
            ssq = ssq + jnp.sum(y * y, axis=-1, keepdims=True)
            y_ref[:, cols] = y
        r = lax.rsqrt(ssq * (1.0 / A_WIDTH) + EPS)
        z = z_ref[rows, :].astype(F32)
        o_ref[rows, :] = (y_ref[...] * r * go_ref[...] * _silu(z)).astype(o_ref.dtype)
        return carry

    lax.fori_loop(0, tm // CHUNK, chunk, 0)


def _gmlp(proj, g_v, w_s, b_s, g_out_a, tm=MIXER_TM):
    m = proj.shape[0]
    assert m % tm == 0 and tm % CHUNK == 0, (m, tm)
    bias = jnp.repeat(b_s.T, A_CH, axis=1)
    col = lambda off: (lambda i: (i, off // A_WIDTH))
    return pl.pallas_call(
        _gmlp_kernel,
        grid=(m // tm,),
        in_specs=[pl.BlockSpec((tm, A_WIDTH), col(OFF_U)),
                  pl.BlockSpec((tm, A_WIDTH), col(OFF_V)),
                  pl.BlockSpec((tm, A_WIDTH), col(OFF_ZA)),
                  pl.BlockSpec((1, A_WIDTH), lambda i: (0, 0)),
                  pl.BlockSpec((A_GROUPS, CHUNK, CHUNK), lambda i: (0, 0, 0)),
                  pl.BlockSpec((CHUNK, A_WIDTH), lambda i: (0, 0)),
                  pl.BlockSpec((1, A_WIDTH), lambda i: (0, 0))],
        out_specs=pl.BlockSpec((tm, A_WIDTH), lambda i: (i, 0)),
        out_shape=jax.ShapeDtypeStruct((m, A_WIDTH), BF16),
        scratch_shapes=[pltpu.VMEM((CHUNK, A_WIDTH), F32)],
        compiler_params=_params(("parallel",)),
        name="gmlp",
    )(proj, proj, proj, g_v.reshape(1, A_WIDTH), w_s, bias,
      g_out_a.reshape(1, A_WIDTH))


SB_T = 128
SB_SUB = 128
SB_NSUB = 3
SB_W = SB_NSUB * SB_SUB
SB_HG = 6
SB_STAGE = 6
SB_ZERO_WEIGHT_LOG = -105.0


def _sb_kernel(q_ref, k_ref, v_ref, o_ref, acc_ref, carry_ref):
    t, w, sub = SB_T, SB_W, SB_SUB
    i = pl.program_id(2)
    scale = HEAD_DIM ** -0.5
    q_pos = i * t + lax.broadcasted_iota(jnp.int32, (t, w), 0)
    k_off = lax.broadcasted_iota(jnp.int32, (t, w), 1)
    row = lax.broadcasted_iota(jnp.int32, (2 * sub, sub), 0) % sub
    col = lax.broadcasted_iota(jnp.int32, (2 * sub, sub), 1)
    neg_suffix = jnp.where(row > col, -1.0, 0.0).astype(BF16)

    acc_ref[...] = jnp.zeros_like(acc_ref)
    carry_ref[...] = jnp.zeros_like(carry_ref)

    def window(start, bound):
        valid = (start + k_off) < jnp.minimum(q_pos, bound)
        return stages(lambda x: jnp.where(valid, x, 0.0), start)

    def diagonal_window(start):
        tri = (lax.broadcasted_iota(jnp.int32, (t, sub), 1)
               < lax.broadcasted_iota(jnp.int32, (t, sub), 0))

        def mask(x):
            return jnp.concatenate(
                [x[:, :w - sub], jnp.where(tri, x[:, w - sub:], 0.0)], axis=1)

        return stages(mask, start)

    def stages(mask, start):
        ks = pl.ds(pl.multiple_of(start, sub), w)
        worst = None
        for h0 in range(0, SB_HG, SB_STAGE):
            worst = stage(range(h0, h0 + SB_STAGE), mask, ks, worst)
        return worst

    def stage(heads, mask, ks, worst):
        log_beta, softplus, pieces = {}, {}, []
        for h in heads:
            hc = slice(h * HEAD_DIM, (h + 1) * HEAD_DIM)
            z = lax.dot_general(q_ref[0, :, hc], k_ref[0, ks, hc],
                                (((1,), (1,)), ((), ())),
                                preferred_element_type=F32) * scale
            sp = jnp.maximum(z, 0.0) + jnp.log(1.0 + jnp.exp(-jnp.abs(z)))
            log_beta[h] = z - sp
            sp = mask(sp)
            softplus[h] = sp
            hi = sp.astype(BF16)
            lo = (sp - hi.astype(F32)).astype(BF16)
            for c in range(SB_NSUB):
                pieces.append(jnp.concatenate(
                    [hi[:, c * sub:(c + 1) * sub], lo[:, c * sub:(c + 1) * sub]],
                    axis=1))
        local = jnp.dot(jnp.concatenate(pieces, axis=0), neg_suffix,
                        preferred_element_type=F32)
        for n, h in enumerate(heads):
            hc = slice(h * HEAD_DIM, (h + 1) * HEAD_DIM)
            sp = softplus[h]
            running = carry_ref[h]
            after = [None] * SB_NSUB
            for c in reversed(range(SB_NSUB)):
                base = (n * SB_NSUB + c) * t
                after[c] = local[base:base + t] + running
                running = running - jnp.sum(sp[:, c * sub:(c + 1) * sub],
                                            axis=-1, keepdims=True)
            a = mask(jnp.exp(log_beta[h] + jnp.concatenate(after, axis=1)))
            acc_ref[h] += jnp.dot(a.astype(BF16), v_ref[0, ks, hc],
                                  preferred_element_type=F32)
            carry_ref[h] = running
            top = jnp.max(running)
            worst = top if worst is None else jnp.maximum(worst, top)
        return worst

    first = jnp.maximum(i + 1 - SB_NSUB, 0) * sub
    worst = lax.cond(i >= SB_NSUB - 1,
                     lambda: diagonal_window(first),
                     lambda: window(first, jnp.int32(2 ** 30)))

    def more(state):
        start, worst = state
        return jnp.logical_and(start > 0, worst > SB_ZERO_WEIGHT_LOG)

    def step(state):
        start, _ = state
        nxt = jnp.maximum(start - w, 0)
        return nxt, window(nxt, start)

    lax.while_loop(more, step, (first, worst))
    for h in range(SB_HG):
        o_ref[0, :, h * HEAD_DIM:(h + 1) * HEAD_DIM] = acc_ref[h].astype(o_ref.dtype)


def _stick_breaking(proj3):
    b, s, _ = proj3.shape
    assert s % SB_T == 0 and s >= SB_W and SB_T == SB_SUB and HEADS % SB_HG == 0
    t, gw = SB_T, SB_HG * HEAD_DIM
    blk = lambda off: off // gw
    return pl.pallas_call(
        _sb_kernel,
        grid=(b, HEADS // SB_HG, s // t),
        in_specs=[
            pl.BlockSpec((1, t, gw), lambda bi, g, i: (bi, i, blk(OFF_QB) + g)),
            pl.BlockSpec((1, s, gw), lambda bi, g, i: (bi, 0, blk(OFF_KB) + g)),
            pl.BlockSpec((1, s, gw), lambda bi, g, i: (bi, 0, blk(OFF_VB) + g)),
        ],
        out_specs=pl.BlockSpec((1, t, gw), lambda bi, g, i: (bi, i, g)),
        out_shape=jax.ShapeDtypeStruct((b, s, B_WIDTH), BF16),
        scratch_shapes=[pltpu.VMEM((SB_HG, t, HEAD_DIM), F32),
                        pltpu.VMEM((SB_HG, t, 1), F32)],
        compiler_params=_params(("parallel", "parallel", "arbitrary")),
        name="stick_breaking",
    )(proj3, proj3, proj3)


def _rope_tile(x, cos_ref, sin_ref):
    return x * cos_ref[...] + pltpu.roll(x, ROPE_X2, axis=1) * sin_ref[...]


def _mla_q_kernel(cq_ref, g_ref, w_ref, cos_ref, sin_ref, o_ref):
    scale = (C_NOPE + C_ROPE) ** -0.5 * math.log2(math.e)
    x = cq_ref[...].astype(F32)
    xn = (x * _rms_scale(x, Q_LORA) * g_ref[...]).astype(BF16)
    q = jnp.dot(xn, w_ref[...], preferred_element_type=F32)
    for h in range(HEADS):
        lo = h * C_QK
        o_ref[:, lo:lo + C_NOPE] = (q[:, lo:lo + C_NOPE] * scale).astype(o_ref.dtype)
        rope = _rope_tile(q[:, lo + C_NOPE:lo + C_QK], cos_ref, sin_ref)
        o_ref[:, lo + C_NOPE:lo + C_QK] = (rope * scale).astype(o_ref.dtype)


VT_ONES = 16
VT_ROWS = HEAD_DIM + VT_ONES


def _mla_kv_kernel(ckv_ref, kr_ref, g_ref, wk_ref, wvt_ref, cos_ref, sin_ref,
                   k_ref, vt_ref):
    tm = ckv_ref.shape[0]
    x = ckv_ref[...].astype(F32)
    xn = (x * _rms_scale(x, KV_LORA) * g_ref[...]).astype(BF16)
    kn = jnp.dot(xn, wk_ref[...], preferred_element_type=F32).astype(k_ref.dtype)
    kr = _rope_tile(kr_ref[...].astype(F32), cos_ref, sin_ref).astype(k_ref.dtype)
    for h in range(HEADS):
        k_ref[:, h * C_QK:h * C_QK + C_NOPE] = kn[:, h * C_NOPE:(h + 1) * C_NOPE]
        k_ref[:, h * C_QK + C_NOPE:(h + 1) * C_QK] = kr
    vt = lax.dot_general(wvt_ref[...], xn, _NT, preferred_element_type=F32)
    for h in range(HEADS):
        vt_ref[h * VT_ROWS:h * VT_ROWS + HEAD_DIM, :] = (
            vt[h * HEAD_DIM:(h + 1) * HEAD_DIM, :].astype(vt_ref.dtype))
        vt_ref[h * VT_ROWS + HEAD_DIM:(h + 1) * VT_ROWS, :] = jnp.ones(
            (VT_ONES, tm), vt_ref.dtype)


def _mla_project(proj, g_q, g_kv, w_uq_p, w_k_p, w_vt_p, tabs, b, s,
                 tm=MIXER_TM):
    m = proj.shape[0]
    assert m == b * s and s % tm == 0, (m, b, s, tm)
    tiles = s // tm
    tab_spec = pl.BlockSpec((tm, ROPE_TILE), lambda i: (i, 0))
    q_cat = pl.pallas_call(
        _mla_q_kernel,
        grid=(m // tm,),
        in_specs=[pl.BlockSpec((tm, Q_LORA), lambda i: (i, TAIL_CQ // Q_LORA)),
                  pl.BlockSpec((1, Q_LORA), lambda i: (0, 0)),
                  pl.BlockSpec((Q_LORA, HEADS * C_QK), lambda i: (0, 0)),
                  tab_spec, tab_spec],
        out_specs=pl.BlockSpec((tm, HEADS * C_QK), lambda i: (i, 0)),
        out_shape=jax.ShapeDtypeStruct((m, HEADS * C_QK), BF16),
        compiler_params=_params(("parallel",)),
        name="mla_q",
    )(proj, g_q.reshape(1, Q_LORA), w_uq_p, *tabs)
    k_cat, vt = pl.pallas_call(
        _mla_kv_kernel,
        grid=(m // tm,),
        in_specs=[pl.BlockSpec((tm, KV_LORA), lambda i: (i, TAIL_CKV // KV_LORA)),
                  pl.BlockSpec((tm, ROPE_TILE), lambda i: (i, TAIL_KR // ROPE_TILE)),
                  pl.BlockSpec((1, KV_LORA), lambda i: (0, 0)),
                  pl.BlockSpec((KV_LORA, HEADS * C_NOPE), lambda i: (0, 0)),
                  pl.BlockSpec((C_WIDTH, KV_LORA), lambda i: (0, 0)),
                  tab_spec, tab_spec],
        out_specs=[pl.BlockSpec((tm, HEADS * C_QK), lambda i: (i, 0)),
                   pl.BlockSpec((None, HEADS * VT_ROWS, tm),
                                lambda i: (i // tiles, 0, i % tiles))],
        out_shape=[jax.ShapeDtypeStruct((m, HEADS * C_QK), BF16),
                   jax.ShapeDtypeStruct((b, HEADS * VT_ROWS, s), BF16)],
        compiler_params=_params(("parallel",)),
        name="mla_kv",
    )(proj, proj, g_kv.reshape(1, KV_LORA), w_k_p, w_vt_p, *tabs)
    return q_cat, k_cat, vt


MLA_T = 256
MLA_W = 512
MLA_HG = 6


def _mla_attn_kernel(q_ref, k_ref, vt_ref, o_ref, m_ref, acc_ref, s_ref,
                     smax_ref):
    t, w = MLA_T, MLA_W
    i = pl.program_id(2)
    k_off = lax.broadcasted_iota(jnp.int32, (w, t), 0)
    q_pos = i * t + lax.broadcasted_iota(jnp.int32, (w, t), 1)

    m_ref[...] = jnp.full_like(m_ref, -jnp.inf)
    acc_ref[...] = jnp.zeros_like(acc_ref)

    def scores(j, slot):
        ks = pl.ds(pl.multiple_of(j * w, w), w)
        for h in range(MLA_HG):
            s = lax.dot_general(k_ref[0, ks, h * C_QK:(h + 1) * C_QK],
                                q_ref[0, :, h * C_QK:(h + 1) * C_QK], _NT,
                                preferred_element_type=F32)
            s_ref[slot, h] = s
            smax_ref[slot, h] = jnp.max(s, axis=0, keepdims=True)

    def softmax_pv(j, slot, masked):
        start = pl.multiple_of(j * w, w)
        ks = pl.ds(start, w)
        for h in range(MLA_HG):
            s = s_ref[slot, h]
            if masked:
                s = jnp.where((start + k_off) <= q_pos, s, -jnp.inf)
                s_max = jnp.max(s, axis=0, keepdims=True)
            else:
                s_max = smax_ref[slot, h]
            m_old = m_ref[h]
            m_new = jnp.maximum(m_old, s_max)
            alpha = jnp.exp2(m_old - m_new)
            p = jnp.exp2(s - m_new)
            acc_ref[h] = alpha * acc_ref[h] + jnp.dot(
                vt_ref[0, h * VT_ROWS:(h + 1) * VT_ROWS, ks], p.astype(BF16),
                preferred_element_type=F32)
            m_ref[h] = m_new

    last = (i * t) // w

    scores(0, 0)

    def pair(jj, carry):
        j = 2 * jj
        scores(j + 1, 1)
        softmax_pv(j, 0, False)
        scores(j + 2, 0)
        softmax_pv(j + 1, 1, False)
        return carry

    lax.fori_loop(0, last // 2, pair, 0)

    @pl.when(last % 2 == 0)
    def _():
        softmax_pv(last, 0, True)

    @pl.when(last % 2 == 1)
    def _():
        scores(last, 1)
        softmax_pv(last - 1, 0, False)
        softmax_pv(last, 1, True)
    for h in range(MLA_HG):
        acc = acc_ref[h]
        out_t = acc[:HEAD_DIM] / acc[HEAD_DIM:HEAD_DIM + 1]
        o_ref[0, :, h * HEAD_DIM:(h + 1) * HEAD_DIM] = out_t.T.astype(o_ref.dtype)


def _mla_attention(q_cat, k_cat, vt, b, s):
    t, hg = MLA_T, MLA_HG
    assert MLA_W % t == 0 and s % MLA_W == 0 and HEADS % hg == 0
    q3 = q_cat.reshape(b, s, HEADS * C_QK)
    k3 = k_cat.reshape(b, s, HEADS * C_QK)
    return pl.pallas_call(
        _mla_attn_kernel,
        grid=(b, HEADS // hg, s // t),
        in_specs=[pl.BlockSpec((1, t, hg * C_QK), lambda bi, g, i: (bi, i, g)),
                  pl.BlockSpec((1, s, hg * C_QK), lambda bi, g, i: (bi, 0, g)),
                  pl.BlockSpec((1, hg * VT_ROWS, s), lambda bi, g, i: (bi, g, 0))],
        out_specs=pl.BlockSpec((1, t, hg * HEAD_DIM), lambda bi, g, i: (bi, i, g)),
        out_shape=jax.ShapeDtypeStruct((b, s, C_WIDTH), BF16),
        scratch_shapes=[pltpu.VMEM((hg, 1, t), F32),
                        pltpu.VMEM((hg, VT_ROWS, t), F32),
                        pltpu.VMEM((2, hg, MLA_W, t), F32),
                        pltpu.VMEM((2, hg, 1, t), F32)],
        compiler_params=_params(("parallel", "parallel", "arbitrary")),
        name="mla_attention",
    )(q3, k3, vt)


GATE_ROWS = 32


def _gate_kernel(ya_ref, yb_ref, zb_ref, yc_ref, zc_ref, g_ref, o_ref):
    o_ref[:, :A_WIDTH] = ya_ref[...]

    def chunk(c, carry):
        rows = pl.ds(pl.multiple_of(c * GATE_ROWS, GATE_ROWS), GATE_ROWS)
        lo = A_WIDTH
        for y_ref, z_ref, width in ((yb_ref, zb_ref, B_WIDTH),
                                    (yc_ref, zc_ref, C_WIDTH)):
            y = y_ref[rows, :].astype(F32)
            z = z_ref[rows, :].astype(F32)
            g = g_ref[:, lo:lo + width]
            o_ref[rows, lo:lo + width] = (y * _rms_scale(y, width) * g
                                          * _silu(z)).astype(o_ref.dtype)
            lo += width
        return carry

    lax.fori_loop(0, ya_ref.shape[0] // GATE_ROWS, chunk, 0)


def _gate(ya, yb, yc, main, tail, g_out, tm=MIXER_TM):
    m = ya.shape[0]
    assert m % tm == 0 and tm % GATE_ROWS == 0, (m, tm)
    return pl.pallas_call(
        _gate_kernel,
        grid=(m // tm,),
        in_specs=[pl.BlockSpec((tm, A_WIDTH), lambda i: (i, 0)),
                  pl.BlockSpec((tm, B_WIDTH), lambda i: (i, 0)),
                  pl.BlockSpec((tm, B_WIDTH), lambda i: (i, OFF_ZB // B_WIDTH)),
                  pl.BlockSpec((tm, C_WIDTH), lambda i: (i, 0)),
                  pl.BlockSpec((tm, C_WIDTH), lambda i: (i, TAIL_ZC // C_WIDTH)),
                  pl.BlockSpec((1, D_MODEL), lambda i: (0, 0))],
        out_specs=pl.BlockSpec((tm, D_MODEL), lambda i: (i, 0)),
        out_shape=jax.ShapeDtypeStruct((m, D_MODEL), BF16),
        compiler_params=_params(("parallel",)),
        name="gate",
    )(ya, yb, main, yc, tail, g_out.reshape(1, D_MODEL))


def _prep_w_uq(w):
    w = w.reshape(Q_LORA, HEADS, C_NOPE + C_ROPE)
    gap = jnp.zeros((Q_LORA, HEADS, ROPE_X2 - ROPE_HALF), w.dtype)
    w = jnp.concatenate([w[:, :, :C_NOPE + ROPE_HALF], gap,
                         w[:, :, C_NOPE + ROPE_HALF:], gap], axis=2)
    return w.reshape(Q_LORA, HEADS * C_QK).astype(BF16)


def _prep_w_ukv(w):
    w = w.reshape(KV_LORA, HEADS, C_NOPE + HEAD_DIM)
    wk = w[:, :, :C_NOPE].reshape(KV_LORA, HEADS * C_NOPE)
    wv_t = w[:, :, C_NOPE:].reshape(KV_LORA, C_WIDTH).T
    return wk.astype(BF16), wv_t.astype(BF16)


def _rope_tables(positions):
    inv_freq = 1.0 / (ROPE_THETA ** (jnp.arange(0, C_ROPE, 2, dtype=F32) / C_ROPE))
    ang = positions.astype(F32).reshape(-1, 1) * inv_freq
    cos, sin = jnp.cos(ang), jnp.sin(ang)
    gap = jnp.zeros((ang.shape[0], ROPE_X2 - ROPE_HALF), F32)
    return (jnp.concatenate([cos, gap, cos, gap], axis=1),
            jnp.concatenate([-sin, gap, sin, gap], axis=1))


def kernel(x, positions, g_pre, w_in, a_g_v, a_w_s, a_b_s, c_g_q, c_g_kv,
           c_w_uq, c_w_ukv, g_out, w_out, g_final):
    b, s, d = x.shape
    m = b * s
    depth = w_in.shape[0]
    tabs = _rope_tables(positions)
    w_in_t = jnp.swapaxes(w_in, 1, 2)
    h = x.reshape(m, d)
    for l in range(depth):
        hn = _rmsnorm(h, g_pre[l], BF16)
        main = _in_proj_main(hn, w_in_t, l)
        tail = _in_proj_tail(hn, w_in_t, l)
        ya = _gmlp(main, a_g_v[l], a_w_s[l], a_b_s[l], g_out[l, :A_WIDTH])
        yb = _stick_breaking(main.reshape(b, s, D_MAIN)).reshape(m, B_WIDTH)
        w_k_p, w_vt_p = _prep_w_ukv(c_w_ukv[l])
        q_cat, k_cat, vt = _mla_project(tail, c_g_q[l], c_g_kv[l],
                                        _prep_w_uq(c_w_uq[l]), w_k_p, w_vt_p,
                                        tabs, b, s)
        yc = _mla_attention(q_cat, k_cat, vt, b, s).reshape(m, C_WIDTH)
        y = _gate(ya, yb, yc, main, tail, g_out[l])
        h = _matmul_residual(y, w_out, l, h)
    return _rmsnorm(h, g_final, F32).reshape(b, s, d)
```

```python
import math

import jax
import jax.numpy as jnp
from jax import lax
from jax.experimental import pallas as pl
from jax.experimental.pallas import tpu as pltpu

F32 = jnp.float32
BF16 = jnp.bfloat16

D_MODEL = 4096
A_GROUPS = 8
A_CH = 128
A_WIDTH = A_GROUPS * A_CH
CHUNK = 128
HEADS = 12
HEAD_DIM = 128
B_WIDTH = HEADS * HEAD_DIM
C_NOPE = 128
C_ROPE = 64
C_WIDTH = HEADS * HEAD_DIM
Q_LORA = 768
KV_LORA = 512
ROPE_THETA = 10000.0
EPS = 1e-6

C_QK = 256
ROPE_HALF = C_ROPE // 2
ROPE_TILE = 128
ROPE_X2 = ROPE_TILE // 2

OFF_U = 0
OFF_V = OFF_U + A_WIDTH
OFF_ZA = OFF_V + A_WIDTH
OFF_QB = OFF_ZA + A_WIDTH
OFF_KB = OFF_QB + B_WIDTH
OFF_VB = OFF_KB + B_WIDTH
OFF_ZB = OFF_VB + B_WIDTH
D_MAIN = OFF_ZB + B_WIDTH
REF_CQ = D_MAIN
REF_CKV = REF_CQ + Q_LORA
REF_KR = REF_CKV + KV_LORA
REF_ZC = REF_KR + C_ROPE
TAIL_ZC = 0
TAIL_CKV = TAIL_ZC + C_WIDTH
TAIL_KR = TAIL_CKV + KV_LORA
TAIL_CQ = TAIL_KR + C_QK
D_TAIL = TAIL_CQ + Q_LORA

V7X_VMEM_BYTES = 64 * 1024 * 1024
VMEM_LIMIT_BYTES = V7X_VMEM_BYTES * 7 // 8

NORM_TM = 512
PROJ_TM = 1024
PROJ_TN = 512
MAIN_TN = 768
TAIL_TK = 512
MIXER_TM = 1024


def _params(sem):
    return pltpu.CompilerParams(dimension_semantics=sem,
                                vmem_limit_bytes=VMEM_LIMIT_BYTES)


def _gelu(x):
    c = math.sqrt(2.0 / math.pi)
    return x * (0.5 * (1.0 + jnp.tanh(c * (x + 0.044715 * (x * x * x)))))


def _silu(z):
    return z / (1.0 + jnp.exp(-z))


def _rms_scale(x, width):
    return lax.rsqrt(jnp.sum(x * x, axis=-1, keepdims=True) * (1.0 / width) + EPS)


def _rmsnorm_kernel(x_ref, g_ref, o_ref):
    x = x_ref[...]
    o_ref[...] = (x * _rms_scale(x, x.shape[-1]) * g_ref[...]).astype(o_ref.dtype)


def _rmsnorm(x, g, out_dtype, tm=NORM_TM):
    m, d = x.shape
    assert m % tm == 0, (m, tm)
    return pl.pallas_call(
        _rmsnorm_kernel,
        grid=(m // tm,),
        in_specs=[pl.BlockSpec((tm, d), lambda i: (i, 0)),
                  pl.BlockSpec((1, d), lambda i: (0, 0))],
        out_specs=pl.BlockSpec((tm, d), lambda i: (i, 0)),
        out_shape=jax.ShapeDtypeStruct((m, d), out_dtype),
        compiler_params=_params(("parallel",)),
        name="rmsnorm",
    )(x, g.reshape(1, d))


_NT = (((1,), (1,)), ((), ()))


def _in_proj_main_kernel(x_ref, wt_ref, o_ref):
    o_ref[...] = lax.dot_general(x_ref[...], wt_ref[...].astype(BF16), _NT,
                                 preferred_element_type=F32).astype(o_ref.dtype)


def _in_proj_main(x, wt, layer, tm=PROJ_TM, tn=MAIN_TN):
    m, k = x.shape
    assert m % tm == 0 and D_MAIN % tn == 0 and wt.shape[2] == k, (m, tm, tn)
    return pl.pallas_call(
        _in_proj_main_kernel,
        grid=(m // tm, D_MAIN // tn),
        in_specs=[pl.BlockSpec((tm, k), lambda i, j: (i, 0)),
                  pl.BlockSpec((None, tn, k), lambda i, j: (layer, j, 0))],
        out_specs=pl.BlockSpec((tm, tn), lambda i, j: (i, j)),
        out_shape=jax.ShapeDtypeStruct((m, D_MAIN), BF16),
        compiler_params=_params(("parallel", "parallel")),
        name="in_proj_main",
    )(x, wt)


TAIL_PART_ROWS = 576
TAIL_PARTS = 5
TAIL_COPIES = (
    (TAIL_ZC, REF_ZC - D_MAIN, C_WIDTH),
    (TAIL_CKV, REF_CKV - D_MAIN, KV_LORA),
    (TAIL_KR, REF_KR - D_MAIN, ROPE_HALF),
    (TAIL_KR + ROPE_X2, REF_KR - D_MAIN + ROPE_HALF, ROPE_HALF),
    (TAIL_CQ, REF_CQ - D_MAIN, Q_LORA),
)
TAIL_ZERO_ROWS = ((TAIL_KR + ROPE_HALF, ROPE_X2 - ROPE_HALF),
                  (TAIL_KR + ROPE_X2 + ROPE_HALF, ROPE_X2 - ROPE_HALF),
                  (TAIL_KR + ROPE_TILE, TAIL_CQ - TAIL_KR - ROPE_TILE))


def _in_proj_tail_kernel(x_ref, *refs):
    w_refs = refs[:TAIL_PARTS]
    o_ref, wp_ref, acc_ref = refs[TAIL_PARTS:]
    kk = pl.program_id(1)

    @pl.when(kk == 0)
    def _():
        acc_ref[...] = jnp.zeros_like(acc_ref)

    for dst, src, rows in TAIL_COPIES:
        done = 0
        while done < rows:
            part, lo = divmod(src + done, TAIL_PART_ROWS)
            n = min(rows - done, TAIL_PART_ROWS - lo)
            wp_ref[dst + done:dst + done + n, :] = (
                w_refs[part][lo:lo + n, :].astype(BF16))
            done += n
    for dst, rows in TAIL_ZERO_ROWS:
        wp_ref[dst:dst + rows, :] = jnp.zeros((rows, wp_ref.shape[1]), BF16)

    acc_ref[...] += lax.dot_general(x_ref[...], wp_ref[...], _NT,
                                    preferred_element_type=F32)

    @pl.when(kk == pl.num_programs(1) - 1)
    def _():
        o_ref[...] = acc_ref[...].astype(o_ref.dtype)


def _in_proj_tail(x, wt, layer, tm=PROJ_TM, tk=TAIL_TK):
    m, k = x.shape
    assert m % tm == 0 and k % tk == 0 and D_MAIN % TAIL_PART_ROWS == 0
    assert wt.shape[1] == D_MAIN + TAIL_PARTS * TAIL_PART_ROWS, wt.shape
    first = D_MAIN // TAIL_PART_ROWS
    w_specs = [pl.BlockSpec((None, TAIL_PART_ROWS, tk),
                            lambda i, kk, p=p: (layer, first + p, kk))
               for p in range(TAIL_PARTS)]
    return pl.pallas_call(
        _in_proj_tail_kernel,
        grid=(m // tm, k // tk),
        in_specs=[pl.BlockSpec((tm, tk), lambda i, kk: (i, kk))] + w_specs,
        out_specs=pl.BlockSpec((tm, D_TAIL), lambda i, kk: (i, 0)),
        out_shape=jax.ShapeDtypeStruct((m, D_TAIL), BF16),
        scratch_shapes=[pltpu.VMEM((D_TAIL, tk), BF16),
                        pltpu.VMEM((tm, D_TAIL), F32)],
        compiler_params=_params(("parallel", "arbitrary")),
        name="in_proj_tail",
    )(x, *([wt] * TAIL_PARTS))


def _matmul_residual_kernel(x_ref, w_ref, r_ref, o_ref):
    o_ref[...] = r_ref[...] + jnp.dot(x_ref[...], w_ref[...].astype(BF16),
                                      preferred_element_type=F32)


def _matmul_residual(x, w, layer, res, tm=PROJ_TM, tn=PROJ_TN):
    m, k = x.shape
    n = w.shape[2]
    assert m % tm == 0 and n % tn == 0 and w.shape[1] == k, (m, k, n)
    return pl.pallas_call(
        _matmul_residual_kernel,
        grid=(m // tm, n // tn),
        in_specs=[pl.BlockSpec((tm, k), lambda i, j: (i, 0)),
                  pl.BlockSpec((None, k, tn), lambda i, j: (layer, 0, j)),
                  pl.BlockSpec((tm, tn), lambda i, j: (i, j))],
        out_specs=pl.BlockSpec((tm, tn), lambda i, j: (i, j)),
        out_shape=jax.ShapeDtypeStruct((m, n), F32),
        compiler_params=_params(("parallel", "parallel")),
        name="out_proj",
    )(x, w, res)


def _gmlp_kernel(u_ref, v_ref, z_ref, gv_ref, w_ref, bias_ref, go_ref, o_ref,
                 y_ref):
    tm = u_ref.shape[0]
    t_idx = lax.broadcasted_iota(jnp.int32, (CHUNK, CHUNK), 0)
    s_idx = lax.broadcasted_iota(jnp.int32, (CHUNK, CHUNK), 1)
    causal = s_idx <= t_idx

    def chunk(c, carry):
        rows = pl.ds(pl.multiple_of(c * CHUNK, CHUNK), CHUNK)
        ssq = jnp.zeros((CHUNK, 1), F32)
        for g in range(A_GROUPS):
            cols = slice(g * A_CH, (g + 1) * A_CH)
            vg = _gelu(v_ref[rows, cols].astype(F32))
            vn = vg * _rms_scale(vg, A_CH) * gv_ref[:, cols]
            w = jnp.where(causal, w_ref[g], 0.0).astype(BF16)
            sv = jnp.dot(w, vn.astype(BF16), preferred_element_type=F32)
            sv = sv + bias_ref[:, cols]
            y = _gelu(u_ref[rows, cols].astype(F32)) * sv
            ssq = ssq + jnp.sum(y * y, axis=-1, keepdims=True)
            y_ref[:, cols] = y
        r = lax.rsqrt(ssq * (1.0 / A_WIDTH) + EPS)
        z = z_ref[rows, :].astype(F32)
        o_ref[rows, :] = (y_ref[...] * r * go_ref[...] * _silu(z)).astype(o_ref.dtype)
        return carry

    lax.fori_loop(0, tm // CHUNK, chunk, 0)


def _gmlp(proj, g_v, w_s, b_s, g_out_a, tm=MIXER_TM):
    m = proj.shape[0]
    assert m % tm == 0 and tm % CHUNK == 0, (m, tm)
    bias = jnp.repeat(b_s.T, A_CH, axis=1)
    col = lambda off: (lambda i: (i, off // A_WIDTH))
    return pl.pallas_call(
        _gmlp_kernel,
        grid=(m // tm,),
        in_specs=[pl.BlockSpec((tm, A_WIDTH), col(OFF_U)),
                  pl.BlockSpec((tm, A_WIDTH), col(OFF_V)),
                  pl.BlockSpec((tm, A_WIDTH), col(OFF_ZA)),
                  pl.BlockSpec((1, A_WIDTH), lambda i: (0, 0)),
                  pl.BlockSpec((A_GROUPS, CHUNK, CHUNK), lambda i: (0, 0, 0)),
                  pl.BlockSpec((CHUNK, A_WIDTH), lambda i: (0, 0)),
                  pl.BlockSpec((1, A_WIDTH), lambda i: (0, 0))],
        out_specs=pl.BlockSpec((tm, A_WIDTH), lambda i: (i, 0)),
        out_shape=jax.ShapeDtypeStruct((m, A_WIDTH), BF16),
        scratch_shapes=[pltpu.VMEM((CHUNK, A_WIDTH), F32)],
        compiler_params=_params(("parallel",)),
        name="gmlp",
    )(proj, proj, proj, g_v.reshape(1, A_WIDTH), w_s, bias,
      g_out_a.reshape(1, A_WIDTH))


SB_T = 128
SB_SUB = 128
SB_NSUB = 3
SB_W = SB_NSUB * SB_SUB
SB_HG = 6
SB_STAGE = 6
SB_ZERO_WEIGHT_LOG = -105.0


def _sb_kernel(q_ref, k_ref, v_ref, o_ref, acc_ref, carry_ref):
    t, w, sub = SB_T, SB_W, SB_SUB
    i = pl.program_id(2)
    scale = HEAD_DIM ** -0.5
    q_pos = i * t + lax.broadcasted_iota(jnp.int32, (t, w), 0)
    k_off = lax.broadcasted_iota(jnp.int32, (t, w), 1)
    row = lax.broadcasted_iota(jnp.int32, (2 * sub, sub), 0) % sub
    col = lax.broadcasted_iota(jnp.int32, (2 * sub, sub), 1)
    neg_suffix = jnp.where(row > col, -1.0, 0.0).astype(BF16)

    acc_ref[...] = jnp.zeros_like(acc_ref)
    carry_ref[...] = jnp.zeros_like(carry_ref)

    def window(start, bound):
        valid = (start + k_off) < jnp.minimum(q_pos, bound)
        ks = pl.ds(pl.multiple_of(start, sub), w)
        worst = None
        for h0 in range(0, SB_HG, SB_STAGE):
            worst = stage(range(h0, h0 + SB_STAGE), valid, ks, worst)
        return worst

    def stage(heads, valid, ks, worst):
        log_beta, softplus, pieces = {}, {}, []
        for h in heads:
            hc = slice(h * HEAD_DIM, (h + 1) * HEAD_DIM)
            z = lax.dot_general(q_ref[0, :, hc], k_ref[0, ks, hc],
                                (((1,), (1,)), ((), ())),
                                preferred_element_type=F32) * scale
            sp = jnp.maximum(z, 0.0) + jnp.log(1.0 + jnp.exp(-jnp.abs(z)))
            log_beta[h] = z - sp
            sp = jnp.where(valid, sp, 0.0)
            softplus[h] = sp
            hi = sp.astype(BF16)
            lo = (sp - hi.astype(F32)).astype(BF16)
            for c in range(SB_NSUB):
                pieces.append(jnp.concatenate(
                    [hi[:, c * sub:(c + 1) * sub], lo[:, c * sub:(c + 1) * sub]],
                    axis=1))
        local = jnp.dot(jnp.concatenate(pieces, axis=0), neg_suffix,
                        preferred_element_type=F32)
        for n, h in enumerate(heads):
            hc = slice(h * HEAD_DIM, (h + 1) * HEAD_DIM)
            sp = softplus[h]
            running = carry_ref[h]
            after = [None] * SB_NSUB
            for c in reversed(range(SB_NSUB)):
                base = (n * SB_NSUB + c) * t
                after[c] = local[base:base + t] + running
                running = running - jnp.sum(sp[:, c * sub:(c + 1) * sub],
                                            axis=-1, keepdims=True)
            a = jnp.where(valid,
                          jnp.exp(log_beta[h] + jnp.concatenate(after, axis=1)),
                          0.0)
            acc_ref[h] += jnp.dot(a.astype(BF16), v_ref[0, ks, hc],
                                  preferred_element_type=F32)
            carry_ref[h] = running
            top = jnp.max(running)
            worst = top if worst is None else jnp.maximum(worst, top)
        return worst

    first = jnp.maximum(i + 1 - SB_NSUB, 0) * sub
    worst = window(first, jnp.int32(2 ** 30))

    def more(state):
        start, worst = state
        return jnp.logical_and(start > 0, worst > SB_ZERO_WEIGHT_LOG)

    def step(state):
        start, _ = state
        nxt = jnp.maximum(start - w, 0)
        return nxt, window(nxt, start)

    lax.while_loop(more, step, (first, worst))
    for h in range(SB_HG):
        o_ref[0, :, h * HEAD_DIM:(h + 1) * HEAD_DIM] = acc_ref[h].astype(o_ref.dtype)


def _stick_breaking(proj3):
    b, s, _ = proj3.shape
    assert s % SB_T == 0 and s >= SB_W and SB_T == SB_SUB and HEADS % SB_HG == 0
    t, gw = SB_T, SB_HG * HEAD_DIM
    blk = lambda off: off // gw
    return pl.pallas_call(
        _sb_kernel,
        grid=(b, HEADS // SB_HG, s // t),
        in_specs=[
            pl.BlockSpec((1, t, gw), lambda bi, g, i: (bi, i, blk(OFF_QB) + g)),
            pl.BlockSpec((1, s, gw), lambda bi, g, i: (bi, 0, blk(OFF_KB) + g)),
            pl.BlockSpec((1, s, gw), lambda bi, g, i: (bi, 0, blk(OFF_VB) + g)),
        ],
        out_specs=pl.BlockSpec((1, t, gw), lambda bi, g, i: (bi, i, g)),
        out_shape=jax.ShapeDtypeStruct((b, s, B_WIDTH), BF16),
        scratch_shapes=[pltpu.VMEM((SB_HG, t, HEAD_DIM), F32),
                        pltpu.VMEM((SB_HG, t, 1), F32)],
        compiler_params=_params(("parallel", "parallel", "arbitrary")),
        name="stick_breaking",
    )(proj3, proj3, proj3)


def _rope_tile(x, cos_ref, sin_ref):
    return x * cos_ref[...] + pltpu.roll(x, ROPE_X2, axis=1) * sin_ref[...]


def _mla_q_kernel(cq_ref, g_ref, w_ref, cos_ref, sin_ref, o_ref):
    scale = (C_NOPE + C_ROPE) ** -0.5 * math.log2(math.e)
    x = cq_ref[...].astype(F32)
    xn = (x * _rms_scale(x, Q_LORA) * g_ref[...]).astype(BF16)
    q = jnp.dot(xn, w_ref[...], preferred_element_type=F32)
    for h in range(HEADS):
        lo = h * C_QK
        o_ref[:, lo:lo + C_NOPE] = (q[:, lo:lo + C_NOPE] * scale).astype(o_ref.dtype)
        rope = _rope_tile(q[:, lo + C_NOPE:lo + C_QK], cos_ref, sin_ref)
        o_ref[:, lo + C_NOPE:lo + C_QK] = (rope * scale).astype(o_ref.dtype)


VT_ONES = 16
VT_ROWS = HEAD_DIM + VT_ONES


def _mla_kv_kernel(ckv_ref, kr_ref, g_ref, wk_ref, wvt_ref, cos_ref, sin_ref,
                   k_ref, vt_ref):
    tm = ckv_ref.shape[0]
    x = ckv_ref[...].astype(F32)
    xn = (x * _rms_scale(x, KV_LORA) * g_ref[...]).astype(BF16)
    kn = jnp.dot(xn, wk_ref[...], preferred_element_type=F32).astype(k_ref.dtype)
    kr = _rope_tile(kr_ref[...].astype(F32), cos_ref, sin_ref).astype(k_ref.dtype)
    for h in range(HEADS):
        k_ref[:, h * C_QK:h * C_QK + C_NOPE] = kn[:, h * C_NOPE:(h + 1) * C_NOPE]
        k_ref[:, h * C_QK + C_NOPE:(h + 1) * C_QK] = kr
    vt = lax.dot_general(wvt_ref[...], xn, _NT, preferred_element_type=F32)
    for h in range(HEADS):
        vt_ref[h * VT_ROWS:h * VT_ROWS + HEAD_DIM, :] = (
            vt[h * HEAD_DIM:(h + 1) * HEAD_DIM, :].astype(vt_ref.dtype))
        vt_ref[h * VT_ROWS + HEAD_DIM:(h + 1) * VT_ROWS, :] = jnp.ones(
            (VT_ONES, tm), vt_ref.dtype)


def _mla_project(proj, g_q, g_kv, w_uq_p, w_k_p, w_vt_p, tabs, b, s,
                 tm=MIXER_TM):
    m = proj.shape[0]
    assert m == b * s and s % tm == 0, (m, b, s, tm)
    tiles = s // tm
    tab_spec = pl.BlockSpec((tm, ROPE_TILE), lambda i: (i, 0))
    q_cat = pl.pallas_call(
        _mla_q_kernel,
        grid=(m // tm,),
        in_specs=[pl.BlockSpec((tm, Q_LORA), lambda i: (i, TAIL_CQ // Q_LORA)),
                  pl.BlockSpec((1, Q_LORA), lambda i: (0, 0)),
                  pl.BlockSpec((Q_LORA, HEADS * C_QK), lambda i: (0, 0)),
                  tab_spec, tab_spec],
        out_specs=pl.BlockSpec((tm, HEADS * C_QK), lambda i: (i, 0)),
        out_shape=jax.ShapeDtypeStruct((m, HEADS * C_QK), BF16),
        compiler_params=_params(("parallel",)),
        name="mla_q",
    )(proj, g_q.reshape(1, Q_LORA), w_uq_p, *tabs)
    k_cat, vt = pl.pallas_call(
        _mla_kv_kernel,
        grid=(m // tm,),
        in_specs=[pl.BlockSpec((tm, KV_LORA), lambda i: (i, TAIL_CKV // KV_LORA)),
                  pl.BlockSpec((tm, ROPE_TILE), lambda i: (i, TAIL_KR // ROPE_TILE)),
                  pl.BlockSpec((1, KV_LORA), lambda i: (0, 0)),
                  pl.BlockSpec((KV_LORA, HEADS * C_NOPE), lambda i: (0, 0)),
                  pl.BlockSpec((C_WIDTH, KV_LORA), lambda i: (0, 0)),
                  tab_spec, tab_spec],
        out_specs=[pl.BlockSpec((tm, HEADS * C_QK), lambda i: (i, 0)),
                   pl.BlockSpec((None, HEADS * VT_ROWS, tm),
                                lambda i: (i // tiles, 0, i % tiles))],
        out_shape=[jax.ShapeDtypeStruct((m, HEADS * C_QK), BF16),
                   jax.ShapeDtypeStruct((b, HEADS * VT_ROWS, s), BF16)],
        compiler_params=_params(("parallel",)),
        name="mla_kv",
    )(proj, proj, g_kv.reshape(1, KV_LORA), w_k_p, w_vt_p, *tabs)
    return q_cat, k_cat, vt


MLA_T = 256
MLA_W = 512
MLA_HG = 6


def _mla_attn_kernel(q_ref, k_ref, vt_ref, o_ref, m_ref, acc_ref, s_ref,
                     smax_ref):
    t, w = MLA_T, MLA_W
    i = pl.program_id(2)
    k_off = lax.broadcasted_iota(jnp.int32, (w, t), 0)
    q_pos = i * t + lax.broadcasted_iota(jnp.int32, (w, t), 1)

    m_ref[...] = jnp.full_like(m_ref, -jnp.inf)
    acc_ref[...] = jnp.zeros_like(acc_ref)

    def scores(j, slot):
        ks = pl.ds(pl.multiple_of(j * w, w), w)
        for h in range(MLA_HG):
            s = lax.dot_general(k_ref[0, ks, h * C_QK:(h + 1) * C_QK],
                                q_ref[0, :, h * C_QK:(h + 1) * C_QK], _NT,
                                preferred_element_type=F32)
            s_ref[slot, h] = s
            smax_ref[slot, h] = jnp.max(s, axis=0, keepdims=True)

    def softmax_pv(j, slot, masked):
        start = pl.multiple_of(j * w, w)
        ks = pl.ds(start, w)
        for h in range(MLA_HG):
            s = s_ref[slot, h]
            if masked:
                s = jnp.where((start + k_off) <= q_pos, s, -jnp.inf)
                s_max = jnp.max(s, axis=0, keepdims=True)
            else:
                s_max = smax_ref[slot, h]
            m_old = m_ref[h]
            m_new = jnp.maximum(m_old, s_max)
            alpha = jnp.exp2(m_old - m_new)
            p = jnp.exp2(s - m_new)
            acc_ref[h] = alpha * acc_ref[h] + jnp.dot(
                vt_ref[0, h * VT_ROWS:(h + 1) * VT_ROWS, ks], p.astype(BF16),
                preferred_element_type=F32)
            m_ref[h] = m_new

    last = (i * t) // w

    scores(0, 0)

    def pair(jj, carry):
        j = 2 * jj
        scores(j + 1, 1)
        softmax_pv(j, 0, False)
        scores(j + 2, 0)
        softmax_pv(j + 1, 1, False)
        return carry

    lax.fori_loop(0, last // 2, pair, 0)

    @pl.when(last % 2 == 0)
    def _():
        softmax_pv(last, 0, True)

    @pl.when(last % 2 == 1)
    def _():
        scores(last, 1)
        softmax_pv(last - 1, 0, False)
        softmax_pv(last, 1, True)
    for h in range(MLA_HG):
        acc = acc_ref[h]
        out_t = acc[:HEAD_DIM] / acc[HEAD_DIM:HEAD_DIM + 1]
        o_ref[0, :, h * HEAD_DIM:(h + 1) * HEAD_DIM] = out_t.T.astype(o_ref.dtype)


def _mla_attention(q_cat, k_cat, vt, b, s):
    t, hg = MLA_T, MLA_HG
    assert MLA_W % t == 0 and s % MLA_W == 0 and HEADS % hg == 0
    q3 = q_cat.reshape(b, s, HEADS * C_QK)
    k3 = k_cat.reshape(b, s, HEADS * C_QK)
    return pl.pallas_call(
        _mla_attn_kernel,
        grid=(b, HEADS // hg, s // t),
        in_specs=[pl.BlockSpec((1, t, hg * C_QK), lambda bi, g, i: (bi, i, g)),
                  pl.BlockSpec((1, s, hg * C_QK), lambda bi, g, i: (bi, 0, g)),
                  pl.BlockSpec((1, hg * VT_ROWS, s), lambda bi, g, i: (bi, g, 0))],
        out_specs=pl.BlockSpec((1, t, hg * HEAD_DIM), lambda bi, g, i: (bi, i, g)),
        out_shape=jax.ShapeDtypeStruct((b, s, C_WIDTH), BF16),
        scratch_shapes=[pltpu.VMEM((hg, 1, t), F32),
                        pltpu.VMEM((hg, VT_ROWS, t), F32),
                        pltpu.VMEM((2, hg, MLA_W, t), F32),
                        pltpu.VMEM((2, hg, 1, t), F32)],
        compiler_params=_params(("parallel", "parallel", "arbitrary")),
        name="mla_attention",
    )(q3, k3, vt)


GATE_ROWS = 32


def _gate_kernel(ya_ref, yb_ref, zb_ref, yc_ref, zc_ref, g_ref, o_ref):
    o_ref[:, :A_WIDTH] = ya_ref[...]

    def chunk(c, carry):
        rows = pl.ds(pl.multiple_of(c * GATE_ROWS, GATE_ROWS), GATE_ROWS)
        lo = A_WIDTH
        for y_ref, z_ref, width in ((yb_ref, zb_ref, B_WIDTH),
                                    (yc_ref, zc_ref, C_WIDTH)):
            y = y_ref[rows, :].astype(F32)
            z = z_ref[rows, :].astype(F32)
            g = g_ref[:, lo:lo + width]
            o_ref[rows, lo:lo + width] = (y * _rms_scale(y, width) * g
                                          * _silu(z)).astype(o_ref.dtype)
            lo += width
        return carry

    lax.fori_loop(0, ya_ref.shape[0] // GATE_ROWS, chunk, 0)


def _gate(ya, yb, yc, main, tail, g_out, tm=MIXER_TM):
    m = ya.shape[0]
    assert m % tm == 0 and tm % GATE_ROWS == 0, (m, tm)
    return pl.pallas_call(
        _gate_kernel,
        grid=(m // tm,),
        in_specs=[pl.BlockSpec((tm, A_WIDTH), lambda i: (i, 0)),
                  pl.BlockSpec((tm, B_WIDTH), lambda i: (i, 0)),
                  pl.BlockSpec((tm, B_WIDTH), lambda i: (i, OFF_ZB // B_WIDTH)),
                  pl.BlockSpec((tm, C_WIDTH), lambda i: (i, 0)),
                  pl.BlockSpec((tm, C_WIDTH), lambda i: (i, TAIL_ZC // C_WIDTH)),
                  pl.BlockSpec((1, D_MODEL), lambda i: (0, 0))],
        out_specs=pl.BlockSpec((tm, D_MODEL), lambda i: (i, 0)),
        out_shape=jax.ShapeDtypeStruct((m, D_MODEL), BF16),
        compiler_params=_params(("parallel",)),
        name="gate",
    )(ya, yb, main, yc, tail, g_out.reshape(1, D_MODEL))


def _prep_w_uq(w):
    w = w.reshape(Q_LORA, HEADS, C_NOPE + C_ROPE)
    gap = jnp.zeros((Q_LORA, HEADS, ROPE_X2 - ROPE_HALF), w.dtype)
    w = jnp.concatenate([w[:, :, :C_NOPE + ROPE_HALF], gap,
                         w[:, :, C_NOPE + ROPE_HALF:], gap], axis=2)
    return w.reshape(Q_LORA, HEADS * C_QK).astype(BF16)


def _prep_w_ukv(w):
    w = w.reshape(KV_LORA, HEADS, C_NOPE + HEAD_DIM)
    wk = w[:, :, :C_NOPE].reshape(KV_LORA, HEADS * C_NOPE)
    wv_t = w[:, :, C_NOPE:].reshape(KV_LORA, C_WIDTH).T
    return wk.astype(BF16), wv_t.astype(BF16)


def _rope_tables(positions):
    inv_freq = 1.0 / (ROPE_THETA ** (jnp.arange(0, C_ROPE, 2, dtype=F32) / C_ROPE))
    ang = positions.astype(F32).reshape(-1, 1) * inv_freq
    cos, sin = jnp.cos(ang), jnp.sin(ang)
    gap = jnp.zeros((ang.shape[0], ROPE_X2 - ROPE_HALF), F32)
    return (jnp.concatenate([cos, gap, cos, gap], axis=1),
            jnp.concatenate([-sin, gap, sin, gap], axis=1))


def kernel(x, positions, g_pre, w_in, a_g_v, a_w_s, a_b_s, c_g_q, c_g_kv,
           c_w_uq, c_w_ukv, g_out, w_out, g_final):
    b, s, d = x.shape
    m = b * s
    depth = w_in.shape[0]
    tabs = _rope_tables(positions)
    w_in_t = jnp.swapaxes(w_in, 1, 2)
    h = x.reshape(m, d)
    for l in range(depth):
        hn = _rmsnorm(h, g_pre[l], BF16)
        main = _in_proj_main(hn, w_in_t, l)
        tail = _in_proj_tail(hn, w_in_t, l)
        ya = _gmlp(main, a_g_v[l], a_w_s[l], a_b_s[l], g_out[l, :A_WIDTH])
        yb = _stick_breaking(main.reshape(b, s, D_MAIN)).reshape(m, B_WIDTH)
        w_k_p, w_vt_p = _prep_w_ukv(c_w_ukv[l])
        q_cat, k_cat, vt = _mla_project(tail, c_g_q[l], c_g_kv[l],
                                        _prep_w_uq(c_w_uq[l]), w_k_p, w_vt_p,
                                        tabs, b, s)
        yc = _mla_attention(q_cat, k_cat, vt, b, s).reshape(m, C_WIDTH)
        y = _gate(ya, yb, yc, main, tail, g_out[l])
        h = _matmul_residual(y, w_out, l, h)
    return _rmsnorm(h, g_final, F32).reshape(b, s, d)
```

```python
import math

import jax
import jax.numpy as jnp
from jax import lax
from jax.experimental import pallas as pl
from jax.experimental.pallas import tpu as pltpu

F32 = jnp.float32
BF16 = jnp.bfloat16

D_MODEL = 4096
A_GROUPS = 8
A_CH = 128
A_WIDTH = A_GROUPS * A_CH
CHUNK = 128
HEADS = 12
HEAD_DIM = 128
B_WIDTH = HEADS * HEAD_DIM
C_NOPE = 128
C_ROPE = 64
C_WIDTH = HEADS * HEAD_DIM
Q_LORA = 768
KV_LORA = 512
ROPE_THETA = 10000.0
EPS = 1e-6

C_QK = 256
ROPE_HALF = C_ROPE // 2
ROPE_TILE = 128
ROPE_X2 = ROPE_TILE // 2

OFF_U = 0
OFF_V = OFF_U + A_WIDTH
OFF_ZA = OFF_V + A_WIDTH
OFF_QB = OFF_ZA + A_WIDTH
OFF_KB = OFF_QB + B_WIDTH
OFF_VB = OFF_KB + B_WIDTH
OFF_ZB = OFF_VB + B_WIDTH
D_MAIN = OFF_ZB + B_WIDTH
REF_CQ = D_MAIN
REF_CKV = REF_CQ + Q_LORA
REF_KR = REF_CKV + KV_LORA
REF_ZC = REF_KR + C_ROPE
TAIL_ZC = 0
TAIL_CKV = TAIL_ZC + C_WIDTH
TAIL_KR = TAIL_CKV + KV_LORA
TAIL_CQ = TAIL_KR + C_QK
D_TAIL = TAIL_CQ + Q_LORA

V7X_VMEM_BYTES = 64 * 1024 * 1024
VMEM_LIMIT_BYTES = V7X_VMEM_BYTES * 7 // 8

NORM_TM = 512
PROJ_TM = 1024
PROJ_TN = 512
MAIN_TN = 768
TAIL_TK = 512
MIXER_TM = 1024


def _params(sem):
    return pltpu.CompilerParams(dimension_semantics=sem,
                                vmem_limit_bytes=VMEM_LIMIT_BYTES)


def _gelu(x):
    c = math.sqrt(2.0 / math.pi)
    return x * (0.5 * (1.0 + jnp.tanh(c * (x + 0.044715 * (x * x * x)))))


def _silu(z):
    return z / (1.0 + jnp.exp(-z))


def _rms_scale(x, width):
    return lax.rsqrt(jnp.sum(x * x, axis=-1, keepdims=True) * (1.0 / width) + EPS)


NORM_ROWS = 8


def _rmsnorm_kernel(x_ref, g_ref, o_ref):
    def chunk(c, carry):
        rows = pl.ds(pl.multiple_of(c * NORM_ROWS, NORM_ROWS), NORM_ROWS)
        x = x_ref[rows, :]
        o_ref[rows, :] = (x * _rms_scale(x, x.shape[-1])
                          * g_ref[...]).astype(o_ref.dtype)
        return carry

    lax.fori_loop(0, x_ref.shape[0] // NORM_ROWS, chunk, 0, unroll=4)


def _rmsnorm(x, g, out_dtype, tm=NORM_TM):
    m, d = x.shape
    assert m % tm == 0, (m, tm)
    return pl.pallas_call(
        _rmsnorm_kernel,
        grid=(m // tm,),
        in_specs=[pl.BlockSpec((tm, d), lambda i: (i, 0)),
                  pl.BlockSpec((1, d), lambda i: (0, 0))],
        out_specs=pl.BlockSpec((tm, d), lambda i: (i, 0)),
        out_shape=jax.ShapeDtypeStruct((m, d), out_dtype),
        compiler_params=_params(("parallel",)),
        name="rmsnorm",
    )(x, g.reshape(1, d))


_NT = (((1,), (1,)), ((), ()))


def _in_proj_main_kernel(x_ref, wt_ref, o_ref):
    o_ref[...] = lax.dot_general(x_ref[...], wt_ref[...].astype(BF16), _NT,
                                 preferred_element_type=F32).astype(o_ref.dtype)


def _in_proj_main(x, wt, layer, tm=PROJ_TM, tn=MAIN_TN):
    m, k = x.shape
    assert m % tm == 0 and D_MAIN % tn == 0 and wt.shape[2] == k, (m, tm, tn)
    return pl.pallas_call(
        _in_proj_main_kernel,
        grid=(m // tm, D_MAIN // tn),
        in_specs=[pl.BlockSpec((tm, k), lambda i, j: (i, 0)),
                  pl.BlockSpec((None, tn, k), lambda i, j: (layer, j, 0))],
        out_specs=pl.BlockSpec((tm, tn), lambda i, j: (i, j)),
        out_shape=jax.ShapeDtypeStruct((m, D_MAIN), BF16),
        compiler_params=_params(("parallel", "parallel")),
        name="in_proj_main",
    )(x, wt)


TAIL_PART_ROWS = 576
TAIL_PARTS = 5
TAIL_COPIES = (
    (TAIL_ZC, REF_ZC - D_MAIN, C_WIDTH),
    (TAIL_CKV, REF_CKV - D_MAIN, KV_LORA),
    (TAIL_KR, REF_KR - D_MAIN, ROPE_HALF),
    (TAIL_KR + ROPE_X2, REF_KR - D_MAIN + ROPE_HALF, ROPE_HALF),
    (TAIL_CQ, REF_CQ - D_MAIN, Q_LORA),
)
TAIL_ZERO_ROWS = ((TAIL_KR + ROPE_HALF, ROPE_X2 - ROPE_HALF),
                  (TAIL_KR + ROPE_X2 + ROPE_HALF, ROPE_X2 - ROPE_HALF),
                  (TAIL_KR + ROPE_TILE, TAIL_CQ - TAIL_KR - ROPE_TILE))


def _in_proj_tail_kernel(x_ref, *refs):
    w_refs = refs[:TAIL_PARTS]
    o_ref, wp_ref, acc_ref = refs[TAIL_PARTS:]
    kk = pl.program_id(1)

    @pl.when(kk == 0)
    def _():
        acc_ref[...] = jnp.zeros_like(acc_ref)

    for dst, src, rows in TAIL_COPIES:
        done = 0
        while done < rows:
            part, lo = divmod(src + done, TAIL_PART_ROWS)
            n = min(rows - done, TAIL_PART_ROWS - lo)
            wp_ref[dst + done:dst + done + n, :] = (
                w_refs[part][lo:lo + n, :].astype(BF16))
            done += n
    for dst, rows in TAIL_ZERO_ROWS:
        wp_ref[dst:dst + rows, :] = jnp.zeros((rows, wp_ref.shape[1]), BF16)

    acc_ref[...] += lax.dot_general(x_ref[...], wp_ref[...], _NT,
                                    preferred_element_type=F32)

    @pl.when(kk == pl.num_programs(1) - 1)
    def _():
        o_ref[...] = acc_ref[...].astype(o_ref.dtype)


def _in_proj_tail(x, wt, layer, tm=PROJ_TM, tk=TAIL_TK):
    m, k = x.shape
    assert m % tm == 0 and k % tk == 0 and D_MAIN % TAIL_PART_ROWS == 0
    assert wt.shape[1] == D_MAIN + TAIL_PARTS * TAIL_PART_ROWS, wt.shape
    first = D_MAIN // TAIL_PART_ROWS
    w_specs = [pl.BlockSpec((None, TAIL_PART_ROWS, tk),
                            lambda i, kk, p=p: (layer, first + p, kk))
               for p in range(TAIL_PARTS)]
    return pl.pallas_call(
        _in_proj_tail_kernel,
        grid=(m // tm, k // tk),
        in_specs=[pl.BlockSpec((tm, tk), lambda i, kk: (i, kk))] + w_specs,
        out_specs=pl.BlockSpec((tm, D_TAIL), lambda i, kk: (i, 0)),
        out_shape=jax.ShapeDtypeStruct((m, D_TAIL), BF16),
        scratch_shapes=[pltpu.VMEM((D_TAIL, tk), BF16),
                        pltpu.VMEM((tm, D_TAIL), F32)],
        compiler_params=_params(("parallel", "arbitrary")),
        name="in_proj_tail",
    )(x, *([wt] * TAIL_PARTS))


def _matmul_residual_kernel(x_ref, w_ref, r_ref, o_ref):
    o_ref[...] = r_ref[...] + jnp.dot(x_ref[...], w_ref[...].astype(BF16),
                                      preferred_element_type=F32)


def _matmul_residual(x, w, layer, res, tm=PROJ_TM, tn=PROJ_TN):
    m, k = x.shape
    n = w.shape[2]
    assert m % tm == 0 and n % tn == 0 and w.shape[1] == k, (m, k, n)
    return pl.pallas_call(
        _matmul_residual_kernel,
        grid=(m // tm, n // tn),
        in_specs=[pl.BlockSpec((tm, k), lambda i, j: (i, 0)),
                  pl.BlockSpec((None, k, tn), lambda i, j: (layer, 0, j)),
                  pl.BlockSpec((tm, tn), lambda i, j: (i, j))],
        out_specs=pl.BlockSpec((tm, tn), lambda i, j: (i, j)),
        out_shape=jax.ShapeDtypeStruct((m, n), F32),
        compiler_params=_params(("parallel", "parallel")),
        name="out_proj",
    )(x, w, res)


def _gmlp_kernel(u_ref, v_ref, z_ref, gv_ref, w_ref, bias_ref, go_ref, o_ref,
                 y_ref):
    tm = u_ref.shape[0]
    t_idx = lax.broadcasted_iota(jnp.int32, (CHUNK, CHUNK), 0)
    s_idx = lax.broadcasted_iota(jnp.int32, (CHUNK, CHUNK), 1)
    causal = s_idx <= t_idx

    def chunk(c, carry):
        rows = pl.ds(pl.multiple_of(c * CHUNK, CHUNK), CHUNK)
        ssq = jnp.zeros((CHUNK, 1), F32)
        for g in range(A_GROUPS):
            cols = slice(g * A_CH, (g + 1) * A_CH)
            vg = _gelu(v_ref[rows, cols].astype(F32))
            vn = vg * _rms_scale(vg, A_CH) * gv_ref[:, cols]
            w = jnp.where(causal, w_ref[g], 0.0).astype(BF16)
            sv = jnp.dot(w, vn.astype(BF16), preferred_element_type=F32)
            sv = sv + bias_ref[:, cols]
            y = _gelu(u_ref[rows, cols].astype(F32)) * sv
            ssq = ssq + jnp.sum(y * y, axis=-1, keepdims=True)
            y_ref[:, cols] = y
        r = lax.rsqrt(ssq * (1.0 / A_WIDTH) + EPS)
        z = z_ref[rows, :].astype(F32)
        o_ref[rows, :] = (y_ref[...] * r * go_ref[...] * _silu(z)).astype(o_ref.dtype)
        return carry

    lax.fori_loop(0, tm // CHUNK, chunk, 0)


def _gmlp(proj, g_v, w_s, b_s, g_out_a, tm=MIXER_TM):
    m = proj.shape[0]
    assert m % tm == 0 and tm % CHUNK == 0, (m, tm)
    bias = jnp.repeat(b_s.T, A_CH, axis=1)
    col = lambda off: (lambda i: (i, off // A_WIDTH))
    return pl.pallas_call(
        _gmlp_kernel,
        grid=(m // tm,),
        in_specs=[pl.BlockSpec((tm, A_WIDTH), col(OFF_U)),
                  pl.BlockSpec((tm, A_WIDTH), col(OFF_V)),
                  pl.BlockSpec((tm, A_WIDTH), col(OFF_ZA)),
                  pl.BlockSpec((1, A_WIDTH), lambda i: (0, 0)),
                  pl.BlockSpec((A_GROUPS, CHUNK, CHUNK), lambda i: (0, 0, 0)),
                  pl.BlockSpec((CHUNK, A_WIDTH), lambda i: (0, 0)),
                  pl.BlockSpec((1, A_WIDTH), lambda i: (0, 0))],
        out_specs=pl.BlockSpec((tm, A_WIDTH), lambda i: (i, 0)),
        out_shape=jax.ShapeDtypeStruct((m, A_WIDTH), BF16),
        scratch_shapes=[pltpu.VMEM((CHUNK, A_WIDTH), F32)],
        compiler_params=_params(("parallel",)),
        name="gmlp",
    )(proj, proj, proj, g_v.reshape(1, A_WIDTH), w_s, bias,
      g_out_a.reshape(1, A_WIDTH))


SB_T = 128
SB_SUB = 128
SB_NSUB = 3
SB_W = SB_NSUB * SB_SUB
SB_HG = 6
SB_STAGE = 6
SB_ZERO_WEIGHT_LOG = -105.0


def _sb_kernel(q_ref, k_ref, v_ref, o_ref, acc_ref, carry_ref):
    t, w, sub = SB_T, SB_W, SB_SUB
    i = pl.program_id(2)
    scale = HEAD_DIM ** -0.5
    q_pos = i * t + lax.broadcasted_iota(jnp.int32, (t, w), 0)
    k_off = lax.broadcasted_iota(jnp.int32, (t, w), 1)
    row = lax.broadcasted_iota(jnp.int32, (2 * sub, sub), 0) % sub
    col = lax.broadcasted_iota(jnp.int32, (2 * sub, sub), 1)
    neg_suffix = jnp.where(row > col, -1.0, 0.0).astype(BF16)

    acc_ref[...] = jnp.zeros_like(acc_ref)
    carry_ref[...] = jnp.zeros_like(carry_ref)

    def window(start, bound):
        valid = (start + k_off) < jnp.minimum(q_pos, bound)
        ks = pl.ds(pl.multiple_of(start, sub), w)
        worst = None
        for h0 in range(0, SB_HG, SB_STAGE):
            worst = stage(range(h0, h0 + SB_STAGE), valid, ks, worst)
        return worst

    def stage(heads, valid, ks, worst):
        log_beta, softplus, pieces = {}, {}, []
        for h in heads:
            hc = slice(h * HEAD_DIM, (h + 1) * HEAD_DIM)
            z = lax.dot_general(q_ref[0, :, hc], k_ref[0, ks, hc],
                                (((1,), (1,)), ((), ())),
                                preferred_element_type=F32) * scale
            sp = jnp.maximum(z, 0.0) + jnp.log(1.0 + jnp.exp(-jnp.abs(z)))
            log_beta[h] = z - sp
            sp = jnp.where(valid, sp, 0.0)
            softplus[h] = sp
            hi = sp.astype(BF16)
            lo = (sp - hi.astype(F32)).astype(BF16)
            for c in range(SB_NSUB):
                pieces.append(jnp.concatenate(
                    [hi[:, c * sub:(c + 1) * sub], lo[:, c * sub:(c + 1) * sub]],
                    axis=1))
        local = jnp.dot(jnp.concatenate(pieces, axis=0), neg_suffix,
                        preferred_element_type=F32)
        for n, h in enumerate(heads):
            hc = slice(h * HEAD_DIM, (h + 1) * HEAD_DIM)
            sp = softplus[h]
            running = carry_ref[h]
            after = [None] * SB_NSUB
            for c in reversed(range(SB_NSUB)):
                base = (n * SB_NSUB + c) * t
                after[c] = local[base:base + t] + running
                running = running - jnp.sum(sp[:, c * sub:(c + 1) * sub],
                                            axis=-1, keepdims=True)
            a = jnp.where(valid,
                          jnp.exp(log_beta[h] + jnp.concatenate(after, axis=1)),
                          0.0)
            acc_ref[h] += jnp.dot(a.astype(BF16), v_ref[0, ks, hc],
                                  preferred_element_type=F32)
            carry_ref[h] = running
            top = jnp.max(running)
            worst = top if worst is None else jnp.maximum(worst, top)
        return worst

    first = jnp.maximum(i + 1 - SB_NSUB, 0) * sub
    worst = window(first, jnp.int32(2 ** 30))

    def more(state):
        start, worst = state
        return jnp.logical_and(start > 0, worst > SB_ZERO_WEIGHT_LOG)

    def step(state):
        start, _ = state
        nxt = jnp.maximum(start - w, 0)
        return nxt, window(nxt, start)

    lax.while_loop(more, step, (first, worst))
    for h in range(SB_HG):
        o_ref[0, :, h * HEAD_DIM:(h + 1) * HEAD_DIM] = acc_ref[h].astype(o_ref.dtype)


def _stick_breaking(proj3):
    b, s, _ = proj3.shape
    assert s % SB_T == 0 and s >= SB_W and SB_T == SB_SUB and HEADS % SB_HG == 0
    t, gw = SB_T, SB_HG * HEAD_DIM
    blk = lambda off: off // gw
    return pl.pallas_call(
        _sb_kernel,
        grid=(b, HEADS // SB_HG, s // t),
        in_specs=[
            pl.BlockSpec((1, t, gw), lambda bi, g, i: (bi, i, blk(OFF_QB) + g)),
            pl.BlockSpec((1, s, gw), lambda bi, g, i: (bi, 0, blk(OFF_KB) + g)),
            pl.BlockSpec((1, s, gw), lambda bi, g, i: (bi, 0, blk(OFF_VB) + g)),
        ],
        out_specs=pl.BlockSpec((1, t, gw), lambda bi, g, i: (bi, i, g)),
        out_shape=jax.ShapeDtypeStruct((b, s, B_WIDTH), BF16),
        scratch_shapes=[pltpu.VMEM((SB_HG, t, HEAD_DIM), F32),
                        pltpu.VMEM((SB_HG, t, 1), F32)],
        compiler_params=_params(("parallel", "parallel", "arbitrary")),
        name="stick_breaking",
    )(proj3, proj3, proj3)


def _rope_tile(x, cos_ref, sin_ref):
    return x * cos_ref[...] + pltpu.roll(x, ROPE_X2, axis=1) * sin_ref[...]


def _mla_q_kernel(cq_ref, g_ref, w_ref, cos_ref, sin_ref, o_ref):
    scale = (C_NOPE + C_ROPE) ** -0.5 * math.log2(math.e)
    x = cq_ref[...].astype(F32)
    xn = (x * _rms_scale(x, Q_LORA) * g_ref[...]).astype(BF16)
    q = jnp.dot(xn, w_ref[...], preferred_element_type=F32)
    for h in range(HEADS):
        lo = h * C_QK
        o_ref[:, lo:lo + C_NOPE] = (q[:, lo:lo + C_NOPE] * scale).astype(o_ref.dtype)
        rope = _rope_tile(q[:, lo + C_NOPE:lo + C_QK], cos_ref, sin_ref)
        o_ref[:, lo + C_NOPE:lo + C_QK] = (rope * scale).astype(o_ref.dtype)


VT_ONES = 16
VT_ROWS = HEAD_DIM + VT_ONES


def _mla_kv_kernel(ckv_ref, kr_ref, g_ref, wk_ref, wvt_ref, cos_ref, sin_ref,
                   k_ref, vt_ref):
    tm = ckv_ref.shape[0]
    x = ckv_ref[...].astype(F32)
    xn = (x * _rms_scale(x, KV_LORA) * g_ref[...]).astype(BF16)
    kn = jnp.dot(xn, wk_ref[...], preferred_element_type=F32).astype(k_ref.dtype)
    kr = _rope_tile(kr_ref[...].astype(F32), cos_ref, sin_ref).astype(k_ref.dtype)
    for h in range(HEADS):
        k_ref[:, h * C_QK:h * C_QK + C_NOPE] = kn[:, h * C_NOPE:(h + 1) * C_NOPE]
        k_ref[:, h * C_QK + C_NOPE:(h + 1) * C_QK] = kr
    vt = lax.dot_general(wvt_ref[...], xn, _NT, preferred_element_type=F32)
    for h in range(HEADS):
        vt_ref[h * VT_ROWS:h * VT_ROWS + HEAD_DIM, :] = (
            vt[h * HEAD_DIM:(h + 1) * HEAD_DIM, :].astype(vt_ref.dtype))
        vt_ref[h * VT_ROWS + HEAD_DIM:(h + 1) * VT_ROWS, :] = jnp.ones(
            (VT_ONES, tm), vt_ref.dtype)


def _mla_project(proj, g_q, g_kv, w_uq_p, w_k_p, w_vt_p, tabs, b, s,
                 tm=MIXER_TM):
    m = proj.shape[0]
    assert m == b * s and s % tm == 0, (m, b, s, tm)
    tiles = s // tm
    tab_spec = pl.BlockSpec((tm, ROPE_TILE), lambda i: (i, 0))
    q_cat = pl.pallas_call(
        _mla_q_kernel,
        grid=(m // tm,),
        in_specs=[pl.BlockSpec((tm, Q_LORA), lambda i: (i, TAIL_CQ // Q_LORA)),
                  pl.BlockSpec((1, Q_LORA), lambda i: (0, 0)),
                  pl.BlockSpec((Q_LORA, HEADS * C_QK), lambda i: (0, 0)),
                  tab_spec, tab_spec],
        out_specs=pl.BlockSpec((tm, HEADS * C_QK), lambda i: (i, 0)),
        out_shape=jax.ShapeDtypeStruct((m, HEADS * C_QK), BF16),
        compiler_params=_params(("parallel",)),
        name="mla_q",
    )(proj, g_q.reshape(1, Q_LORA), w_uq_p, *tabs)
    k_cat, vt = pl.pallas_call(
        _mla_kv_kernel,
        grid=(m // tm,),
        in_specs=[pl.BlockSpec((tm, KV_LORA), lambda i: (i, TAIL_CKV // KV_LORA)),
                  pl.BlockSpec((tm, ROPE_TILE), lambda i: (i, TAIL_KR // ROPE_TILE)),
                  pl.BlockSpec((1, KV_LORA), lambda i: (0, 0)),
                  pl.BlockSpec((KV_LORA, HEADS * C_NOPE), lambda i: (0, 0)),
                  pl.BlockSpec((C_WIDTH, KV_LORA), lambda i: (0, 0)),
                  tab_spec, tab_spec],
        out_specs=[pl.BlockSpec((tm, HEADS * C_QK), lambda i: (i, 0)),
                   pl.BlockSpec((None, HEADS * VT_ROWS, tm),
                                lambda i: (i // tiles, 0, i % tiles))],
        out_shape=[jax.ShapeDtypeStruct((m, HEADS * C_QK), BF16),
                   jax.ShapeDtypeStruct((b, HEADS * VT_ROWS, s), BF16)],
        compiler_params=_params(("parallel",)),
        name="mla_kv",
    )(proj, proj, g_kv.reshape(1, KV_LORA), w_k_p, w_vt_p, *tabs)
    return q_cat, k_cat, vt


MLA_T = 256
MLA_W = 512
MLA_HG = 6


def _mla_attn_kernel(q_ref, k_ref, vt_ref, o_ref, m_ref, acc_ref, s_ref,
                     smax_ref):
    t, w = MLA_T, MLA_W
    i = pl.program_id(2)
    k_off = lax.broadcasted_iota(jnp.int32, (w, t), 0)
    q_pos = i * t + lax.broadcasted_iota(jnp.int32, (w, t), 1)

    m_ref[...] = jnp.full_like(m_ref, -jnp.inf)
    acc_ref[...] = jnp.zeros_like(acc_ref)

    def scores(j, slot):
        ks = pl.ds(pl.multiple_of(j * w, w), w)
        for h in range(MLA_HG):
            s = lax.dot_general(k_ref[0, ks, h * C_QK:(h + 1) * C_QK],
                                q_ref[0, :, h * C_QK:(h + 1) * C_QK], _NT,
                                preferred_element_type=F32)
            s_ref[slot, h] = s
            smax_ref[slot, h] = jnp.max(s, axis=0, keepdims=True)

    def softmax_pv(j, slot, masked):
        start = pl.multiple_of(j * w, w)
        ks = pl.ds(start, w)
        for h in range(MLA_HG):
            s = s_ref[slot, h]
            if masked:
                s = jnp.where((start + k_off) <= q_pos, s, -jnp.inf)
                s_max = jnp.max(s, axis=0, keepdims=True)
            else:
                s_max = smax_ref[slot, h]
            m_old = m_ref[h]
            m_new = jnp.maximum(m_old, s_max)
            alpha = jnp.exp2(m_old - m_new)
            p = jnp.exp2(s - m_new)
            acc_ref[h] = alpha * acc_ref[h] + jnp.dot(
                vt_ref[0, h * VT_ROWS:(h + 1) * VT_ROWS, ks], p.astype(BF16),
                preferred_element_type=F32)
            m_ref[h] = m_new

    last = (i * t) // w

    scores(0, 0)

    def pair(jj, carry):
        j = 2 * jj
        scores(j + 1, 1)
        softmax_pv(j, 0, False)
        scores(j + 2, 0)
        softmax_pv(j + 1, 1, False)
        return carry

    lax.fori_loop(0, last // 2, pair, 0)

    @pl.when(last % 2 == 0)
    def _():
        softmax_pv(last, 0, True)

    @pl.when(last % 2 == 1)
    def _():
        scores(last, 1)
        softmax_pv(last - 1, 0, False)
        softmax_pv(last, 1, True)
    for h in range(MLA_HG):
        acc = acc_ref[h]
        out_t = acc[:HEAD_DIM] / acc[HEAD_DIM:HEAD_DIM + 1]
        o_ref[0, :, h * HEAD_DIM:(h + 1) * HEAD_DIM] = out_t.T.astype(o_ref.dtype)


def _mla_attention(q_cat, k_cat, vt, b, s):
    t, hg = MLA_T, MLA_HG
    assert MLA_W % t == 0 and s % MLA_W == 0 and HEADS % hg == 0
    q3 = q_cat.reshape(b, s, HEADS * C_QK)
    k3 = k_cat.reshape(b, s, HEADS * C_QK)
    return pl.pallas_call(
        _mla_attn_kernel,
        grid=(b, HEADS // hg, s // t),
        in_specs=[pl.BlockSpec((1, t, hg * C_QK), lambda bi, g, i: (bi, i, g)),
                  pl.BlockSpec((1, s, hg * C_QK), lambda bi, g, i: (bi, 0, g)),
                  pl.BlockSpec((1, hg * VT_ROWS, s), lambda bi, g, i: (bi, g, 0))],
        out_specs=pl.BlockSpec((1, t, hg * HEAD_DIM), lambda bi, g, i: (bi, i, g)),
        out_shape=jax.ShapeDtypeStruct((b, s, C_WIDTH), BF16),
        scratch_shapes=[pltpu.VMEM((hg, 1, t), F32),
                        pltpu.VMEM((hg, VT_ROWS, t), F32),
                        pltpu.VMEM((2, hg, MLA_W, t), F32),
                        pltpu.VMEM((2, hg, 1, t), F32)],
        compiler_params=_params(("parallel", "parallel", "arbitrary")),
        name="mla_attention",
    )(q3, k3, vt)


GATE_ROWS = 32


def _gate_kernel(ya_ref, yb_ref, zb_ref, yc_ref, zc_ref, g_ref, o_ref):
    o_ref[:, :A_WIDTH] = ya_ref[...]

    def chunk(c, carry):
        rows = pl.ds(pl.multiple_of(c * GATE_ROWS, GATE_ROWS), GATE_ROWS)
        lo = A_WIDTH
        for y_ref, z_ref, width in ((yb_ref, zb_ref, B_WIDTH),
                                    (yc_ref, zc_ref, C_WIDTH)):
            y = y_ref[rows, :].astype(F32)
            z = z_ref[rows, :].astype(F32)
            g = g_ref[:, lo:lo + width]
            o_ref[rows, lo:lo + width] = (y * _rms_scale(y, width) * g
                                          * _silu(z)).astype(o_ref.dtype)
            lo += width
        return carry

    lax.fori_loop(0, ya_ref.shape[0] // GATE_ROWS, chunk, 0)


def _gate(ya, yb, yc, main, tail, g_out, tm=MIXER_TM):
    m = ya.shape[0]
    assert m % tm == 0 and tm % GATE_ROWS == 0, (m, tm)
    return pl.pallas_call(
        _gate_kernel,
        grid=(m // tm,),
        in_specs=[pl.BlockSpec((tm, A_WIDTH), lambda i: (i, 0)),
                  pl.BlockSpec((tm, B_WIDTH), lambda i: (i, 0)),
                  pl.BlockSpec((tm, B_WIDTH), lambda i: (i, OFF_ZB // B_WIDTH)),
                  pl.BlockSpec((tm, C_WIDTH), lambda i: (i, 0)),
                  pl.BlockSpec((tm, C_WIDTH), lambda i: (i, TAIL_ZC // C_WIDTH)),
                  pl.BlockSpec((1, D_MODEL), lambda i: (0, 0))],
        out_specs=pl.BlockSpec((tm, D_MODEL), lambda i: (i, 0)),
        out_shape=jax.ShapeDtypeStruct((m, D_MODEL), BF16),
        compiler_params=_params(("parallel",)),
        name="gate",
    )(ya, yb, main, yc, tail, g_out.reshape(1, D_MODEL))


def _prep_w_uq(w):
    w = w.reshape(Q_LORA, HEADS, C_NOPE + C_ROPE)
    gap = jnp.zeros((Q_LORA, HEADS, ROPE_X2 - ROPE_HALF), w.dtype)
    w = jnp.concatenate([w[:, :, :C_NOPE + ROPE_HALF], gap,
                         w[:, :, C_NOPE + ROPE_HALF:], gap], axis=2)
    return w.reshape(Q_LORA, HEADS * C_QK).astype(BF16)


def _prep_w_ukv(w):
    w = w.reshape(KV_LORA, HEADS, C_NOPE + HEAD_DIM)
    wk = w[:, :, :C_NOPE].reshape(KV_LORA, HEADS * C_NOPE)
    wv_t = w[:, :, C_NOPE:].reshape(KV_LORA, C_WIDTH).T
    return wk.astype(BF16), wv_t.astype(BF16)


def _rope_tables(positions):
    inv_freq = 1.0 / (ROPE_THETA ** (jnp.arange(0, C_ROPE, 2, dtype=F32) / C_ROPE))
    ang = positions.astype(F32).reshape(-1, 1) * inv_freq
    cos, sin = jnp.cos(ang), jnp.sin(ang)
    gap = jnp.zeros((ang.shape[0], ROPE_X2 - ROPE_HALF), F32)
    return (jnp.concatenate([cos, gap, cos, gap], axis=1),
            jnp.concatenate([-sin, gap, sin, gap], axis=1))


def kernel(x, positions, g_pre, w_in, a_g_v, a_w_s, a_b_s, c_g_q, c_g_kv,
           c_w_uq, c_w_ukv, g_out, w_out, g_final):
    b, s, d = x.shape
    m = b * s
    depth = w_in.shape[0]
    tabs = _rope_tables(positions)
    w_in_t = jnp.swapaxes(w_in, 1, 2)
    h = x.reshape(m, d)
    for l in range(depth):
        hn = _rmsnorm(h, g_pre[l], BF16)
        main = _in_proj_main(hn, w_in_t, l)
        tail = _in_proj_tail(hn, w_in_t, l)
        ya = _gmlp(main, a_g_v[l], a_w_s[l], a_b_s[l], g_out[l, :A_WIDTH])
        yb = _stick_breaking(main.reshape(b, s, D_MAIN)).reshape(m, B_WIDTH)
        w_k_p, w_vt_p = _prep_w_ukv(c_w_ukv[l])
        q_cat, k_cat, vt = _mla_project(tail, c_g_q[l], c_g_kv[l],
                                        _prep_w_uq(c_w_uq[l]), w_k_p, w_vt_p,
                                        tabs, b, s)
        yc = _mla_attention(q_cat, k_cat, vt, b, s).reshape(m, C_WIDTH)
        y = _gate(ya, yb, yc, main, tail, g_out[l])
        h = _matmul_residual(y, w_out, l, h)
    return _rmsnorm(h, g_final, F32).reshape(b, s, d)
```

```python
import math

import jax
import jax.numpy as jnp
from jax import lax
from jax.experimental import pallas as pl
from jax.experimental.pallas import tpu as pltpu

F32 = jnp.float32
BF16 = jnp.bfloat16

D_MODEL = 4096
A_GROUPS = 8
A_CH = 128
A_WIDTH = A_GROUPS * A_CH
CHUNK = 128
HEADS = 12
HEAD_DIM = 128
B_WIDTH = HEADS * HEAD_DIM
C_NOPE = 128
C_ROPE = 64
C_WIDTH = HEADS * HEAD_DIM
Q_LORA = 768
KV_LORA = 512
ROPE_THETA = 10000.0
EPS = 1e-6

C_QK = 256
ROPE_HALF = C_ROPE // 2
ROPE_TILE = 128
ROPE_X2 = ROPE_TILE // 2

OFF_U = 0
OFF_V = OFF_U + A_WIDTH
OFF_ZA = OFF_V + A_WIDTH
OFF_QB = OFF_ZA + A_WIDTH
OFF_KB = OFF_QB + B_WIDTH
OFF_VB = OFF_KB + B_WIDTH
OFF_ZB = OFF_VB + B_WIDTH
D_MAIN = OFF_ZB + B_WIDTH
REF_CQ = D_MAIN
REF_CKV = REF_CQ + Q_LORA
REF_KR = REF_CKV + KV_LORA
REF_ZC = REF_KR + C_ROPE
TAIL_ZC = 0
TAIL_CKV = TAIL_ZC + C_WIDTH
TAIL_KR = TAIL_CKV + KV_LORA
TAIL_CQ = TAIL_KR + C_QK
D_TAIL = TAIL_CQ + Q_LORA

V7X_VMEM_BYTES = 64 * 1024 * 1024
VMEM_LIMIT_BYTES = V7X_VMEM_BYTES * 7 // 8

NORM_TM = 512
PROJ_TM = 1024
PROJ_TN = 512
MAIN_TN = 768
TAIL_TK = 512
MIXER_TM = 1024


def _params(sem):
    return pltpu.CompilerParams(dimension_semantics=sem,
                                vmem_limit_bytes=VMEM_LIMIT_BYTES)


def _gelu(x):
    c = math.sqrt(2.0 / math.pi)
    return x * (0.5 * (1.0 + jnp.tanh(c * (x + 0.044715 * (x * x * x)))))


def _silu(z):
    return z / (1.0 + jnp.exp(-z))


def _rms_scale(x, width):
    return lax.rsqrt(jnp.sum(x * x, axis=-1, keepdims=True) * (1.0 / width) + EPS)


def _rmsnorm_kernel(x_ref, g_ref, o_ref):
    x = x_ref[...]
    o_ref[...] = (x * _rms_scale(x, x.shape[-1]) * g_ref[...]).astype(o_ref.dtype)


def _row_param(p, layer, lo=0, width=None):
    stacked = p.reshape(p.shape[0], 1, -1)
    width = stacked.shape[2] if width is None else width
    spec = pl.BlockSpec((None, 1, width), lambda *_: (layer, 0, lo // width))
    return stacked, spec


def _layer_param(p, layer):
    zeros = (0,) * (p.ndim - 1)
    return p, pl.BlockSpec((None,) + p.shape[1:], lambda *_: (layer,) + zeros)


def _rmsnorm(x, g, layer, out_dtype, tm=NORM_TM):
    m, d = x.shape
    assert m % tm == 0, (m, tm)
    g, g_spec = _row_param(g, layer)
    return pl.pallas_call(
        _rmsnorm_kernel,
        grid=(m // tm,),
        in_specs=[pl.BlockSpec((tm, d), lambda i: (i, 0)), g_spec],
        out_specs=pl.BlockSpec((tm, d), lambda i: (i, 0)),
        out_shape=jax.ShapeDtypeStruct((m, d), out_dtype),
        compiler_params=_params(("parallel",)),
        name="rmsnorm",
    )(x, g)


_NT = (((1,), (1,)), ((), ()))


def _in_proj_main_kernel(x_ref, wt_ref, o_ref):
    o_ref[...] = lax.dot_general(x_ref[...], wt_ref[...].astype(BF16), _NT,
                                 preferred_element_type=F32).astype(o_ref.dtype)


def _in_proj_main(x, wt, layer, tm=PROJ_TM, tn=MAIN_TN):
    m, k = x.shape
    assert m % tm == 0 and D_MAIN % tn == 0 and wt.shape[2] == k, (m, tm, tn)
    return pl.pallas_call(
        _in_proj_main_kernel,
        grid=(m // tm, D_MAIN // tn),
        in_specs=[pl.BlockSpec((tm, k), lambda i, j: (i, 0)),
                  pl.BlockSpec((None, tn, k), lambda i, j: (layer, j, 0))],
        out_specs=pl.BlockSpec((tm, tn), lambda i, j: (i, j)),
        out_shape=jax.ShapeDtypeStruct((m, D_MAIN), BF16),
        compiler_params=_params(("parallel", "parallel")),
        name="in_proj_main",
    )(x, wt)


TAIL_PART_ROWS = 576
TAIL_PARTS = 5
TAIL_COPIES = (
    (TAIL_ZC, REF_ZC - D_MAIN, C_WIDTH),
    (TAIL_CKV, REF_CKV - D_MAIN, KV_LORA),
    (TAIL_KR, REF_KR - D_MAIN, ROPE_HALF),
    (TAIL_KR + ROPE_X2, REF_KR - D_MAIN + ROPE_HALF, ROPE_HALF),
    (TAIL_CQ, REF_CQ - D_MAIN, Q_LORA),
)
TAIL_ZERO_ROWS = ((TAIL_KR + ROPE_HALF, ROPE_X2 - ROPE_HALF),
                  (TAIL_KR + ROPE_X2 + ROPE_HALF, ROPE_X2 - ROPE_HALF),
                  (TAIL_KR + ROPE_TILE, TAIL_CQ - TAIL_KR - ROPE_TILE))


def _in_proj_tail_kernel(x_ref, *refs):
    w_refs = refs[:TAIL_PARTS]
    o_ref, wp_ref, acc_ref = refs[TAIL_PARTS:]
    kk = pl.program_id(1)

    @pl.when(kk == 0)
    def _():
        acc_ref[...] = jnp.zeros_like(acc_ref)

    for dst, src, rows in TAIL_COPIES:
        done = 0
        while done < rows:
            part, lo = divmod(src + done, TAIL_PART_ROWS)
            n = min(rows - done, TAIL_PART_ROWS - lo)
            wp_ref[dst + done:dst + done + n, :] = (
                w_refs[part][lo:lo + n, :].astype(BF16))
            done += n
    for dst, rows in TAIL_ZERO_ROWS:
        wp_ref[dst:dst + rows, :] = jnp.zeros((rows, wp_ref.shape[1]), BF16)

    acc_ref[...] += lax.dot_general(x_ref[...], wp_ref[...], _NT,
                                    preferred_element_type=F32)

    @pl.when(kk == pl.num_programs(1) - 1)
    def _():
        o_ref[...] = acc_ref[...].astype(o_ref.dtype)


def _in_proj_tail(x, wt, layer, tm=PROJ_TM, tk=TAIL_TK):
    m, k = x.shape
    assert m % tm == 0 and k % tk == 0 and D_MAIN % TAIL_PART_ROWS == 0
    assert wt.shape[1] == D_MAIN + TAIL_PARTS * TAIL_PART_ROWS, wt.shape
    first = D_MAIN // TAIL_PART_ROWS
    w_specs = [pl.BlockSpec((None, TAIL_PART_ROWS, tk),
                            lambda i, kk, p=p: (layer, first + p, kk))
               for p in range(TAIL_PARTS)]
    return pl.pallas_call(
        _in_proj_tail_kernel,
        grid=(m // tm, k // tk),
        in_specs=[pl.BlockSpec((tm, tk), lambda i, kk: (i, kk))] + w_specs,
        out_specs=pl.BlockSpec((tm, D_TAIL), lambda i, kk: (i, 0)),
        out_shape=jax.ShapeDtypeStruct((m, D_TAIL), BF16),
        scratch_shapes=[pltpu.VMEM((D_TAIL, tk), BF16),
                        pltpu.VMEM((tm, D_TAIL), F32)],
        compiler_params=_params(("parallel", "arbitrary")),
        name="in_proj_tail",
    )(x, *([wt] * TAIL_PARTS))


def _matmul_residual_kernel(x_ref, w_ref, r_ref, o_ref):
    o_ref[...] = r_ref[...] + jnp.dot(x_ref[...], w_ref[...].astype(BF16),
                                      preferred_element_type=F32)


def _matmul_residual(x, w, layer, res, tm=PROJ_TM, tn=PROJ_TN):
    m, k = x.shape
    n = w.shape[2]
    assert m % tm == 0 and n % tn == 0 and w.shape[1] == k, (m, k, n)
    return pl.pallas_call(
        _matmul_residual_kernel,
        grid=(m // tm, n // tn),
        in_specs=[pl.BlockSpec((tm, k), lambda i, j: (i, 0)),
                  pl.BlockSpec((None, k, tn), lambda i, j: (layer, 0, j)),
                  pl.BlockSpec((tm, tn), lambda i, j: (i, j))],
        out_specs=pl.BlockSpec((tm, tn), lambda i, j: (i, j)),
        out_shape=jax.ShapeDtypeStruct((m, n), F32),
        compiler_params=_params(("parallel", "parallel")),
        name="out_proj",
    )(x, w, res)


def _gmlp_kernel(u_ref, v_ref, z_ref, gv_ref, w_ref, bias_ref, go_ref, o_ref,
                 y_ref):
    tm = u_ref.shape[0]
    t_idx = lax.broadcasted_iota(jnp.int32, (CHUNK, CHUNK), 0)
    s_idx = lax.broadcasted_iota(jnp.int32, (CHUNK, CHUNK), 1)
    causal = s_idx <= t_idx

    def chunk(c, carry):
        rows = pl.ds(pl.multiple_of(c * CHUNK, CHUNK), CHUNK)
        ssq = jnp.zeros((CHUNK, 1), F32)
        for g in range(A_GROUPS):
            cols = slice(g * A_CH, (g + 1) * A_CH)
            vg = _gelu(v_ref[rows, cols].astype(F32))
            vn = vg * _rms_scale(vg, A_CH) * gv_ref[:, cols]
            w = jnp.where(causal, w_ref[g], 0.0).astype(BF16)
            sv = jnp.dot(w, vn.astype(BF16), preferred_element_type=F32)
            sv = sv + bias_ref[:, cols]
            y = _gelu(u_ref[rows, cols].astype(F32)) * sv
            ssq = ssq + jnp.sum(y * y, axis=-1, keepdims=True)
            y_ref[:, cols] = y
        r = lax.rsqrt(ssq * (1.0 / A_WIDTH) + EPS)
        z = z_ref[rows, :].astype(F32)
        o_ref[rows, :] = (y_ref[...] * r * go_ref[...] * _silu(z)).astype(o_ref.dtype)
        return carry

    lax.fori_loop(0, tm // CHUNK, chunk, 0)


def _gmlp(proj, g_v, w_s, bias, g_out, layer, tm=MIXER_TM):
    m = proj.shape[0]
    assert m % tm == 0 and tm % CHUNK == 0, (m, tm)
    col = lambda off: (lambda i: (i, off // A_WIDTH))
    g_v, g_v_spec = _row_param(g_v, layer)
    w_s, w_s_spec = _layer_param(w_s, layer)
    bias, bias_spec = _layer_param(bias, layer)
    g_out, g_out_spec = _row_param(g_out, layer, 0, A_WIDTH)
    return pl.pallas_call(
        _gmlp_kernel,
        grid=(m // tm,),
        in_specs=[pl.BlockSpec((tm, A_WIDTH), col(OFF_U)),
                  pl.BlockSpec((tm, A_WIDTH), col(OFF_V)),
                  pl.BlockSpec((tm, A_WIDTH), col(OFF_ZA)),
                  g_v_spec, w_s_spec, bias_spec, g_out_spec],
        out_specs=pl.BlockSpec((tm, A_WIDTH), lambda i: (i, 0)),
        out_shape=jax.ShapeDtypeStruct((m, A_WIDTH), BF16),
        scratch_shapes=[pltpu.VMEM((CHUNK, A_WIDTH), F32)],
        compiler_params=_params(("parallel",)),
        name="gmlp",
    )(proj, proj, proj, g_v, w_s, bias, g_out)


SB_T = 128
SB_SUB = 128
SB_NSUB = 3
SB_W = SB_NSUB * SB_SUB
SB_HG = 6
SB_STAGE = 6
SB_ZERO_WEIGHT_LOG = -105.0


def _sb_kernel(q_ref, k_ref, v_ref, o_ref, acc_ref, carry_ref):
    t, w, sub = SB_T, SB_W, SB_SUB
    i = pl.program_id(2)
    scale = HEAD_DIM ** -0.5
    q_pos = i * t + lax.broadcasted_iota(jnp.int32, (t, w), 0)
    k_off = lax.broadcasted_iota(jnp.int32, (t, w), 1)
    row = lax.broadcasted_iota(jnp.int32, (2 * sub, sub), 0) % sub
    col = lax.broadcasted_iota(jnp.int32, (2 * sub, sub), 1)
    neg_suffix = jnp.where(row > col, -1.0, 0.0).astype(BF16)

    acc_ref[...] = jnp.zeros_like(acc_ref)
    carry_ref[...] = jnp.zeros_like(carry_ref)

    def window(start, bound):
        valid = (start + k_off) < jnp.minimum(q_pos, bound)
        ks = pl.ds(pl.multiple_of(start, sub), w)
        worst = None
        for h0 in range(0, SB_HG, SB_STAGE):
            worst = stage(range(h0, h0 + SB_STAGE), valid, ks, worst)
        return worst

    def stage(heads, valid, ks, worst):
        log_beta, softplus, pieces = {}, {}, []
        for h in heads:
            hc = slice(h * HEAD_DIM, (h + 1) * HEAD_DIM)
            z = lax.dot_general(q_ref[0, :, hc], k_ref[0, ks, hc],
                                (((1,), (1,)), ((), ())),
                                preferred_element_type=F32) * scale
            sp = jnp.maximum(z, 0.0) + jnp.log(1.0 + jnp.exp(-jnp.abs(z)))
            log_beta[h] = z - sp
            sp = jnp.where(valid, sp, 0.0)
            softplus[h] = sp
            hi = sp.astype(BF16)
            lo = (sp - hi.astype(F32)).astype(BF16)
            for c in range(SB_NSUB):
                pieces.append(jnp.concatenate(
                    [hi[:, c * sub:(c + 1) * sub], lo[:, c * sub:(c + 1) * sub]],
                    axis=1))
        local = jnp.dot(jnp.concatenate(pieces, axis=0), neg_suffix,
                        preferred_element_type=F32)
        for n, h in enumerate(heads):
            hc = slice(h * HEAD_DIM, (h + 1) * HEAD_DIM)
            sp = softplus[h]
            running = carry_ref[h]
            after = [None] * SB_NSUB
            for c in reversed(range(SB_NSUB)):
                base = (n * SB_NSUB + c) * t
                after[c] = local[base:base + t] + running
                running = running - jnp.sum(sp[:, c * sub:(c + 1) * sub],
                                            axis=-1, keepdims=True)
            a = jnp.where(valid,
                          jnp.exp(log_beta[h] + jnp.concatenate(after, axis=1)),
                          0.0)
            acc_ref[h] += jnp.dot(a.astype(BF16), v_ref[0, ks, hc],
                                  preferred_element_type=F32)
            carry_ref[h] = running
            top = jnp.max(running)
            worst = top if worst is None else jnp.maximum(worst, top)
        return worst

    first = jnp.maximum(i + 1 - SB_NSUB, 0) * sub
    worst = window(first, jnp.int32(2 ** 30))

    def more(state):
        start, worst = state
        return jnp.logical_and(start > 0, worst > SB_ZERO_WEIGHT_LOG)

    def step(state):
        start, _ = state
        nxt = jnp.maximum(start - w, 0)
        return nxt, window(nxt, start)

    lax.while_loop(more, step, (first, worst))
    for h in range(SB_HG):
        o_ref[0, :, h * HEAD_DIM:(h + 1) * HEAD_DIM] = acc_ref[h].astype(o_ref.dtype)


def _stick_breaking(proj3):
    b, s, _ = proj3.shape
    assert s % SB_T == 0 and s >= SB_W and SB_T == SB_SUB and HEADS % SB_HG == 0
    t, gw = SB_T, SB_HG * HEAD_DIM
    blk = lambda off: off // gw
    return pl.pallas_call(
        _sb_kernel,
        grid=(b, HEADS // SB_HG, s // t),
        in_specs=[
            pl.BlockSpec((1, t, gw), lambda bi, g, i: (bi, i, blk(OFF_QB) + g)),
            pl.BlockSpec((1, s, gw), lambda bi, g, i: (bi, 0, blk(OFF_KB) + g)),
            pl.BlockSpec((1, s, gw), lambda bi, g, i: (bi, 0, blk(OFF_VB) + g)),
        ],
        out_specs=pl.BlockSpec((1, t, gw), lambda bi, g, i: (bi, i, g)),
        out_shape=jax.ShapeDtypeStruct((b, s, B_WIDTH), BF16),
        scratch_shapes=[pltpu.VMEM((SB_HG, t, HEAD_DIM), F32),
                        pltpu.VMEM((SB_HG, t, 1), F32)],
        compiler_params=_params(("parallel", "parallel", "arbitrary")),
        name="stick_breaking",
    )(proj3, proj3, proj3)


def _rope_tile(x, cos_ref, sin_ref):
    return x * cos_ref[...] + pltpu.roll(x, ROPE_X2, axis=1) * sin_ref[...]


def _mla_q_kernel(cq_ref, g_ref, w_ref, cos_ref, sin_ref, o_ref):
    scale = (C_NOPE + C_ROPE) ** -0.5 * math.log2(math.e)
    x = cq_ref[...].astype(F32)
    xn = (x * _rms_scale(x, Q_LORA) * g_ref[...]).astype(BF16)
    q = jnp.dot(xn, w_ref[...], preferred_element_type=F32)
    for h in range(HEADS):
        lo = h * C_QK
        o_ref[:, lo:lo + C_NOPE] = (q[:, lo:lo + C_NOPE] * scale).astype(o_ref.dtype)
        rope = _rope_tile(q[:, lo + C_NOPE:lo + C_QK], cos_ref, sin_ref)
        o_ref[:, lo + C_NOPE:lo + C_QK] = (rope * scale).astype(o_ref.dtype)


VT_ONES = 16
VT_ROWS = HEAD_DIM + VT_ONES


def _mla_kv_kernel(ckv_ref, kr_ref, g_ref, wk_ref, wvt_ref, cos_ref, sin_ref,
                   k_ref, vt_ref):
    tm = ckv_ref.shape[0]
    x = ckv_ref[...].astype(F32)
    xn = (x * _rms_scale(x, KV_LORA) * g_ref[...]).astype(BF16)
    kn = jnp.dot(xn, wk_ref[...], preferred_element_type=F32).astype(k_ref.dtype)
    kr = _rope_tile(kr_ref[...].astype(F32), cos_ref, sin_ref).astype(k_ref.dtype)
    for h in range(HEADS):
        k_ref[:, h * C_QK:h * C_QK + C_NOPE] = kn[:, h * C_NOPE:(h + 1) * C_NOPE]
        k_ref[:, h * C_QK + C_NOPE:(h + 1) * C_QK] = kr
    vt = lax.dot_general(wvt_ref[...], xn, _NT, preferred_element_type=F32)
    for h in range(HEADS):
        vt_ref[h * VT_ROWS:h * VT_ROWS + HEAD_DIM, :] = (
            vt[h * HEAD_DIM:(h + 1) * HEAD_DIM, :].astype(vt_ref.dtype))
        vt_ref[h * VT_ROWS + HEAD_DIM:(h + 1) * VT_ROWS, :] = jnp.ones(
            (VT_ONES, tm), vt_ref.dtype)


def _mla_project(proj, g_q, g_kv, w_uq_p, w_k_p, w_vt_p, tabs, layer, b, s,
                 tm=MIXER_TM):
    m = proj.shape[0]
    assert m == b * s and s % tm == 0, (m, b, s, tm)
    tiles = s // tm
    tab_spec = pl.BlockSpec((tm, ROPE_TILE), lambda i: (i, 0))
    g_q, g_q_spec = _row_param(g_q, layer)
    g_kv, g_kv_spec = _row_param(g_kv, layer)
    w_uq_p, w_uq_spec = _layer_param(w_uq_p, layer)
    w_k_p, w_k_spec = _layer_param(w_k_p, layer)
    w_vt_p, w_vt_spec = _layer_param(w_vt_p, layer)
    q_cat = pl.pallas_call(
        _mla_q_kernel,
        grid=(m // tm,),
        in_specs=[pl.BlockSpec((tm, Q_LORA), lambda i: (i, TAIL_CQ // Q_LORA)),
                  g_q_spec, w_uq_spec, tab_spec, tab_spec],
        out_specs=pl.BlockSpec((tm, HEADS * C_QK), lambda i: (i, 0)),
        out_shape=jax.ShapeDtypeStruct((m, HEADS * C_QK), BF16),
        compiler_params=_params(("parallel",)),
        name="mla_q",
    )(proj, g_q, w_uq_p, *tabs)
    k_cat, vt = pl.pallas_call(
        _mla_kv_kernel,
        grid=(m // tm,),
        in_specs=[pl.BlockSpec((tm, KV_LORA), lambda i: (i, TAIL_CKV // KV_LORA)),
                  pl.BlockSpec((tm, ROPE_TILE), lambda i: (i, TAIL_KR // ROPE_TILE)),
                  g_kv_spec, w_k_spec, w_vt_spec, tab_spec, tab_spec],
        out_specs=[pl.BlockSpec((tm, HEADS * C_QK), lambda i: (i, 0)),
                   pl.BlockSpec((None, HEADS * VT_ROWS, tm),
                                lambda i: (i // tiles, 0, i % tiles))],
        out_shape=[jax.ShapeDtypeStruct((m, HEADS * C_QK), BF16),
                   jax.ShapeDtypeStruct((b, HEADS * VT_ROWS, s), BF16)],
        compiler_params=_params(("parallel",)),
        name="mla_kv",
    )(proj, proj, g_kv, w_k_p, w_vt_p, *tabs)
    return q_cat, k_cat, vt


MLA_T = 256
MLA_W = 512
MLA_HG = 6


def _mla_attn_kernel(q_ref, k_ref, vt_ref, o_ref, m_ref, acc_ref, s_ref,
                     smax_ref):
    t, w = MLA_T, MLA_W
    i = pl.program_id(2)
    k_off = lax.broadcasted_iota(jnp.int32, (w, t), 0)
    q_pos = i * t + lax.broadcasted_iota(jnp.int32, (w, t), 1)

    m_ref[...] = jnp.full_like(m_ref, -jnp.inf)
    acc_ref[...] = jnp.zeros_like(acc_ref)

    def scores(j, slot):
        ks = pl.ds(pl.multiple_of(j * w, w), w)
        for h in range(MLA_HG):
            s = lax.dot_general(k_ref[0, ks, h * C_QK:(h + 1) * C_QK],
                                q_ref[0, :, h * C_QK:(h + 1) * C_QK], _NT,
                                preferred_element_type=F32)
            s_ref[slot, h] = s
            smax_ref[slot, h] = jnp.max(s, axis=0, keepdims=True)

    def softmax_pv(j, slot, masked):
        start = pl.multiple_of(j * w, w)
        ks = pl.ds(start, w)
        for h in range(MLA_HG):
            s = s_ref[slot, h]
            if masked:
                s = jnp.where((start + k_off) <= q_pos, s, -jnp.inf)
                s_max = jnp.max(s, axis=0, keepdims=True)
            else:
                s_max = smax_ref[slot, h]
            m_old = m_ref[h]
            m_new = jnp.maximum(m_old, s_max)
            alpha = jnp.exp2(m_old - m_new)
            p = jnp.exp2(s - m_new)
            acc_ref[h] = alpha * acc_ref[h] + jnp.dot(
                vt_ref[0, h * VT_ROWS:(h + 1) * VT_ROWS, ks], p.astype(BF16),
                preferred_element_type=F32)
            m_ref[h] = m_new

    last = (i * t) // w

    scores(0, 0)

    def pair(jj, carry):
        j = 2 * jj
        scores(j + 1, 1)
        softmax_pv(j, 0, False)
        scores(j + 2, 0)
        softmax_pv(j + 1, 1, False)
        return carry

    lax.fori_loop(0, last // 2, pair, 0)

    @pl.when(last % 2 == 0)
    def _():
        softmax_pv(last, 0, True)

    @pl.when(last % 2 == 1)
    def _():
        scores(last, 1)
        softmax_pv(last - 1, 0, False)
        softmax_pv(last, 1, True)
    for h in range(MLA_HG):
        acc = acc_ref[h]
        out_t = acc[:HEAD_DIM] / acc[HEAD_DIM:HEAD_DIM + 1]
        o_ref[0, :, h * HEAD_DIM:(h + 1) * HEAD_DIM] = out_t.T.astype(o_ref.dtype)


def _mla_attention(q_cat, k_cat, vt, b, s):
    t, hg = MLA_T, MLA_HG
    assert MLA_W % t == 0 and s % MLA_W == 0 and HEADS % hg == 0
    q3 = q_cat.reshape(b, s, HEADS * C_QK)
    k3 = k_cat.reshape(b, s, HEADS * C_QK)
    return pl.pallas_call(
        _mla_attn_kernel,
        grid=(b, HEADS // hg, s // t),
        in_specs=[pl.BlockSpec((1, t, hg * C_QK), lambda bi, g, i: (bi, i, g)),
                  pl.BlockSpec((1, s, hg * C_QK), lambda bi, g, i: (bi, 0, g)),
                  pl.BlockSpec((1, hg * VT_ROWS, s), lambda bi, g, i: (bi, g, 0))],
        out_specs=pl.BlockSpec((1, t, hg * HEAD_DIM), lambda bi, g, i: (bi, i, g)),
        out_shape=jax.ShapeDtypeStruct((b, s, C_WIDTH), BF16),
        scratch_shapes=[pltpu.VMEM((hg, 1, t), F32),
                        pltpu.VMEM((hg, VT_ROWS, t), F32),
                        pltpu.VMEM((2, hg, MLA_W, t), F32),
                        pltpu.VMEM((2, hg, 1, t), F32)],
        compiler_params=_params(("parallel", "parallel", "arbitrary")),
        name="mla_attention",
    )(q3, k3, vt)


GATE_ROWS = 32


def _gate_kernel(ya_ref, yb_ref, zb_ref, yc_ref, zc_ref, g_ref, o_ref):
    o_ref[:, :A_WIDTH] = ya_ref[...]

    def chunk(c, carry):
        rows = pl.ds(pl.multiple_of(c * GATE_ROWS, GATE_ROWS), GATE_ROWS)
        lo = A_WIDTH
        for y_ref, z_ref, width in ((yb_ref, zb_ref, B_WIDTH),
                                    (yc_ref, zc_ref, C_WIDTH)):
            y = y_ref[rows, :].astype(F32)
            z = z_ref[rows, :].astype(F32)
            g = g_ref[:, lo:lo + width]
            o_ref[rows, lo:lo + width] = (y * _rms_scale(y, width) * g
                                          * _silu(z)).astype(o_ref.dtype)
            lo += width
        return carry

    lax.fori_loop(0, ya_ref.shape[0] // GATE_ROWS, chunk, 0)


def _gate(ya, yb, yc, main, tail, g_out, layer, tm=MIXER_TM):
    m = ya.shape[0]
    assert m % tm == 0 and tm % GATE_ROWS == 0, (m, tm)
    g_out, g_out_spec = _row_param(g_out, layer)
    return pl.pallas_call(
        _gate_kernel,
        grid=(m // tm,),
        in_specs=[pl.BlockSpec((tm, A_WIDTH), lambda i: (i, 0)),
                  pl.BlockSpec((tm, B_WIDTH), lambda i: (i, 0)),
                  pl.BlockSpec((tm, B_WIDTH), lambda i: (i, OFF_ZB // B_WIDTH)),
                  pl.BlockSpec((tm, C_WIDTH), lambda i: (i, 0)),
                  pl.BlockSpec((tm, C_WIDTH), lambda i: (i, TAIL_ZC // C_WIDTH)),
                  g_out_spec],
        out_specs=pl.BlockSpec((tm, D_MODEL), lambda i: (i, 0)),
        out_shape=jax.ShapeDtypeStruct((m, D_MODEL), BF16),
        compiler_params=_params(("parallel",)),
        name="gate",
    )(ya, yb, main, yc, tail, g_out)


def _prep_w_uq(w):
    depth = w.shape[0]
    w = w.reshape(depth, Q_LORA, HEADS, C_NOPE + C_ROPE)
    gap = jnp.zeros((depth, Q_LORA, HEADS, ROPE_X2 - ROPE_HALF), w.dtype)
    w = jnp.concatenate([w[..., :C_NOPE + ROPE_HALF], gap,
                         w[..., C_NOPE + ROPE_HALF:], gap], axis=3)
    return w.reshape(depth, Q_LORA, HEADS * C_QK).astype(BF16)


def _prep_w_ukv(w):
    depth = w.shape[0]
    w = w.reshape(depth, KV_LORA, HEADS, C_NOPE + HEAD_DIM)
    wk = w[..., :C_NOPE].reshape(depth, KV_LORA, HEADS * C_NOPE)
    wv_t = jnp.swapaxes(w[..., C_NOPE:].reshape(depth, KV_LORA, C_WIDTH), 1, 2)
    return wk.astype(BF16), wv_t.astype(BF16)


def _rope_tables(positions):
    inv_freq = 1.0 / (ROPE_THETA ** (jnp.arange(0, C_ROPE, 2, dtype=F32) / C_ROPE))
    ang = positions.astype(F32).reshape(-1, 1) * inv_freq
    cos, sin = jnp.cos(ang), jnp.sin(ang)
    gap = jnp.zeros((ang.shape[0], ROPE_X2 - ROPE_HALF), F32)
    return (jnp.concatenate([cos, gap, cos, gap], axis=1),
            jnp.concatenate([-sin, gap, sin, gap], axis=1))


def kernel(x, positions, g_pre, w_in, a_g_v, a_w_s, a_b_s, c_g_q, c_g_kv,
           c_w_uq, c_w_ukv, g_out, w_out, g_final):
    b, s, d = x.shape
    m = b * s
    depth = w_in.shape[0]
    tabs = _rope_tables(positions)
    w_in_t = jnp.swapaxes(w_in, 1, 2)
    w_uq_p = _prep_w_uq(c_w_uq)
    w_k_p, w_vt_p = _prep_w_ukv(c_w_ukv)
    a_bias = jnp.repeat(jnp.swapaxes(a_b_s, 1, 2), A_CH, axis=2)
    h = x.reshape(m, d)
    for l in range(depth):
        hn = _rmsnorm(h, g_pre, l, BF16)
        main = _in_proj_main(hn, w_in_t, l)
        tail = _in_proj_tail(hn, w_in_t, l)
        ya = _gmlp(main, a_g_v, a_w_s, a_bias, g_out, l)
        yb = _stick_breaking(main.reshape(b, s, D_MAIN)).reshape(m, B_WIDTH)
        q_cat, k_cat, vt = _mla_project(tail, c_g_q, c_g_kv, w_uq_p, w_k_p,
                                        w_vt_p, tabs, l, b, s)
        yc = _mla_attention(q_cat, k_cat, vt, b, s).reshape(m, C_WIDTH)
        y = _gate(ya, yb, yc, main, tail, g_out, l)
        h = _matmul_residual(y, w_out, l, h)
    return _rmsnorm(h, g_final.reshape(1, d), 0, F32).reshape(b, s, d)
```

```python
import math

import jax
import jax.numpy as jnp
from jax import lax
from jax.experimental import pallas as pl
from jax.experimental.pallas import tpu as pltpu

F32 = jnp.float32
BF16 = jnp.bfloat16

D_MODEL = 4096
A_GROUPS = 8
A_CH = 128
A_WIDTH = A_GROUPS * A_CH
CHUNK = 128
HEADS = 12
HEAD_DIM = 128
B_WIDTH = HEADS * HEAD_DIM
C_NOPE = 128
C_ROPE = 64
C_WIDTH = HEADS * HEAD_DIM
Q_LORA = 768
KV_LORA = 512
ROPE_THETA = 10000.0
EPS = 1e-6

C_QK = 256
ROPE_HALF = C_ROPE // 2
ROPE_TILE = 128
ROPE_X2 = ROPE_TILE // 2

OFF_U = 0
OFF_V = OFF_U + A_WIDTH
OFF_ZA = OFF_V + A_WIDTH
OFF_QB = OFF_ZA + A_WIDTH
OFF_KB = OFF_QB + B_WIDTH
OFF_VB = OFF_KB + B_WIDTH
OFF_ZB = OFF_VB + B_WIDTH
D_MAIN = OFF_ZB + B_WIDTH
REF_CQ = D_MAIN
REF_CKV = REF_CQ + Q_LORA
REF_KR = REF_CKV + KV_LORA
REF_ZC = REF_KR + C_ROPE
TAIL_ZC = 0
TAIL_CKV = TAIL_ZC + C_WIDTH
TAIL_KR = TAIL_CKV + KV_LORA
TAIL_CQ = TAIL_KR + C_QK
D_TAIL = TAIL_CQ + Q_LORA

V7X_VMEM_BYTES = 64 * 1024 * 1024
VMEM_LIMIT_BYTES = V7X_VMEM_BYTES * 7 // 8

NORM_TM = 512
PROJ_TM = 1024
PROJ_TN = 512
MAIN_TN = 768
OUT_TM = 2048
OUT_TN = 256
TAIL_TK = 512
MIXER_TM = 1024


def _params(sem):
    return pltpu.CompilerParams(dimension_semantics=sem,
                                vmem_limit_bytes=VMEM_LIMIT_BYTES)


def _gelu(x):
    c = math.sqrt(2.0 / math.pi)
    return x * (0.5 * (1.0 + jnp.tanh(c * (x + 0.044715 * (x * x * x)))))


def _silu(z):
    return z / (1.0 + jnp.exp(-z))


def _rms_scale(x, width):
    return lax.rsqrt(jnp.sum(x * x, axis=-1, keepdims=True) * (1.0 / width) + EPS)


def _rmsnorm_kernel(x_ref, g_ref, o_ref):
    x = x_ref[...]
    o_ref[...] = (x * _rms_scale(x, x.shape[-1]) * g_ref[...]).astype(o_ref.dtype)


def _row_param(p, layer, lo=0, width=None):
    stacked = p.reshape(p.shape[0], 1, -1)
    width = stacked.shape[2] if width is None else width
    spec = pl.BlockSpec((None, 1, width), lambda *_: (layer, 0, lo // width))
    return stacked, spec


def _layer_param(p, layer):
    zeros = (0,) * (p.ndim - 1)
    return p, pl.BlockSpec((None,) + p.shape[1:], lambda *_: (layer,) + zeros)


def _rmsnorm(x, g, layer, out_dtype, tm=NORM_TM):
    m, d = x.shape
    assert m % tm == 0, (m, tm)
    g, g_spec = _row_param(g, layer)
    return pl.pallas_call(
        _rmsnorm_kernel,
        grid=(m // tm,),
        in_specs=[pl.BlockSpec((tm, d), lambda i: (i, 0)), g_spec],
        out_specs=pl.BlockSpec((tm, d), lambda i: (i, 0)),
        out_shape=jax.ShapeDtypeStruct((m, d), out_dtype),
        compiler_params=_params(("parallel",)),
        name="rmsnorm",
    )(x, g)


_NT = (((1,), (1,)), ((), ()))


def _in_proj_main_kernel(x_ref, wt_ref, o_ref):
    o_ref[...] = lax.dot_general(x_ref[...], wt_ref[...].astype(BF16), _NT,
                                 preferred_element_type=F32).astype(o_ref.dtype)


def _in_proj_main(x, wt, layer, tm=PROJ_TM, tn=MAIN_TN):
    m, k = x.shape
    assert m % tm == 0 and D_MAIN % tn == 0 and wt.shape[2] == k, (m, tm, tn)
    return pl.pallas_call(
        _in_proj_main_kernel,
        grid=(m // tm, D_MAIN // tn),
        in_specs=[pl.BlockSpec((tm, k), lambda i, j: (i, 0)),
                  pl.BlockSpec((None, tn, k), lambda i, j: (layer, j, 0))],
        out_specs=pl.BlockSpec((tm, tn), lambda i, j: (i, j)),
        out_shape=jax.ShapeDtypeStruct((m, D_MAIN), BF16),
        compiler_params=_params(("parallel", "parallel")),
        name="in_proj_main",
    )(x, wt)


TAIL_PART_ROWS = 576
TAIL_PARTS = 5
TAIL_COPIES = (
    (TAIL_ZC, REF_ZC - D_MAIN, C_WIDTH),
    (TAIL_CKV, REF_CKV - D_MAIN, KV_LORA),
    (TAIL_KR, REF_KR - D_MAIN, ROPE_HALF),
    (TAIL_KR + ROPE_X2, REF_KR - D_MAIN + ROPE_HALF, ROPE_HALF),
    (TAIL_CQ, REF_CQ - D_MAIN, Q_LORA),
)
TAIL_ZERO_ROWS = ((TAIL_KR + ROPE_HALF, ROPE_X2 - ROPE_HALF),
                  (TAIL_KR + ROPE_X2 + ROPE_HALF, ROPE_X2 - ROPE_HALF),
                  (TAIL_KR + ROPE_TILE, TAIL_CQ - TAIL_KR - ROPE_TILE))


def _in_proj_tail_kernel(x_ref, *refs):
    w_refs = refs[:TAIL_PARTS]
    o_ref, wp_ref, acc_ref = refs[TAIL_PARTS:]
    kk = pl.program_id(1)

    @pl.when(kk == 0)
    def _():
        acc_ref[...] = jnp.zeros_like(acc_ref)

    for dst, src, rows in TAIL_COPIES:
        done = 0
        while done < rows:
            part, lo = divmod(src + done, TAIL_PART_ROWS)
            n = min(rows - done, TAIL_PART_ROWS - lo)
            wp_ref[dst + done:dst + done + n, :] = (
                w_refs[part][lo:lo + n, :].astype(BF16))
            done += n
    for dst, rows in TAIL_ZERO_ROWS:
        wp_ref[dst:dst + rows, :] = jnp.zeros((rows, wp_ref.shape[1]), BF16)

    acc_ref[...] += lax.dot_general(x_ref[...], wp_ref[...], _NT,
                                    preferred_element_type=F32)

    @pl.when(kk == pl.num_programs(1) - 1)
    def _():
        o_ref[...] = acc_ref[...].astype(o_ref.dtype)


def _in_proj_tail(x, wt, layer, tm=PROJ_TM, tk=TAIL_TK):
    m, k = x.shape
    assert m % tm == 0 and k % tk == 0 and D_MAIN % TAIL_PART_ROWS == 0
    assert wt.shape[1] == D_MAIN + TAIL_PARTS * TAIL_PART_ROWS, wt.shape
    first = D_MAIN // TAIL_PART_ROWS
    w_specs = [pl.BlockSpec((None, TAIL_PART_ROWS, tk),
                            lambda i, kk, p=p: (layer, first + p, kk))
               for p in range(TAIL_PARTS)]
    return pl.pallas_call(
        _in_proj_tail_kernel,
        grid=(m // tm, k // tk),
        in_specs=[pl.BlockSpec((tm, tk), lambda i, kk: (i, kk))] + w_specs,
        out_specs=pl.BlockSpec((tm, D_TAIL), lambda i, kk: (i, 0)),
        out_shape=jax.ShapeDtypeStruct((m, D_TAIL), BF16),
        scratch_shapes=[pltpu.VMEM((D_TAIL, tk), BF16),
                        pltpu.VMEM((tm, D_TAIL), F32)],
        compiler_params=_params(("parallel", "arbitrary")),
        name="in_proj_tail",
    )(x, *([wt] * TAIL_PARTS))


def _matmul_residual_kernel(x_ref, w_ref, r_ref, o_ref):
    o_ref[...] = r_ref[...] + jnp.dot(x_ref[...], w_ref[...].astype(BF16),
                                      preferred_element_type=F32)


def _matmul_residual(x, w, layer, res, tm=OUT_TM, tn=OUT_TN):
    m, k = x.shape
    n = w.shape[2]
    assert m % tm == 0 and n % tn == 0 and w.shape[1] == k, (m, k, n)
    return pl.pallas_call(
        _matmul_residual_kernel,
        grid=(m // tm, n // tn),
        in_specs=[pl.BlockSpec((tm, k), lambda i, j: (i, 0),
                               pipeline_mode=pl.Buffered(1)),
                  pl.BlockSpec((None, k, tn), lambda i, j: (layer, 0, j)),
                  pl.BlockSpec((tm, tn), lambda i, j: (i, j))],
        out_specs=pl.BlockSpec((tm, tn), lambda i, j: (i, j)),
        out_shape=jax.ShapeDtypeStruct((m, n), F32),
        compiler_params=_params(("parallel", "parallel")),
        name="out_proj",
    )(x, w, res)


def _gmlp_kernel(u_ref, v_ref, z_ref, gv_ref, w_ref, bias_ref, go_ref, o_ref,
                 y_ref):
    tm = u_ref.shape[0]
    t_idx = lax.broadcasted_iota(jnp.int32, (CHUNK, CHUNK), 0)
    s_idx = lax.broadcasted_iota(jnp.int32, (CHUNK, CHUNK), 1)
    causal = s_idx <= t_idx

    def chunk(c, carry):
        rows = pl.ds(pl.multiple_of(c * CHUNK, CHUNK), CHUNK)
        ssq = jnp.zeros((CHUNK, 1), F32)
        for g in range(A_GROUPS):
            cols = slice(g * A_CH, (g + 1) * A_CH)
            vg = _gelu(v_ref[rows, cols].astype(F32))
            vn = vg * _rms_scale(vg, A_CH) * gv_ref[:, cols]
            w = jnp.where(causal, w_ref[g], 0.0).astype(BF16)
            sv = jnp.dot(w, vn.astype(BF16), preferred_element_type=F32)
            sv = sv + bias_ref[:, cols]
            y = _gelu(u_ref[rows, cols].astype(F32)) * sv
            ssq = ssq + jnp.sum(y * y, axis=-1, keepdims=True)
            y_ref[:, cols] = y
        r = lax.rsqrt(ssq * (1.0 / A_WIDTH) + EPS)
        z = z_ref[rows, :].astype(F32)
        o_ref[rows, :] = (y_ref[...] * r * go_ref[...] * _silu(z)).astype(o_ref.dtype)
        return carry

    lax.fori_loop(0, tm // CHUNK, chunk, 0)


def _gmlp(proj, g_v, w_s, bias, g_out, layer, tm=MIXER_TM):
    m = proj.shape[0]
    assert m % tm == 0 and tm % CHUNK == 0, (m, tm)
    col = lambda off: (lambda i: (i, off // A_WIDTH))
    g_v, g_v_spec = _row_param(g_v, layer)
    w_s, w_s_spec = _layer_param(w_s, layer)
    bias, bias_spec = _layer_param(bias, layer)
    g_out, g_out_spec = _row_param(g_out, layer, 0, A_WIDTH)
    return pl.pallas_call(
        _gmlp_kernel,
        grid=(m // tm,),
        in_specs=[pl.BlockSpec((tm, A_WIDTH), col(OFF_U)),
                  pl.BlockSpec((tm, A_WIDTH), col(OFF_V)),
                  pl.BlockSpec((tm, A_WIDTH), col(OFF_ZA)),
                  g_v_spec, w_s_spec, bias_spec, g_out_spec],
        out_specs=pl.BlockSpec((tm, A_WIDTH), lambda i: (i, 0)),
        out_shape=jax.ShapeDtypeStruct((m, A_WIDTH), BF16),
        scratch_shapes=[pltpu.VMEM((CHUNK, A_WIDTH), F32)],
        compiler_params=_params(("parallel",)),
        name="gmlp",
    )(proj, proj, proj, g_v, w_s, bias, g_out)


SB_T = 128
SB_SUB = 128
SB_NSUB = 3
SB_W = SB_NSUB * SB_SUB
SB_HG = 6
SB_STAGE = 6
SB_ZERO_WEIGHT_LOG = -105.0


def _sb_kernel(q_ref, k_ref, v_ref, o_ref, acc_ref, carry_ref):
    t, w, sub = SB_T, SB_W, SB_SUB
    i = pl.program_id(2)
    scale = HEAD_DIM ** -0.5
    q_pos = i * t + lax.broadcasted_iota(jnp.int32, (t, w), 0)
    k_off = lax.broadcasted_iota(jnp.int32, (t, w), 1)
    row = lax.broadcasted_iota(jnp.int32, (2 * sub, sub), 0) % sub
    col = lax.broadcasted_iota(jnp.int32, (2 * sub, sub), 1)
    neg_suffix = jnp.where(row > col, -1.0, 0.0).astype(BF16)

    acc_ref[...] = jnp.zeros_like(acc_ref)
    carry_ref[...] = jnp.zeros_like(carry_ref)

    def window(start, bound):
        valid = (start + k_off) < jnp.minimum(q_pos, bound)
        ks = pl.ds(pl.multiple_of(start, sub), w)
        worst = None
        for h0 in range(0, SB_HG, SB_STAGE):
            worst = stage(range(h0, h0 + SB_STAGE), valid, ks, worst)
        return worst

    def stage(heads, valid, ks, worst):
        log_beta, softplus, pieces = {}, {}, []
        for h in heads:
            hc = slice(h * HEAD_DIM, (h + 1) * HEAD_DIM)
            z = lax.dot_general(q_ref[0, :, hc], k_ref[0, ks, hc],
                                (((1,), (1,)), ((), ())),
                                preferred_element_type=F32) * scale
            sp = jnp.maximum(z, 0.0) + jnp.log(1.0 + jnp.exp(-jnp.abs(z)))
            log_beta[h] = z - sp
            sp = jnp.where(valid, sp, 0.0)
            softplus[h] = sp
            hi = sp.astype(BF16)
            lo = (sp - hi.astype(F32)).astype(BF16)
            for c in range(SB_NSUB):
                pieces.append(jnp.concatenate(
                    [hi[:, c * sub:(c + 1) * sub], lo[:, c * sub:(c + 1) * sub]],
                    axis=1))
        local = jnp.dot(jnp.concatenate(pieces, axis=0), neg_suffix,
                        preferred_element_type=F32)
        for n, h in enumerate(heads):
            hc = slice(h * HEAD_DIM, (h + 1) * HEAD_DIM)
            sp = softplus[h]
            running = carry_ref[h]
            after = [None] * SB_NSUB
            for c in reversed(range(SB_NSUB)):
                base = (n * SB_NSUB + c) * t
                after[c] = local[base:base + t] + running
                running = running - jnp.sum(sp[:, c * sub:(c + 1) * sub],
                                            axis=-1, keepdims=True)
            a = jnp.where(valid,
                          jnp.exp(log_beta[h] + jnp.concatenate(after, axis=1)),
                          0.0)
            acc_ref[h] += jnp.dot(a.astype(BF16), v_ref[0, ks, hc],
                                  preferred_element_type=F32)
            carry_ref[h] = running
            top = jnp.max(running)
            worst = top if worst is None else jnp.maximum(worst, top)
        return worst

    first = jnp.maximum(i + 1 - SB_NSUB, 0) * sub
    worst = window(first, jnp.int32(2 ** 30))

    def more(state):
        start, worst = state
        return jnp.logical_and(start > 0, worst > SB_ZERO_WEIGHT_LOG)

    def step(state):
        start, _ = state
        nxt = jnp.maximum(start - w, 0)
        return nxt, window(nxt, start)

    lax.while_loop(more, step, (first, worst))
    for h in range(SB_HG):
        o_ref[0, :, h * HEAD_DIM:(h + 1) * HEAD_DIM] = acc_ref[h].astype(o_ref.dtype)


def _stick_breaking(proj3):
    b, s, _ = proj3.shape
    assert s % SB_T == 0 and s >= SB_W and SB_T == SB_SUB and HEADS % SB_HG == 0
    t, gw = SB_T, SB_HG * HEAD_DIM
    blk = lambda off: off // gw
    return pl.pallas_call(
        _sb_kernel,
        grid=(b, HEADS // SB_HG, s // t),
        in_specs=[
            pl.BlockSpec((1, t, gw), lambda bi, g, i: (bi, i, blk(OFF_QB) + g)),
            pl.BlockSpec((1, s, gw), lambda bi, g, i: (bi, 0, blk(OFF_KB) + g)),
            pl.BlockSpec((1, s, gw), lambda bi, g, i: (bi, 0, blk(OFF_VB) + g)),
        ],
        out_specs=pl.BlockSpec((1, t, gw), lambda bi, g, i: (bi, i, g)),
        out_shape=jax.ShapeDtypeStruct((b, s, B_WIDTH), BF16),
        scratch_shapes=[pltpu.VMEM((SB_HG, t, HEAD_DIM), F32),
                        pltpu.VMEM((SB_HG, t, 1), F32)],
        compiler_params=_params(("parallel", "parallel", "arbitrary")),
        name="stick_breaking",
    )(proj3, proj3, proj3)


def _rope_tile(x, cos_ref, sin_ref):
    return x * cos_ref[...] + pltpu.roll(x, ROPE_X2, axis=1) * sin_ref[...]


def _mla_q_kernel(cq_ref, g_ref, w_ref, cos_ref, sin_ref, o_ref):
    scale = (C_NOPE + C_ROPE) ** -0.5 * math.log2(math.e)
    x = cq_ref[...].astype(F32)
    xn = (x * _rms_scale(x, Q_LORA) * g_ref[...]).astype(BF16)
    q = jnp.dot(xn, w_ref[...], preferred_element_type=F32)
    for h in range(HEADS):
        lo = h * C_QK
        o_ref[:, lo:lo + C_NOPE] = (q[:, lo:lo + C_NOPE] * scale).astype(o_ref.dtype)
        rope = _rope_tile(q[:, lo + C_NOPE:lo + C_QK], cos_ref, sin_ref)
        o_ref[:, lo + C_NOPE:lo + C_QK] = (rope * scale).astype(o_ref.dtype)


VT_ONES = 16
VT_ROWS = HEAD_DIM + VT_ONES


def _mla_kv_kernel(ckv_ref, kr_ref, g_ref, wk_ref, wvt_ref, cos_ref, sin_ref,
                   k_ref, vt_ref):
    tm = ckv_ref.shape[0]
    x = ckv_ref[...].astype(F32)
    xn = (x * _rms_scale(x, KV_LORA) * g_ref[...]).astype(BF16)
    kn = jnp.dot(xn, wk_ref[...], preferred_element_type=F32).astype(k_ref.dtype)
    kr = _rope_tile(kr_ref[...].astype(F32), cos_ref, sin_ref).astype(k_ref.dtype)
    for h in range(HEADS):
        k_ref[:, h * C_QK:h * C_QK + C_NOPE] = kn[:, h * C_NOPE:(h + 1) * C_NOPE]
        k_ref[:, h * C_QK + C_NOPE:(h + 1) * C_QK] = kr
    vt = lax.dot_general(wvt_ref[...], xn, _NT, preferred_element_type=F32)
    for h in range(HEADS):
        vt_ref[h * VT_ROWS:h * VT_ROWS + HEAD_DIM, :] = (
            vt[h * HEAD_DIM:(h + 1) * HEAD_DIM, :].astype(vt_ref.dtype))
        vt_ref[h * VT_ROWS + HEAD_DIM:(h + 1) * VT_ROWS, :] = jnp.ones(
            (VT_ONES, tm), vt_ref.dtype)


def _mla_project(proj, g_q, g_kv, w_uq_p, w_k_p, w_vt_p, tabs, layer, b, s,
                 tm=MIXER_TM):
    m = proj.shape[0]
    assert m == b * s and s % tm == 0, (m, b, s, tm)
    tiles = s // tm
    tab_spec = pl.BlockSpec((tm, ROPE_TILE), lambda i: (i, 0))
    g_q, g_q_spec = _row_param(g_q, layer)
    g_kv, g_kv_spec = _row_param(g_kv, layer)
    w_uq_p, w_uq_spec = _layer_param(w_uq_p, layer)
    w_k_p, w_k_spec = _layer_param(w_k_p, layer)
    w_vt_p, w_vt_spec = _layer_param(w_vt_p, layer)
    q_cat = pl.pallas_call(
        _mla_q_kernel,
        grid=(m // tm,),
        in_specs=[pl.BlockSpec((tm, Q_LORA), lambda i: (i, TAIL_CQ // Q_LORA)),
                  g_q_spec, w_uq_spec, tab_spec, tab_spec],
        out_specs=pl.BlockSpec((tm, HEADS * C_QK), lambda i: (i, 0)),
        out_shape=jax.ShapeDtypeStruct((m, HEADS * C_QK), BF16),
        compiler_params=_params(("parallel",)),
        name="mla_q",
    )(proj, g_q, w_uq_p, *tabs)
    k_cat, vt = pl.pallas_call(
        _mla_kv_kernel,
        grid=(m // tm,),
        in_specs=[pl.BlockSpec((tm, KV_LORA), lambda i: (i, TAIL_CKV // KV_LORA)),
                  pl.BlockSpec((tm, ROPE_TILE), lambda i: (i, TAIL_KR // ROPE_TILE)),
                  g_kv_spec, w_k_spec, w_vt_spec, tab_spec, tab_spec],
        out_specs=[pl.BlockSpec((tm, HEADS * C_QK), lambda i: (i, 0)),
                   pl.BlockSpec((None, HEADS * VT_ROWS, tm),
                                lambda i: (i // tiles, 0, i % tiles))],
        out_shape=[jax.ShapeDtypeStruct((m, HEADS * C_QK), BF16),
                   jax.ShapeDtypeStruct((b, HEADS * VT_ROWS, s), BF16)],
        compiler_params=_params(("parallel",)),
        name="mla_kv",
    )(proj, proj, g_kv, w_k_p, w_vt_p, *tabs)
    return q_cat, k_cat, vt


MLA_T = 256
MLA_W = 512
MLA_HG = 6


def _mla_attn_kernel(q_ref, k_ref, vt_ref, o_ref, m_ref, acc_ref, s_ref,
                     smax_ref):
    t, w = MLA_T, MLA_W
    i = pl.program_id(2)
    k_off = lax.broadcasted_iota(jnp.int32, (w, t), 0)
    q_pos = i * t + lax.broadcasted_iota(jnp.int32, (w, t), 1)

    m_ref[...] = jnp.full_like(m_ref, -jnp.inf)
    acc_ref[...] = jnp.zeros_like(acc_ref)

    def scores(j, slot):
        ks = pl.ds(pl.multiple_of(j * w, w), w)
        for h in range(MLA_HG):
            s = lax.dot_general(k_ref[0, ks, h * C_QK:(h + 1) * C_QK],
                                q_ref[0, :, h * C_QK:(h + 1) * C_QK], _NT,
                                preferred_element_type=F32)
            s_ref[slot, h] = s
            smax_ref[slot, h] = jnp.max(s, axis=0, keepdims=True)

    def softmax_pv(j, slot, masked):
        start = pl.multiple_of(j * w, w)
        ks = pl.ds(start, w)
        for h in range(MLA_HG):
            s = s_ref[slot, h]
            if masked:
                s = jnp.where((start + k_off) <= q_pos, s, -jnp.inf)
                s_max = jnp.max(s, axis=0, keepdims=True)
            else:
                s_max = smax_ref[slot, h]
            m_old = m_ref[h]
            m_new = jnp.maximum(m_old, s_max)
            alpha = jnp.exp2(m_old - m_new)
            p = jnp.exp2(s - m_new)
            acc_ref[h] = alpha * acc_ref[h] + jnp.dot(
                vt_ref[0, h * VT_ROWS:(h + 1) * VT_ROWS, ks], p.astype(BF16),
                preferred_element_type=F32)
            m_ref[h] = m_new

    last = (i * t) // w

    scores(0, 0)

    def pair(jj, carry):
        j = 2 * jj
        scores(j + 1, 1)
        softmax_pv(j, 0, False)
        scores(j + 2, 0)
        softmax_pv(j + 1, 1, False)
        return carry

    lax.fori_loop(0, last // 2, pair, 0)

    @pl.when(last % 2 == 0)
    def _():
        softmax_pv(last, 0, True)

    @pl.when(last % 2 == 1)
    def _():
        scores(last, 1)
        softmax_pv(last - 1, 0, False)
        softmax_pv(last, 1, True)
    for h in range(MLA_HG):
        acc = acc_ref[h]
        out_t = acc[:HEAD_DIM] / acc[HEAD_DIM:HEAD_DIM + 1]
        o_ref[0, :, h * HEAD_DIM:(h + 1) * HEAD_DIM] = out_t.T.astype(o_ref.dtype)


def _mla_attention(q_cat, k_cat, vt, b, s):
    t, hg = MLA_T, MLA_HG
    assert MLA_W % t == 0 and s % MLA_W == 0 and HEADS % hg == 0
    q3 = q_cat.reshape(b, s, HEADS * C_QK)
    k3 = k_cat.reshape(b, s, HEADS * C_QK)
    return pl.pallas_call(
        _mla_attn_kernel,
        grid=(b, HEADS // hg, s // t),
        in_specs=[pl.BlockSpec((1, t, hg * C_QK), lambda bi, g, i: (bi, i, g)),
                  pl.BlockSpec((1, s, hg * C_QK), lambda bi, g, i: (bi, 0, g)),
                  pl.BlockSpec((1, hg * VT_ROWS, s), lambda bi, g, i: (bi, g, 0))],
        out_specs=pl.BlockSpec((1, t, hg * HEAD_DIM), lambda bi, g, i: (bi, i, g)),
        out_shape=jax.ShapeDtypeStruct((b, s, C_WIDTH), BF16),
        scratch_shapes=[pltpu.VMEM((hg, 1, t), F32),
                        pltpu.VMEM((hg, VT_ROWS, t), F32),
                        pltpu.VMEM((2, hg, MLA_W, t), F32),
                        pltpu.VMEM((2, hg, 1, t), F32)],
        compiler_params=_params(("parallel", "parallel", "arbitrary")),
        name="mla_attention",
    )(q3, k3, vt)


GATE_ROWS = 32


def _gate_kernel(ya_ref, yb_ref, zb_ref, yc_ref, zc_ref, g_ref, o_ref):
    o_ref[:, :A_WIDTH] = ya_ref[...]

    def chunk(c, carry):
        rows = pl.ds(pl.multiple_of(c * GATE_ROWS, GATE_ROWS), GATE_ROWS)
        lo = A_WIDTH
        for y_ref, z_ref, width in ((yb_ref, zb_ref, B_WIDTH),
                                    (yc_ref, zc_ref, C_WIDTH)):
            y = y_ref[rows, :].astype(F32)
            z = z_ref[rows, :].astype(F32)
            g = g_ref[:, lo:lo + width]
            o_ref[rows, lo:lo + width] = (y * _rms_scale(y, width) * g
                                          * _silu(z)).astype(o_ref.dtype)
            lo += width
        return carry

    lax.fori_loop(0, ya_ref.shape[0] // GATE_ROWS, chunk, 0)


def _gate(ya, yb, yc, main, tail, g_out, layer, tm=MIXER_TM):
    m = ya.shape[0]
    assert m % tm == 0 and tm % GATE_ROWS == 0, (m, tm)
    g_out, g_out_spec = _row_param(g_out, layer)
    return pl.pallas_call(
        _gate_kernel,
        grid=(m // tm,),
        in_specs=[pl.BlockSpec((tm, A_WIDTH), lambda i: (i, 0)),
                  pl.BlockSpec((tm, B_WIDTH), lambda i: (i, 0)),
                  pl.BlockSpec((tm, B_WIDTH), lambda i: (i, OFF_ZB // B_WIDTH)),
                  pl.BlockSpec((tm, C_WIDTH), lambda i: (i, 0)),
                  pl.BlockSpec((tm, C_WIDTH), lambda i: (i, TAIL_ZC // C_WIDTH)),
                  g_out_spec],
        out_specs=pl.BlockSpec((tm, D_MODEL), lambda i: (i, 0)),
        out_shape=jax.ShapeDtypeStruct((m, D_MODEL), BF16),
        compiler_params=_params(("parallel",)),
        name="gate",
    )(ya, yb, main, yc, tail, g_out)


def _prep_w_uq(w):
    depth = w.shape[0]
    w = w.reshape(depth, Q_LORA, HEADS, C_NOPE + C_ROPE)
    gap = jnp.zeros((depth, Q_LORA, HEADS, ROPE_X2 - ROPE_HALF), w.dtype)
    w = jnp.concatenate([w[..., :C_NOPE + ROPE_HALF], gap,
                         w[..., C_NOPE + ROPE_HALF:], gap], axis=3)
    return w.reshape(depth, Q_LORA, HEADS * C_QK).astype(BF16)


def _prep_w_ukv(w):
    depth = w.shape[0]
    w = w.reshape(depth, KV_LORA, HEADS, C_NOPE + HEAD_DIM)
    wk = w[..., :C_NOPE].reshape(depth, KV_LORA, HEADS * C_NOPE)
    wv_t = jnp.swapaxes(w[..., C_NOPE:].reshape(depth, KV_LORA, C_WIDTH), 1, 2)
    return wk.astype(BF16), wv_t.astype(BF16)


def _rope_tables(positions):
    inv_freq = 1.0 / (ROPE_THETA ** (jnp.arange(0, C_ROPE, 2, dtype=F32) / C_ROPE))
    ang = positions.astype(F32).reshape(-1, 1) * inv_freq
    cos, sin = jnp.cos(ang), jnp.sin(ang)
    gap = jnp.zeros((ang.shape[0], ROPE_X2 - ROPE_HALF), F32)
    return (jnp.concatenate([cos, gap, cos, gap], axis=1),
            jnp.concatenate([-sin, gap, sin, gap], axis=1))


def kernel(x, positions, g_pre, w_in, a_g_v, a_w_s, a_b_s, c_g_q, c_g_kv,
           c_w_uq, c_w_ukv, g_out, w_out, g_final):
    b, s, d = x.shape
    m = b * s
    depth = w_in.shape[0]
    tabs = _rope_tables(positions)
    w_in_t = jnp.swapaxes(w_in, 1, 2)
    w_uq_p = _prep_w_uq(c_w_uq)
    w_k_p, w_vt_p = _prep_w_ukv(c_w_ukv)
    a_bias = jnp.repeat(jnp.swapaxes(a_b_s, 1, 2), A_CH, axis=2)
    h = x.reshape(m, d)
    for l in range(depth):
        hn = _rmsnorm(h, g_pre, l, BF16)
        main = _in_proj_main(hn, w_in_t, l)
        tail = _in_proj_tail(hn, w_in_t, l)
        ya = _gmlp(main, a_g_v, a_w_s, a_bias, g_out, l)
        yb = _stick_breaking(main.reshape(b, s, D_MAIN)).reshape(m, B_WIDTH)
        q_cat, k_cat, vt = _mla_project(tail, c_g_q, c_g_kv, w_uq_p, w_k_p,
                                        w_vt_p, tabs, l, b, s)
        yc = _mla_attention(q_cat, k_cat, vt, b, s).reshape(m, C_WIDTH)
        y = _gate(ya, yb, yc, main, tail, g_out, l)
        h = _matmul_residual(y, w_out, l, h)
    return _rmsnorm(h, g_final.reshape(1, d), 0, F32).reshape(b, s, d)
```

```python
import math

import jax
import jax.numpy as jnp
from jax import lax
from jax.experimental import pallas as pl
from jax.experimental.pallas import tpu as pltpu

F32 = jnp.float32
BF16 = jnp.bfloat16

D_MODEL = 4096
A_GROUPS = 8
A_CH = 128
A_WIDTH = A_GROUPS * A_CH
CHUNK = 128
HEADS = 12
HEAD_DIM = 128
B_WIDTH = HEADS * HEAD_DIM
C_NOPE = 128
C_ROPE = 64
C_WIDTH = HEADS * HEAD_DIM
Q_LORA = 768
KV_LORA = 512
ROPE_THETA = 10000.0
EPS = 1e-6

C_QK = 256
ROPE_HALF = C_ROPE // 2
ROPE_TILE = 128
ROPE_X2 = ROPE_TILE // 2

OFF_U = 0
OFF_V = OFF_U + A_WIDTH
OFF_ZA = OFF_V + A_WIDTH
OFF_QB = OFF_ZA + A_WIDTH
OFF_KB = OFF_QB + B_WIDTH
OFF_VB = OFF_KB + B_WIDTH
OFF_ZB = OFF_VB + B_WIDTH
D_MAIN = OFF_ZB + B_WIDTH
REF_CQ = D_MAIN
REF_CKV = REF_CQ + Q_LORA
REF_KR = REF_CKV + KV_LORA
REF_ZC = REF_KR + C_ROPE
TAIL_ZC = 0
TAIL_CKV = TAIL_ZC + C_WIDTH
TAIL_KR = TAIL_CKV + KV_LORA
TAIL_CQ = TAIL_KR + C_QK
D_TAIL = TAIL_CQ + Q_LORA

V7X_VMEM_BYTES = 64 * 1024 * 1024
VMEM_LIMIT_BYTES = V7X_VMEM_BYTES * 7 // 8

NORM_TM = 512
PROJ_TM = 1024
PROJ_TN = 512
MAIN_TN = 768
TAIL_TK = 512
MIXER_TM = 1024


def _params(sem):
    return pltpu.CompilerParams(dimension_semantics=sem,
                                vmem_limit_bytes=VMEM_LIMIT_BYTES)


def _gelu(x):
    c = math.sqrt(2.0 / math.pi)
    return x * (0.5 * (1.0 + jnp.tanh(c * (x + 0.044715 * (x * x * x)))))


def _silu(z):
    return z / (1.0 + jnp.exp(-z))


def _rms_scale(x, width):
    return lax.rsqrt(jnp.sum(x * x, axis=-1, keepdims=True) * (1.0 / width) + EPS)


def _rmsnorm_kernel(x_ref, g_ref, o_ref):
    x = x_ref[...]
    o_ref[...] = (x * _rms_scale(x, x.shape[-1]) * g_ref[...]).astype(o_ref.dtype)


def _row_param(p, layer, lo=0, width=None):
    stacked = p.reshape(p.shape[0], 1, -1)
    width = stacked.shape[2] if width is None else width
    spec = pl.BlockSpec((None, 1, width), lambda *_: (layer, 0, lo // width))
    return stacked, spec


def _layer_param(p, layer):
    zeros = (0,) * (p.ndim - 1)
    return p, pl.BlockSpec((None,) + p.shape[1:], lambda *_: (layer,) + zeros)


def _rmsnorm(x, g, layer, out_dtype, tm=NORM_TM):
    m, d = x.shape
    assert m % tm == 0, (m, tm)
    g, g_spec = _row_param(g, layer)
    return pl.pallas_call(
        _rmsnorm_kernel,
        grid=(m // tm,),
        in_specs=[pl.BlockSpec((tm, d), lambda i: (i, 0)), g_spec],
        out_specs=pl.BlockSpec((tm, d), lambda i: (i, 0)),
        out_shape=jax.ShapeDtypeStruct((m, d), out_dtype),
        compiler_params=_params(("parallel",)),
        name="rmsnorm",
    )(x, g)


_NT = (((1,), (1,)), ((), ()))


def _in_proj_main_kernel(x_ref, wt_ref, o_ref):
    o_ref[...] = lax.dot_general(x_ref[...], wt_ref[...].astype(BF16), _NT,
                                 preferred_element_type=F32).astype(o_ref.dtype)


def _in_proj_main(x, wt, layer, tm=PROJ_TM, tn=MAIN_TN):
    m, k = x.shape
    assert m % tm == 0 and D_MAIN % tn == 0 and wt.shape[2] == k, (m, tm, tn)
    return pl.pallas_call(
        _in_proj_main_kernel,
        grid=(m // tm, D_MAIN // tn),
        in_specs=[pl.BlockSpec((tm, k), lambda i, j: (i, 0)),
                  pl.BlockSpec((None, tn, k), lambda i, j: (layer, j, 0))],
        out_specs=pl.BlockSpec((tm, tn), lambda i, j: (i, j)),
        out_shape=jax.ShapeDtypeStruct((m, D_MAIN), BF16),
        compiler_params=_params(("parallel", "parallel")),
        name="in_proj_main",
    )(x, wt)


TAIL_PART_ROWS = 576
TAIL_PARTS = 5
TAIL_COPIES = (
    (TAIL_ZC, REF_ZC - D_MAIN, C_WIDTH),
    (TAIL_CKV, REF_CKV - D_MAIN, KV_LORA),
    (TAIL_KR, REF_KR - D_MAIN, ROPE_HALF),
    (TAIL_KR + ROPE_X2, REF_KR - D_MAIN + ROPE_HALF, ROPE_HALF),
    (TAIL_CQ, REF_CQ - D_MAIN, Q_LORA),
)
TAIL_ZERO_ROWS = ((TAIL_KR + ROPE_HALF, ROPE_X2 - ROPE_HALF),
                  (TAIL_KR + ROPE_X2 + ROPE_HALF, ROPE_X2 - ROPE_HALF),
                  (TAIL_KR + ROPE_TILE, TAIL_CQ - TAIL_KR - ROPE_TILE))


def _in_proj_tail_kernel(x_ref, *refs):
    w_refs = refs[:TAIL_PARTS]
    o_ref, wp_ref, acc_ref = refs[TAIL_PARTS:]
    kk = pl.program_id(1)

    @pl.when(kk == 0)
    def _():
        acc_ref[...] = jnp.zeros_like(acc_ref)

    for dst, src, rows in TAIL_COPIES:
        done = 0
        while done < rows:
            part, lo = divmod(src + done, TAIL_PART_ROWS)
            n = min(rows - done, TAIL_PART_ROWS - lo)
            wp_ref[dst + done:dst + done + n, :] = (
                w_refs[part][lo:lo + n, :].astype(BF16))
            done += n
    for dst, rows in TAIL_ZERO_ROWS:
        wp_ref[dst:dst + rows, :] = jnp.zeros((rows, wp_ref.shape[1]), BF16)

    acc_ref[...] += lax.dot_general(x_ref[...], wp_ref[...], _NT,
                                    preferred_element_type=F32)

    @pl.when(kk == pl.num_programs(1) - 1)
    def _():
        o_ref[...] = acc_ref[...].astype(o_ref.dtype)


def _in_proj_tail(x, wt, layer, tm=PROJ_TM, tk=TAIL_TK):
    m, k = x.shape
    assert m % tm == 0 and k % tk == 0 and D_MAIN % TAIL_PART_ROWS == 0
    assert wt.shape[1] == D_MAIN + TAIL_PARTS * TAIL_PART_ROWS, wt.shape
    first = D_MAIN // TAIL_PART_ROWS
    w_specs = [pl.BlockSpec((None, TAIL_PART_ROWS, tk),
                            lambda i, kk, p=p: (layer, first + p, kk))
               for p in range(TAIL_PARTS)]
    return pl.pallas_call(
        _in_proj_tail_kernel,
        grid=(m // tm, k // tk),
        in_specs=[pl.BlockSpec((tm, tk), lambda i, kk: (i, kk))] + w_specs,
        out_specs=pl.BlockSpec((tm, D_TAIL), lambda i, kk: (i, 0)),
        out_shape=jax.ShapeDtypeStruct((m, D_TAIL), BF16),
        scratch_shapes=[pltpu.VMEM((D_TAIL, tk), BF16),
                        pltpu.VMEM((tm, D_TAIL), F32)],
        compiler_params=_params(("parallel", "arbitrary")),
        name="in_proj_tail",
    )(x, *([wt] * TAIL_PARTS))


def _matmul_residual_kernel(x_ref, w_ref, r_ref, o_ref):
    o_ref[...] = r_ref[...] + jnp.dot(x_ref[...], w_ref[...].astype(BF16),
                                      preferred_element_type=F32)


def _matmul_residual(x, w, layer, res, tm=PROJ_TM, tn=PROJ_TN):
    m, k = x.shape
    n = w.shape[2]
    assert m % tm == 0 and n % tn == 0 and w.shape[1] == k, (m, k, n)
    return pl.pallas_call(
        _matmul_residual_kernel,
        grid=(m // tm, n // tn),
        in_specs=[pl.BlockSpec((tm, k), lambda i, j: (i, 0)),
                  pl.BlockSpec((None, k, tn), lambda i, j: (layer, 0, j)),
                  pl.BlockSpec((tm, tn), lambda i, j: (i, j))],
        out_specs=pl.BlockSpec((tm, tn), lambda i, j: (i, j)),
        out_shape=jax.ShapeDtypeStruct((m, n), F32),
        compiler_params=_params(("parallel", "parallel")),
        name="out_proj",
    )(x, w, res)


def _gmlp_kernel(u_ref, v_ref, z_ref, gv_ref, w_ref, bias_ref, go_ref, o_ref,
                 y_ref):
    tm = u_ref.shape[0]
    t_idx = lax.broadcasted_iota(jnp.int32, (CHUNK, CHUNK), 0)
    s_idx = lax.broadcasted_iota(jnp.int32, (CHUNK, CHUNK), 1)
    causal = s_idx <= t_idx

    def chunk(c, carry):
        rows = pl.ds(pl.multiple_of(c * CHUNK, CHUNK), CHUNK)
        ssq = jnp.zeros((CHUNK, 1), F32)
        for g in range(A_GROUPS):
            cols = slice(g * A_CH, (g + 1) * A_CH)
            vg = _gelu(v_ref[rows, cols].astype(F32))
            vn = vg * _rms_scale(vg, A_CH) * gv_ref[:, cols]
            w = jnp.where(causal, w_ref[g], 0.0).astype(BF16)
            sv = jnp.dot(w, vn.astype(BF16), preferred_element_type=F32)
            sv = sv + bias_ref[:, cols]
            y = _gelu(u_ref[rows, cols].astype(F32)) * sv
            ssq = ssq + jnp.sum(y * y, axis=-1, keepdims=True)
            y_ref[:, cols] = y
        r = lax.rsqrt(ssq * (1.0 / A_WIDTH) + EPS)
        z = z_ref[rows, :].astype(F32)
        o_ref[rows, :] = (y_ref[...] * r * go_ref[...] * _silu(z)).astype(o_ref.dtype)
        return carry

    lax.fori_loop(0, tm // CHUNK, chunk, 0)


def _gmlp(proj, g_v, w_s, bias, g_out, layer, tm=MIXER_TM):
    m = proj.shape[0]
    assert m % tm == 0 and tm % CHUNK == 0, (m, tm)
    col = lambda off: (lambda i: (i, off // A_WIDTH))
    g_v, g_v_spec = _row_param(g_v, layer)
    w_s, w_s_spec = _layer_param(w_s, layer)
    bias, bias_spec = _layer_param(bias, layer)
    g_out, g_out_spec = _row_param(g_out, layer, 0, A_WIDTH)
    return pl.pallas_call(
        _gmlp_kernel,
        grid=(m // tm,),
        in_specs=[pl.BlockSpec((tm, A_WIDTH), col(OFF_U)),
                  pl.BlockSpec((tm, A_WIDTH), col(OFF_V)),
                  pl.BlockSpec((tm, A_WIDTH), col(OFF_ZA)),
                  g_v_spec, w_s_spec, bias_spec, g_out_spec],
        out_specs=pl.BlockSpec((tm, A_WIDTH), lambda i: (i, 0)),
        out_shape=jax.ShapeDtypeStruct((m, A_WIDTH), BF16),
        scratch_shapes=[pltpu.VMEM((CHUNK, A_WIDTH), F32)],
        compiler_params=_params(("parallel",)),
        name="gmlp",
    )(proj, proj, proj, g_v, w_s, bias, g_out)


SB_T = 128
SB_SUB = 128
SB_NSUB = 3
SB_W = SB_NSUB * SB_SUB
SB_HG = 6
SB_STAGE = 6
SB_ZERO_WEIGHT_LOG = -105.0


def _sb_kernel(q_ref, k_ref, v_ref, o_ref, acc_ref, carry_ref):
    t, w, sub = SB_T, SB_W, SB_SUB
    i = pl.program_id(2)
    scale = HEAD_DIM ** -0.5
    q_pos = i * t + lax.broadcasted_iota(jnp.int32, (t, w), 0)
    k_off = lax.broadcasted_iota(jnp.int32, (t, w), 1)
    row = lax.broadcasted_iota(jnp.int32, (2 * sub, sub), 0) % sub
    col = lax.broadcasted_iota(jnp.int32, (2 * sub, sub), 1)
    neg_suffix = jnp.where(row > col, -1.0, 0.0).astype(BF16)

    acc_ref[...] = jnp.zeros_like(acc_ref)
    carry_ref[...] = jnp.zeros_like(carry_ref)

    def window(start, bound):
        valid = (start + k_off) < jnp.minimum(q_pos, bound)
        ks = pl.ds(pl.multiple_of(start, sub), w)
        worst = None
        for h0 in range(0, SB_HG, SB_STAGE):
            worst = stage(range(h0, h0 + SB_STAGE), valid, ks, worst)
        return jnp.max(worst)

    def stage(heads, valid, ks, worst):
        log_beta, softplus, pieces = {}, {}, []
        for h in heads:
            hc = slice(h * HEAD_DIM, (h + 1) * HEAD_DIM)
            z = lax.dot_general(q_ref[0, :, hc], k_ref[0, ks, hc],
                                (((1,), (1,)), ((), ())),
                                preferred_element_type=F32) * scale
            sp = jnp.maximum(z, 0.0) + jnp.log(1.0 + jnp.exp(-jnp.abs(z)))
            log_beta[h] = z - sp
            sp = jnp.where(valid, sp, 0.0)
            softplus[h] = sp
            hi = sp.astype(BF16)
            lo = (sp - hi.astype(F32)).astype(BF16)
            for c in range(SB_NSUB):
                pieces.append(jnp.concatenate(
                    [hi[:, c * sub:(c + 1) * sub], lo[:, c * sub:(c + 1) * sub]],
                    axis=1))
        local = jnp.dot(jnp.concatenate(pieces, axis=0), neg_suffix,
                        preferred_element_type=F32)
        for n, h in enumerate(heads):
            hc = slice(h * HEAD_DIM, (h + 1) * HEAD_DIM)
            sp = softplus[h]
            running = carry_ref[h]
            after = [None] * SB_NSUB
            for c in reversed(range(SB_NSUB)):
                base = (n * SB_NSUB + c) * t
                after[c] = local[base:base + t] + running
                running = running - jnp.sum(sp[:, c * sub:(c + 1) * sub],
                                            axis=-1, keepdims=True)
            a = jnp.where(valid,
                          jnp.exp(log_beta[h] + jnp.concatenate(after, axis=1)),
                          0.0)
            acc_ref[h] += jnp.dot(a.astype(BF16), v_ref[0, ks, hc],
                                  preferred_element_type=F32)
            carry_ref[h] = running
            worst = running if worst is None else jnp.maximum(worst, running)
        return worst

    first = jnp.maximum(i + 1 - SB_NSUB, 0) * sub
    worst = window(first, jnp.int32(2 ** 30))

    def more(state):
        start, worst = state
        return jnp.logical_and(start > 0, worst > SB_ZERO_WEIGHT_LOG)

    def step(state):
        start, _ = state
        nxt = jnp.maximum(start - w, 0)
        return nxt, window(nxt, start)

    lax.while_loop(more, step, (first, worst))
    for h in range(SB_HG):
        o_ref[0, :, h * HEAD_DIM:(h + 1) * HEAD_DIM] = acc_ref[h].astype(o_ref.dtype)


def _stick_breaking(proj3):
    b, s, _ = proj3.shape
    assert s % SB_T == 0 and s >= SB_W and SB_T == SB_SUB and HEADS % SB_HG == 0
    t, gw = SB_T, SB_HG * HEAD_DIM
    blk = lambda off: off // gw
    return pl.pallas_call(
        _sb_kernel,
        grid=(b, HEADS // SB_HG, s // t),
        in_specs=[
            pl.BlockSpec((1, t, gw), lambda bi, g, i: (bi, i, blk(OFF_QB) + g)),
            pl.BlockSpec((1, s, gw), lambda bi, g, i: (bi, 0, blk(OFF_KB) + g)),
            pl.BlockSpec((1, s, gw), lambda bi, g, i: (bi, 0, blk(OFF_VB) + g)),
        ],
        out_specs=pl.BlockSpec((1, t, gw), lambda bi, g, i: (bi, i, g)),
        out_shape=jax.ShapeDtypeStruct((b, s, B_WIDTH), BF16),
        scratch_shapes=[pltpu.VMEM((SB_HG, t, HEAD_DIM), F32),
                        pltpu.VMEM((SB_HG, t, 1), F32)],
        compiler_params=_params(("parallel", "parallel", "arbitrary")),
        name="stick_breaking",
    )(proj3, proj3, proj3)


def _rope_tile(x, cos_ref, sin_ref):
    return x * cos_ref[...] + pltpu.roll(x, ROPE_X2, axis=1) * sin_ref[...]


def _mla_q_kernel(cq_ref, g_ref, w_ref, cos_ref, sin_ref, o_ref):
    scale = (C_NOPE + C_ROPE) ** -0.5 * math.log2(math.e)
    x = cq_ref[...].astype(F32)
    xn = (x * _rms_scale(x, Q_LORA) * g_ref[...]).astype(BF16)
    q = jnp.dot(xn, w_ref[...], preferred_element_type=F32)
    for h in range(HEADS):
        lo = h * C_QK
        o_ref[:, lo:lo + C_NOPE] = (q[:, lo:lo + C_NOPE] * scale).astype(o_ref.dtype)
        rope = _rope_tile(q[:, lo + C_NOPE:lo + C_QK], cos_ref, sin_ref)
        o_ref[:, lo + C_NOPE:lo + C_QK] = (rope * scale).astype(o_ref.dtype)


VT_ONES = 16
VT_ROWS = HEAD_DIM + VT_ONES


def _mla_kv_kernel(ckv_ref, kr_ref, g_ref, wk_ref, wvt_ref, cos_ref, sin_ref,
                   k_ref, vt_ref):
    tm = ckv_ref.shape[0]
    x = ckv_ref[...].astype(F32)
    xn = (x * _rms_scale(x, KV_LORA) * g_ref[...]).astype(BF16)
    kn = jnp.dot(xn, wk_ref[...], preferred_element_type=F32).astype(k_ref.dtype)
    kr = _rope_tile(kr_ref[...].astype(F32), cos_ref, sin_ref).astype(k_ref.dtype)
    for h in range(HEADS):
        k_ref[:, h * C_QK:h * C_QK + C_NOPE] = kn[:, h * C_NOPE:(h + 1) * C_NOPE]
        k_ref[:, h * C_QK + C_NOPE:(h + 1) * C_QK] = kr
    vt = lax.dot_general(wvt_ref[...], xn, _NT, preferred_element_type=F32)
    for h in range(HEADS):
        vt_ref[h * VT_ROWS:h * VT_ROWS + HEAD_DIM, :] = (
            vt[h * HEAD_DIM:(h + 1) * HEAD_DIM, :].astype(vt_ref.dtype))
        vt_ref[h * VT_ROWS + HEAD_DIM:(h + 1) * VT_ROWS, :] = jnp.ones(
            (VT_ONES, tm), vt_ref.dtype)


def _mla_project(proj, g_q, g_kv, w_uq_p, w_k_p, w_vt_p, tabs, layer, b, s,
                 tm=MIXER_TM):
    m = proj.shape[0]
    assert m == b * s and s % tm == 0, (m, b, s, tm)
    tiles = s // tm
    tab_spec = pl.BlockSpec((tm, ROPE_TILE), lambda i: (i, 0))
    g_q, g_q_spec = _row_param(g_q, layer)
    g_kv, g_kv_spec = _row_param(g_kv, layer)
    w_uq_p, w_uq_spec = _layer_param(w_uq_p, layer)
    w_k_p, w_k_spec = _layer_param(w_k_p, layer)
    w_vt_p, w_vt_spec = _layer_param(w_vt_p, layer)
    q_cat = pl.pallas_call(
        _mla_q_kernel,
        grid=(m // tm,),
        in_specs=[pl.BlockSpec((tm, Q_LORA), lambda i: (i, TAIL_CQ // Q_LORA)),
                  g_q_spec, w_uq_spec, tab_spec, tab_spec],
        out_specs=pl.BlockSpec((tm, HEADS * C_QK), lambda i: (i, 0)),
        out_shape=jax.ShapeDtypeStruct((m, HEADS * C_QK), BF16),
        compiler_params=_params(("parallel",)),
        name="mla_q",
    )(proj, g_q, w_uq_p, *tabs)
    k_cat, vt = pl.pallas_call(
        _mla_kv_kernel,
        grid=(m // tm,),
        in_specs=[pl.BlockSpec((tm, KV_LORA), lambda i: (i, TAIL_CKV // KV_LORA)),
                  pl.BlockSpec((tm, ROPE_TILE), lambda i: (i, TAIL_KR // ROPE_TILE)),
                  g_kv_spec, w_k_spec, w_vt_spec, tab_spec, tab_spec],
        out_specs=[pl.BlockSpec((tm, HEADS * C_QK), lambda i: (i, 0)),
                   pl.BlockSpec((None, HEADS * VT_ROWS, tm),
                                lambda i: (i // tiles, 0, i % tiles))],
        out_shape=[jax.ShapeDtypeStruct((m, HEADS * C_QK), BF16),
                   jax.ShapeDtypeStruct((b, HEADS * VT_ROWS, s), BF16)],
        compiler_params=_params(("parallel",)),
        name="mla_kv",
    )(proj, proj, g_kv, w_k_p, w_vt_p, *tabs)
    return q_cat, k_cat, vt


MLA_T = 256
MLA_W = 512
MLA_HG = 6


def _mla_attn_kernel(q_ref, k_ref, vt_ref, o_ref, m_ref, acc_ref, s_ref,
                     smax_ref):
    t, w = MLA_T, MLA_W
    i = pl.program_id(2)
    k_off = lax.broadcasted_iota(jnp.int32, (w, t), 0)
    q_pos = i * t + lax.broadcasted_iota(jnp.int32, (w, t), 1)

    m_ref[...] = jnp.full_like(m_ref, -jnp.inf)
    acc_ref[...] = jnp.zeros_like(acc_ref)

    def scores(j, slot):
        ks = pl.ds(pl.multiple_of(j * w, w), w)
        for h in range(MLA_HG):
            s = lax.dot_general(k_ref[0, ks, h * C_QK:(h + 1) * C_QK],
                                q_ref[0, :, h * C_QK:(h + 1) * C_QK], _NT,
                                preferred_element_type=F32)
            s_ref[slot, h] = s
            smax_ref[slot, h] = jnp.max(s, axis=0, keepdims=True)

    def softmax_pv(j, slot, masked):
        start = pl.multiple_of(j * w, w)
        ks = pl.ds(start, w)
        for h in range(MLA_HG):
            s = s_ref[slot, h]
            if masked:
                s = jnp.where((start + k_off) <= q_pos, s, -jnp.inf)
                s_max = jnp.max(s, axis=0, keepdims=True)
            else:
                s_max = smax_ref[slot, h]
            m_old = m_ref[h]
            m_new = jnp.maximum(m_old, s_max)
            alpha = jnp.exp2(m_old - m_new)
            p = jnp.exp2(s - m_new)
            acc_ref[h] = alpha * acc_ref[h] + jnp.dot(
                vt_ref[0, h * VT_ROWS:(h + 1) * VT_ROWS, ks], p.astype(BF16),
                preferred_element_type=F32)
            m_ref[h] = m_new

    last = (i * t) // w

    scores(0, 0)

    def pair(jj, carry):
        j = 2 * jj
        scores(j + 1, 1)
        softmax_pv(j, 0, False)
        scores(j + 2, 0)
        softmax_pv(j + 1, 1, False)
        return carry

    lax.fori_loop(0, last // 2, pair, 0)

    @pl.when(last % 2 == 0)
    def _():
        softmax_pv(last, 0, True)

    @pl.when(last % 2 == 1)
    def _():
        scores(last, 1)
        softmax_pv(last - 1, 0, False)
        softmax_pv(last, 1, True)
    for h in range(MLA_HG):
        acc = acc_ref[h]
        out_t = acc[:HEAD_DIM] / acc[HEAD_DIM:HEAD_DIM + 1]
        o_ref[0, :, h * HEAD_DIM:(h + 1) * HEAD_DIM] = out_t.T.astype(o_ref.dtype)


def _mla_attention(q_cat, k_cat, vt, b, s):
    t, hg = MLA_T, MLA_HG
    assert MLA_W % t == 0 and s % MLA_W == 0 and HEADS % hg == 0
    q3 = q_cat.reshape(b, s, HEADS * C_QK)
    k3 = k_cat.reshape(b, s, HEADS * C_QK)
    return pl.pallas_call(
        _mla_attn_kernel,
        grid=(b, HEADS // hg, s // t),
        in_specs=[pl.BlockSpec((1, t, hg * C_QK), lambda bi, g, i: (bi, i, g)),
                  pl.BlockSpec((1, s, hg * C_QK), lambda bi, g, i: (bi, 0, g)),
                  pl.BlockSpec((1, hg * VT_ROWS, s), lambda bi, g, i: (bi, g, 0))],
        out_specs=pl.BlockSpec((1, t, hg * HEAD_DIM), lambda bi, g, i: (bi, i, g)),
        out_shape=jax.ShapeDtypeStruct((b, s, C_WIDTH), BF16),
        scratch_shapes=[pltpu.VMEM((hg, 1, t), F32),
                        pltpu.VMEM((hg, VT_ROWS, t), F32),
                        pltpu.VMEM((2, hg, MLA_W, t), F32),
                        pltpu.VMEM((2, hg, 1, t), F32)],
        compiler_params=_params(("parallel", "parallel", "arbitrary")),
        name="mla_attention",
    )(q3, k3, vt)


GATE_ROWS = 32


def _gate_kernel(ya_ref, yb_ref, zb_ref, yc_ref, zc_ref, g_ref, o_ref):
    o_ref[:, :A_WIDTH] = ya_ref[...]

    def chunk(c, carry):
        rows = pl.ds(pl.multiple_of(c * GATE_ROWS, GATE_ROWS), GATE_ROWS)
        lo = A_WIDTH
        for y_ref, z_ref, width in ((yb_ref, zb_ref, B_WIDTH),
                                    (yc_ref, zc_ref, C_WIDTH)):
            y = y_ref[rows, :].astype(F32)
            z = z_ref[rows, :].astype(F32)
            g = g_ref[:, lo:lo + width]
            o_ref[rows, lo:lo + width] = (y * _rms_scale(y, width) * g
                                          * _silu(z)).astype(o_ref.dtype)
            lo += width
        return carry

    lax.fori_loop(0, ya_ref.shape[0] // GATE_ROWS, chunk, 0)


def _gate(ya, yb, yc, main, tail, g_out, layer, tm=MIXER_TM):
    m = ya.shape[0]
    assert m % tm == 0 and tm % GATE_ROWS == 0, (m, tm)
    g_out, g_out_spec = _row_param(g_out, layer)
    return pl.pallas_call(
        _gate_kernel,
        grid=(m // tm,),
        in_specs=[pl.BlockSpec((tm, A_WIDTH), lambda i: (i, 0)),
                  pl.BlockSpec((tm, B_WIDTH), lambda i: (i, 0)),
                  pl.BlockSpec((tm, B_WIDTH), lambda i: (i, OFF_ZB // B_WIDTH)),
                  pl.BlockSpec((tm, C_WIDTH), lambda i: (i, 0)),
                  pl.BlockSpec((tm, C_WIDTH), lambda i: (i, TAIL_ZC // C_WIDTH)),
                  g_out_spec],
        out_specs=pl.BlockSpec((tm, D_MODEL), lambda i: (i, 0)),
        out_shape=jax.ShapeDtypeStruct((m, D_MODEL), BF16),
        compiler_params=_params(("parallel",)),
        name="gate",
    )(ya, yb, main, yc, tail, g_out)


def _prep_w_uq(w):
    depth = w.shape[0]
    w = w.reshape(depth, Q_LORA, HEADS, C_NOPE + C_ROPE)
    gap = jnp.zeros((depth, Q_LORA, HEADS, ROPE_X2 - ROPE_HALF), w.dtype)
    w = jnp.concatenate([w[..., :C_NOPE + ROPE_HALF], gap,
                         w[..., C_NOPE + ROPE_HALF:], gap], axis=3)
    return w.reshape(depth, Q_LORA, HEADS * C_QK).astype(BF16)


def _prep_w_ukv(w):
    depth = w.shape[0]
    w = w.reshape(depth, KV_LORA, HEADS, C_NOPE + HEAD_DIM)
    wk = w[..., :C_NOPE].reshape(depth, KV_LORA, HEADS * C_NOPE)
    wv_t = jnp.swapaxes(w[..., C_NOPE:].reshape(depth, KV_LORA, C_WIDTH), 1, 2)
    return wk.astype(BF16), wv_t.astype(BF16)


def _rope_tables(positions):
    inv_freq = 1.0 / (ROPE_THETA ** (jnp.arange(0, C_ROPE, 2, dtype=F32) / C_ROPE))
    ang = positions.astype(F32).reshape(-1, 1) * inv_freq
    cos, sin = jnp.cos(ang), jnp.sin(ang)
    gap = jnp.zeros((ang.shape[0], ROPE_X2 - ROPE_HALF), F32)
    return (jnp.concatenate([cos, gap, cos, gap], axis=1),
            jnp.concatenate([-sin, gap, sin, gap], axis=1))


def kernel(x, positions, g_pre, w_in, a_g_v, a_w_s, a_b_s, c_g_q, c_g_kv,
           c_w_uq, c_w_ukv, g_out, w_out, g_final):
    b, s, d = x.shape
    m = b * s
    depth = w_in.shape[0]
    tabs = _rope_tables(positions)
    w_in_t = jnp.swapaxes(w_in, 1, 2)
    w_uq_p = _prep_w_uq(c_w_uq)
    w_k_p, w_vt_p = _prep_w_ukv(c_w_ukv)
    a_bias = jnp.repeat(jnp.swapaxes(a_b_s, 1, 2), A_CH, axis=2)
    h = x.reshape(m, d)
    for l in range(depth):
        hn = _rmsnorm(h, g_pre, l, BF16)
        main = _in_proj_main(hn, w_in_t, l)
        tail = _in_proj_tail(hn, w_in_t, l)
        ya = _gmlp(main, a_g_v, a_w_s, a_bias, g_out, l)
        yb = _stick_breaking(main.reshape(b, s, D_MAIN)).reshape(m, B_WIDTH)
        q_cat, k_cat, vt = _mla_project(tail, c_g_q, c_g_kv, w_uq_p, w_k_p,
                                        w_vt_p, tabs, l, b, s)
        yc = _mla_attention(q_cat, k_cat, vt, b, s).reshape(m, C_WIDTH)
        y = _gate(ya, yb, yc, main, tail, g_out, l)
        h = _matmul_residual(y, w_out, l, h)
    return _rmsnorm(h, g_final.reshape(1, d), 0, F32).reshape(b, s, d)
```

```python
import math

import jax
import jax.numpy as jnp
from jax import lax
from jax.experimental import pallas as pl
from jax.experimental.pallas import tpu as pltpu

F32 = jnp.float32
BF16 = jnp.bfloat16

D_MODEL = 4096
A_GROUPS = 8
A_CH = 128
A_WIDTH = A_GROUPS * A_CH
CHUNK = 128
HEADS = 12
HEAD_DIM = 128
B_WIDTH = HEADS * HEAD_DIM
C_NOPE = 128
C_ROPE = 64
C_WIDTH = HEADS * HEAD_DIM
Q_LORA = 768
KV_LORA = 512
ROPE_THETA = 10000.0
EPS = 1e-6

C_QK = 256
ROPE_HALF = C_ROPE // 2
ROPE_TILE = 128
ROPE_X2 = ROPE_TILE // 2

OFF_U = 0
OFF_V = OFF_U + A_WIDTH
OFF_ZA = OFF_V + A_WIDTH
OFF_QB = OFF_ZA + A_WIDTH
OFF_KB = OFF_QB + B_WIDTH
OFF_VB = OFF_KB + B_WIDTH
OFF_ZB = OFF_VB + B_WIDTH
D_MAIN = OFF_ZB + B_WIDTH
REF_CQ = D_MAIN
REF_CKV = REF_CQ + Q_LORA
REF_KR = REF_CKV + KV_LORA
REF_ZC = REF_KR + C_ROPE
TAIL_ZC = 0
TAIL_CKV = TAIL_ZC + C_WIDTH
TAIL_KR = TAIL_CKV + KV_LORA
TAIL_CQ = TAIL_KR + C_QK
D_TAIL = TAIL_CQ + Q_LORA

V7X_VMEM_BYTES = 64 * 1024 * 1024
VMEM_LIMIT_BYTES = V7X_VMEM_BYTES * 7 // 8

NORM_TM = 512
PROJ_TM = 1024
PROJ_TN = 512
MAIN_TN = 768
TAIL_TK = 512
MIXER_TM = 1024


def _params(sem):
    return pltpu.CompilerParams(dimension_semantics=sem,
                                vmem_limit_bytes=VMEM_LIMIT_BYTES)


def _gelu(x):
    c = math.sqrt(2.0 / math.pi)
    return x * (0.5 * (1.0 + jnp.tanh(c * (x + 0.044715 * (x * x * x)))))


def _silu(z):
    return z / (1.0 + jnp.exp(-z))


def _rms_scale(x, width):
    return lax.rsqrt(jnp.sum(x * x, axis=-1, keepdims=True) * (1.0 / width) + EPS)


def _rmsnorm_kernel(x_ref, g_ref, o_ref):
    x = x_ref[...]
    o_ref[...] = (x * _rms_scale(x, x.shape[-1]) * g_ref[...]).astype(o_ref.dtype)


def _row_param(p, layer, lo=0, width=None):
    stacked = p.reshape(p.shape[0], 1, -1)
    width = stacked.shape[2] if width is None else width
    spec = pl.BlockSpec((None, 1, width), lambda *_: (layer, 0, lo // width))
    return stacked, spec


def _layer_param(p, layer):
    zeros = (0,) * (p.ndim - 1)
    return p, pl.BlockSpec((None,) + p.shape[1:], lambda *_: (layer,) + zeros)


def _rmsnorm(x, g, layer, out_dtype, tm=NORM_TM):
    m, d = x.shape
    assert m % tm == 0, (m, tm)
    g, g_spec = _row_param(g, layer)
    return pl.pallas_call(
        _rmsnorm_kernel,
        grid=(m // tm,),
        in_specs=[pl.BlockSpec((tm, d), lambda i: (i, 0)), g_spec],
        out_specs=pl.BlockSpec((tm, d), lambda i: (i, 0)),
        out_shape=jax.ShapeDtypeStruct((m, d), out_dtype),
        compiler_params=_params(("parallel",)),
        name="rmsnorm",
    )(x, g)


_NT = (((1,), (1,)), ((), ()))


def _in_proj_main_kernel(x_ref, wt_ref, o_ref):
    o_ref[...] = lax.dot_general(x_ref[...], wt_ref[...].astype(BF16), _NT,
                                 preferred_element_type=F32).astype(o_ref.dtype)


def _in_proj_main(x, wt, layer, tm=PROJ_TM, tn=MAIN_TN):
    m, k = x.shape
    assert m % tm == 0 and D_MAIN % tn == 0 and wt.shape[2] == k, (m, tm, tn)
    return pl.pallas_call(
        _in_proj_main_kernel,
        grid=(m // tm, D_MAIN // tn),
        in_specs=[pl.BlockSpec((tm, k), lambda i, j: (i, 0)),
                  pl.BlockSpec((None, tn, k), lambda i, j: (layer, j, 0))],
        out_specs=pl.BlockSpec((tm, tn), lambda i, j: (i, j)),
        out_shape=jax.ShapeDtypeStruct((m, D_MAIN), BF16),
        compiler_params=_params(("parallel", "parallel")),
        name="in_proj_main",
    )(x, wt)


TAIL_PART_ROWS = 576
TAIL_PARTS = 5
TAIL_COPIES = (
    (TAIL_ZC, REF_ZC - D_MAIN, C_WIDTH),
    (TAIL_CKV, REF_CKV - D_MAIN, KV_LORA),
    (TAIL_KR, REF_KR - D_MAIN, ROPE_HALF),
    (TAIL_KR + ROPE_X2, REF_KR - D_MAIN + ROPE_HALF, ROPE_HALF),
    (TAIL_CQ, REF_CQ - D_MAIN, Q_LORA),
)
TAIL_ZERO_ROWS = ((TAIL_KR + ROPE_HALF, ROPE_X2 - ROPE_HALF),
                  (TAIL_KR + ROPE_X2 + ROPE_HALF, ROPE_X2 - ROPE_HALF),
                  (TAIL_KR + ROPE_TILE, TAIL_CQ - TAIL_KR - ROPE_TILE))


def _in_proj_tail_kernel(x_ref, *refs):
    w_refs = refs[:TAIL_PARTS]
    o_ref, wp_ref, acc_ref = refs[TAIL_PARTS:]
    kk = pl.program_id(1)

    @pl.when(kk == 0)
    def _():
        acc_ref[...] = jnp.zeros_like(acc_ref)

    for dst, src, rows in TAIL_COPIES:
        done = 0
        while done < rows:
            part, lo = divmod(src + done, TAIL_PART_ROWS)
            n = min(rows - done, TAIL_PART_ROWS - lo)
            wp_ref[dst + done:dst + done + n, :] = (
                w_refs[part][lo:lo + n, :].astype(BF16))
            done += n
    for dst, rows in TAIL_ZERO_ROWS:
        wp_ref[dst:dst + rows, :] = jnp.zeros((rows, wp_ref.shape[1]), BF16)

    acc_ref[...] += lax.dot_general(x_ref[...], wp_ref[...], _NT,
                                    preferred_element_type=F32)

    @pl.when(kk == pl.num_programs(1) - 1)
    def _():
        o_ref[...] = acc_ref[...].astype(o_ref.dtype)


def _in_proj_tail(x, wt, layer, tm=PROJ_TM, tk=TAIL_TK):
    m, k = x.shape
    assert m % tm == 0 and k % tk == 0 and D_MAIN % TAIL_PART_ROWS == 0
    assert wt.shape[1] == D_MAIN + TAIL_PARTS * TAIL_PART_ROWS, wt.shape
    first = D_MAIN // TAIL_PART_ROWS
    w_specs = [pl.BlockSpec((None, TAIL_PART_ROWS, tk),
                            lambda i, kk, p=p: (layer, first + p, kk))
               for p in range(TAIL_PARTS)]
    return pl.pallas_call(
        _in_proj_tail_kernel,
        grid=(m // tm, k // tk),
        in_specs=[pl.BlockSpec((tm, tk), lambda i, kk: (i, kk))] + w_specs,
        out_specs=pl.BlockSpec((tm, D_TAIL), lambda i, kk: (i, 0)),
        out_shape=jax.ShapeDtypeStruct((m, D_TAIL), BF16),
        scratch_shapes=[pltpu.VMEM((D_TAIL, tk), BF16),
                        pltpu.VMEM((tm, D_TAIL), F32)],
        compiler_params=_params(("parallel", "arbitrary")),
        name="in_proj_tail",
    )(x, *([wt] * TAIL_PARTS))


def _matmul_residual_kernel(x_ref, w_ref, r_ref, o_ref):
    o_ref[...] = r_ref[...] + jnp.dot(x_ref[...], w_ref[...].astype(BF16),
                                      preferred_element_type=F32)


def _matmul_residual(x, w, layer, res, tm=PROJ_TM, tn=PROJ_TN):
    m, k = x.shape
    n = w.shape[2]
    assert m % tm == 0 and n % tn == 0 and w.shape[1] == k, (m, k, n)
    return pl.pallas_call(
        _matmul_residual_kernel,
        grid=(m // tm, n // tn),
        in_specs=[pl.BlockSpec((tm, k), lambda i, j: (i, 0)),
                  pl.BlockSpec((None, k, tn), lambda i, j: (layer, 0, j)),
                  pl.BlockSpec((tm, tn), lambda i, j: (i, j))],
        out_specs=pl.BlockSpec((tm, tn), lambda i, j: (i, j)),
        out_shape=jax.ShapeDtypeStruct((m, n), F32),
        compiler_params=_params(("parallel", "parallel")),
        name="out_proj",
    )(x, w, res)


def _gmlp_kernel(u_ref, v_ref, z_ref, gv_ref, w_ref, bias_ref, go_ref, o_ref,
                 y_ref):
    tm = u_ref.shape[0]
    t_idx = lax.broadcasted_iota(jnp.int32, (CHUNK, CHUNK), 0)
    s_idx = lax.broadcasted_iota(jnp.int32, (CHUNK, CHUNK), 1)
    causal = s_idx <= t_idx

    def chunk(c, carry):
        rows = pl.ds(pl.multiple_of(c * CHUNK, CHUNK), CHUNK)
        ssq = jnp.zeros((CHUNK, 1), F32)
        for g in range(A_GROUPS):
            cols = slice(g * A_CH, (g + 1) * A_CH)
            vg = _gelu(v_ref[rows, cols].astype(F32))
            vn = vg * _rms_scale(vg, A_CH) * gv_ref[:, cols]
            w = jnp.where(causal, w_ref[g], 0.0).astype(BF16)
            sv = jnp.dot(w, vn.astype(BF16), preferred_element_type=F32)
            sv = sv + bias_ref[:, cols]
            y = _gelu(u_ref[rows, cols].astype(F32)) * sv
            ssq = ssq + jnp.sum(y * y, axis=-1, keepdims=True)
            y_ref[:, cols] = y
        r = lax.rsqrt(ssq * (1.0 / A_WIDTH) + EPS)
        z = z_ref[rows, :].astype(F32)
        o_ref[rows, :] = (y_ref[...] * r * go_ref[...] * _silu(z)).astype(o_ref.dtype)
        return carry

    lax.fori_loop(0, tm // CHUNK, chunk, 0)


def _gmlp(proj, g_v, w_s, bias, g_out, layer, tm=MIXER_TM):
    m = proj.shape[0]
    assert m % tm == 0 and tm % CHUNK == 0, (m, tm)
    col = lambda off: (lambda i: (i, off // A_WIDTH))
    g_v, g_v_spec = _row_param(g_v, layer)
    w_s, w_s_spec = _layer_param(w_s, layer)
    bias, bias_spec = _layer_param(bias, layer)
    g_out, g_out_spec = _row_param(g_out, layer, 0, A_WIDTH)
    return pl.pallas_call(
        _gmlp_kernel,
        grid=(m // tm,),
        in_specs=[pl.BlockSpec((tm, A_WIDTH), col(OFF_U)),
                  pl.BlockSpec((tm, A_WIDTH), col(OFF_V)),
                  pl.BlockSpec((tm, A_WIDTH), col(OFF_ZA)),
                  g_v_spec, w_s_spec, bias_spec, g_out_spec],
        out_specs=pl.BlockSpec((tm, A_WIDTH), lambda i: (i, 0)),
        out_shape=jax.ShapeDtypeStruct((m, A_WIDTH), BF16),
        scratch_shapes=[pltpu.VMEM((CHUNK, A_WIDTH), F32)],
        compiler_params=_params(("parallel",)),
        name="gmlp",
    )(proj, proj, proj, g_v, w_s, bias, g_out)


SB_T = 128
SB_SUB = 128
SB_NSUB = 3
SB_W = SB_NSUB * SB_SUB
SB_HG = 6
SB_STAGE = 6
SB_ZERO_WEIGHT_LOG = -105.0


def _sb_kernel(q_ref, k_ref, v_ref, o_ref, acc_ref, carry_ref):
    t, w, sub = SB_T, SB_W, SB_SUB
    i = pl.program_id(2)
    scale = HEAD_DIM ** -0.5
    q_pos = i * t + lax.broadcasted_iota(jnp.int32, (t, w), 0)
    k_off = lax.broadcasted_iota(jnp.int32, (t, w), 1)
    row = lax.broadcasted_iota(jnp.int32, (2 * sub, sub), 0) % sub
    col = lax.broadcasted_iota(jnp.int32, (2 * sub, sub), 1)
    neg_suffix = jnp.where(row > col, -1.0, 0.0).astype(BF16)

    acc_ref[...] = jnp.zeros_like(acc_ref)
    carry_ref[...] = jnp.zeros_like(carry_ref)

    def window(start, bound):
        valid = (start + k_off) < jnp.minimum(q_pos, bound)
        ks = pl.ds(pl.multiple_of(start, sub), w)
        worst = None
        for h0 in range(0, SB_HG, SB_STAGE):
            worst = stage(range(h0, h0 + SB_STAGE), valid, ks, worst)
        return jnp.max(worst)

    def stage(heads, valid, ks, worst):
        log_beta, softplus, pieces = {}, {}, []
        for h in heads:
            hc = slice(h * HEAD_DIM, (h + 1) * HEAD_DIM)
            z = lax.dot_general(q_ref[0, :, hc], k_ref[0, ks, hc],
                                (((1,), (1,)), ((), ())),
                                preferred_element_type=F32) * scale
            sp = jnp.maximum(z, 0.0) + jnp.log(1.0 + jnp.exp(-jnp.abs(z)))
            log_beta[h] = z - sp
            sp = jnp.where(valid, sp, 0.0)
            softplus[h] = sp
            hi = sp.astype(BF16)
            lo = (sp - hi.astype(F32)).astype(BF16)
            for c in range(SB_NSUB):
                pieces.append(jnp.concatenate(
                    [hi[:, c * sub:(c + 1) * sub], lo[:, c * sub:(c + 1) * sub]],
                    axis=1))
        local = jnp.dot(jnp.concatenate(pieces, axis=0), neg_suffix,
                        preferred_element_type=F32)
        for n, h in enumerate(heads):
            hc = slice(h * HEAD_DIM, (h + 1) * HEAD_DIM)
            sp = softplus[h]
            running = carry_ref[h]
            after = [None] * SB_NSUB
            for c in reversed(range(SB_NSUB)):
                base = (n * SB_NSUB + c) * t
                after[c] = local[base:base + t] + running
                running = running - jnp.sum(sp[:, c * sub:(c + 1) * sub],
                                            axis=-1, keepdims=True)
            a = jnp.where(valid,
                          jnp.exp(log_beta[h] + jnp.concatenate(after, axis=1)),
                          0.0)
            acc_ref[h] += jnp.dot(a.astype(BF16), v_ref[0, ks, hc],
                                  preferred_element_type=F32)
            carry_ref[h] = running
            worst = running if worst is None else jnp.maximum(worst, running)
        return worst

    first = jnp.maximum(i + 1 - SB_NSUB, 0) * sub
    worst = window(first, jnp.int32(2 ** 30))

    def more(state):
        start, worst = state
        return jnp.logical_and(start > 0, worst > SB_ZERO_WEIGHT_LOG)

    def step(state):
        start, _ = state
        nxt = jnp.maximum(start - w, 0)
        return nxt, window(nxt, start)

    lax.while_loop(more, step, (first, worst))
    for h in range(SB_HG):
        o_ref[0, :, h * HEAD_DIM:(h + 1) * HEAD_DIM] = acc_ref[h].astype(o_ref.dtype)


def _stick_breaking(proj3):
    b, s, _ = proj3.shape
    assert s % SB_T == 0 and s >= SB_W and SB_T == SB_SUB and HEADS % SB_HG == 0
    t, gw = SB_T, SB_HG * HEAD_DIM
    blk = lambda off: off // gw
    return pl.pallas_call(
        _sb_kernel,
        grid=(b, HEADS // SB_HG, s // t),
        in_specs=[
            pl.BlockSpec((1, t, gw), lambda bi, g, i: (bi, i, blk(OFF_QB) + g)),
            pl.BlockSpec((1, s, gw), lambda bi, g, i: (bi, 0, blk(OFF_KB) + g)),
            pl.BlockSpec((1, s, gw), lambda bi, g, i: (bi, 0, blk(OFF_VB) + g)),
        ],
        out_specs=pl.BlockSpec((1, t, gw), lambda bi, g, i: (bi, i, g)),
        out_shape=jax.ShapeDtypeStruct((b, s, B_WIDTH), BF16),
        scratch_shapes=[pltpu.VMEM((SB_HG, t, HEAD_DIM), F32),
                        pltpu.VMEM((SB_HG, t, 1), F32)],
        compiler_params=_params(("parallel", "parallel", "arbitrary")),
        name="stick_breaking",
    )(proj3, proj3, proj3)


def _rope_tile(x, cos_ref, sin_ref):
    return x * cos_ref[...] + pltpu.roll(x, ROPE_X2, axis=1) * sin_ref[...]


def _mla_q_kernel(cq_ref, g_ref, w_ref, cos_ref, sin_ref, o_ref):
    scale = (C_NOPE + C_ROPE) ** -0.5 * math.log2(math.e)
    x = cq_ref[...].astype(F32)
    xn = (x * _rms_scale(x, Q_LORA) * g_ref[...]).astype(BF16)
    q = jnp.dot(xn, w_ref[...], preferred_element_type=F32)
    for h in range(HEADS):
        lo = h * C_QK
        o_ref[:, lo:lo + C_NOPE] = (q[:, lo:lo + C_NOPE] * scale).astype(o_ref.dtype)
        rope = _rope_tile(q[:, lo + C_NOPE:lo + C_QK], cos_ref, sin_ref)
        o_ref[:, lo + C_NOPE:lo + C_QK] = (rope * scale).astype(o_ref.dtype)


VT_ONES = 16
VT_ROWS = HEAD_DIM + VT_ONES


def _mla_kv_kernel(ckv_ref, kr_ref, g_ref, wk_ref, wvt_ref, cos_ref, sin_ref,
                   k_ref, vt_ref):
    tm = ckv_ref.shape[0]
    x = ckv_ref[...].astype(F32)
    xn = (x * _rms_scale(x, KV_LORA) * g_ref[...]).astype(BF16)
    kn = jnp.dot(xn, wk_ref[...], preferred_element_type=F32).astype(k_ref.dtype)
    kr = _rope_tile(kr_ref[...].astype(F32), cos_ref, sin_ref).astype(k_ref.dtype)
    for h in range(HEADS):
        k_ref[:, h * C_QK:h * C_QK + C_NOPE] = kn[:, h * C_NOPE:(h + 1) * C_NOPE]
        k_ref[:, h * C_QK + C_NOPE:(h + 1) * C_QK] = kr
    vt = lax.dot_general(wvt_ref[...], xn, _NT, preferred_element_type=F32)
    for h in range(HEADS):
        vt_ref[h * VT_ROWS:h * VT_ROWS + HEAD_DIM, :] = (
            vt[h * HEAD_DIM:(h + 1) * HEAD_DIM, :].astype(vt_ref.dtype))
        vt_ref[h * VT_ROWS + HEAD_DIM:(h + 1) * VT_ROWS, :] = jnp.ones(
            (VT_ONES, tm), vt_ref.dtype)


def _mla_project(proj, g_q, g_kv, w_uq_p, w_k_p, w_vt_p, tabs, layer, b, s,
                 tm=MIXER_TM):
    m = proj.shape[0]
    assert m == b * s and s % tm == 0, (m, b, s, tm)
    tiles = s // tm
    tab_spec = pl.BlockSpec((tm, ROPE_TILE), lambda i: (i, 0))
    g_q, g_q_spec = _row_param(g_q, layer)
    g_kv, g_kv_spec = _row_param(g_kv, layer)
    w_uq_p, w_uq_spec = _layer_param(w_uq_p, layer)
    w_k_p, w_k_spec = _layer_param(w_k_p, layer)
    w_vt_p, w_vt_spec = _layer_param(w_vt_p, layer)
    def both(cq_ref, ckv_ref, kr_ref, gq_ref, gkv_ref, wuq_ref, wk_ref, wvt_ref,
             cos_ref, sin_ref, q_ref, k_ref, vt_ref):
        _mla_q_kernel(cq_ref, gq_ref, wuq_ref, cos_ref, sin_ref, q_ref)
        _mla_kv_kernel(ckv_ref, kr_ref, gkv_ref, wk_ref, wvt_ref, cos_ref, sin_ref,
                       k_ref, vt_ref)

    row_out = pl.BlockSpec((tm, HEADS * C_QK), lambda i: (i, 0))
    q_cat, k_cat, vt = pl.pallas_call(
        both,
        grid=(m // tm,),
        in_specs=[pl.BlockSpec((tm, Q_LORA), lambda i: (i, TAIL_CQ // Q_LORA)),
                  pl.BlockSpec((tm, KV_LORA), lambda i: (i, TAIL_CKV // KV_LORA)),
                  pl.BlockSpec((tm, ROPE_TILE), lambda i: (i, TAIL_KR // ROPE_TILE)),
                  g_q_spec, g_kv_spec, w_uq_spec, w_k_spec, w_vt_spec,
                  tab_spec, tab_spec],
        out_specs=[row_out, row_out,
                   pl.BlockSpec((None, HEADS * VT_ROWS, tm),
                                lambda i: (i // tiles, 0, i % tiles))],
        out_shape=[jax.ShapeDtypeStruct((m, HEADS * C_QK), BF16),
                   jax.ShapeDtypeStruct((m, HEADS * C_QK), BF16),
                   jax.ShapeDtypeStruct((b, HEADS * VT_ROWS, s), BF16)],
        compiler_params=_params(("parallel",)),
        name="mla_qkv",
    )(proj, proj, proj, g_q, g_kv, w_uq_p, w_k_p, w_vt_p, *tabs)
    return q_cat, k_cat, vt


MLA_T = 256
MLA_W = 512
MLA_HG = 6


def _mla_attn_kernel(q_ref, k_ref, vt_ref, o_ref, m_ref, acc_ref, s_ref,
                     smax_ref):
    t, w = MLA_T, MLA_W
    i = pl.program_id(2)
    k_off = lax.broadcasted_iota(jnp.int32, (w, t), 0)
    q_pos = i * t + lax.broadcasted_iota(jnp.int32, (w, t), 1)

    m_ref[...] = jnp.full_like(m_ref, -jnp.inf)
    acc_ref[...] = jnp.zeros_like(acc_ref)

    def scores(j, slot):
        ks = pl.ds(pl.multiple_of(j * w, w), w)
        for h in range(MLA_HG):
            s = lax.dot_general(k_ref[0, ks, h * C_QK:(h + 1) * C_QK],
                                q_ref[0, :, h * C_QK:(h + 1) * C_QK], _NT,
                                preferred_element_type=F32)
            s_ref[slot, h] = s
            smax_ref[slot, h] = jnp.max(s, axis=0, keepdims=True)

    def softmax_pv(j, slot, masked):
        start = pl.multiple_of(j * w, w)
        ks = pl.ds(start, w)
        for h in range(MLA_HG):
            s = s_ref[slot, h]
            if masked:
                s = jnp.where((start + k_off) <= q_pos, s, -jnp.inf)
                s_max = jnp.max(s, axis=0, keepdims=True)
            else:
                s_max = smax_ref[slot, h]
            m_old = m_ref[h]
            m_new = jnp.maximum(m_old, s_max)
            alpha = jnp.exp2(m_old - m_new)
            p = jnp.exp2(s - m_new)
            acc_ref[h] = alpha * acc_ref[h] + jnp.dot(
                vt_ref[0, h * VT_ROWS:(h + 1) * VT_ROWS, ks], p.astype(BF16),
                preferred_element_type=F32)
            m_ref[h] = m_new

    last = (i * t) // w

    scores(0, 0)

    def pair(jj, carry):
        j = 2 * jj
        scores(j + 1, 1)
        softmax_pv(j, 0, False)
        scores(j + 2, 0)
        softmax_pv(j + 1, 1, False)
        return carry

    lax.fori_loop(0, last // 2, pair, 0)

    @pl.when(last % 2 == 0)
    def _():
        softmax_pv(last, 0, True)

    @pl.when(last % 2 == 1)
    def _():
        scores(last, 1)
        softmax_pv(last - 1, 0, False)
        softmax_pv(last, 1, True)
    for h in range(MLA_HG):
        acc = acc_ref[h]
        out_t = acc[:HEAD_DIM] / acc[HEAD_DIM:HEAD_DIM + 1]
        o_ref[0, :, h * HEAD_DIM:(h + 1) * HEAD_DIM] = out_t.T.astype(o_ref.dtype)


def _mla_attention(q_cat, k_cat, vt, b, s):
    t, hg = MLA_T, MLA_HG
    assert MLA_W % t == 0 and s % MLA_W == 0 and HEADS % hg == 0
    q3 = q_cat.reshape(b, s, HEADS * C_QK)
    k3 = k_cat.reshape(b, s, HEADS * C_QK)
    return pl.pallas_call(
        _mla_attn_kernel,
        grid=(b, HEADS // hg, s // t),
        in_specs=[pl.BlockSpec((1, t, hg * C_QK), lambda bi, g, i: (bi, i, g)),
                  pl.BlockSpec((1, s, hg * C_QK), lambda bi, g, i: (bi, 0, g)),
                  pl.BlockSpec((1, hg * VT_ROWS, s), lambda bi, g, i: (bi, g, 0))],
        out_specs=pl.BlockSpec((1, t, hg * HEAD_DIM), lambda bi, g, i: (bi, i, g)),
        out_shape=jax.ShapeDtypeStruct((b, s, C_WIDTH), BF16),
        scratch_shapes=[pltpu.VMEM((hg, 1, t), F32),
                        pltpu.VMEM((hg, VT_ROWS, t), F32),
                        pltpu.VMEM((2, hg, MLA_W, t), F32),
                        pltpu.VMEM((2, hg, 1, t), F32)],
        compiler_params=_params(("parallel", "parallel", "arbitrary")),
        name="mla_attention",
    )(q3, k3, vt)


GATE_ROWS = 32


def _gate_kernel(ya_ref, yb_ref, zb_ref, yc_ref, zc_ref, g_ref, o_ref):
    o_ref[:, :A_WIDTH] = ya_ref[...]

    def chunk(c, carry):
        rows = pl.ds(pl.multiple_of(c * GATE_ROWS, GATE_ROWS), GATE_ROWS)
        lo = A_WIDTH
        for y_ref, z_ref, width in ((yb_ref, zb_ref, B_WIDTH),
                                    (yc_ref, zc_ref, C_WIDTH)):
            y = y_ref[rows, :].astype(F32)
            z = z_ref[rows, :].astype(F32)
            g = g_ref[:, lo:lo + width]
            o_ref[rows, lo:lo + width] = (y * _rms_scale(y, width) * g
                                          * _silu(z)).astype(o_ref.dtype)
            lo += width
        return carry

    lax.fori_loop(0, ya_ref.shape[0] // GATE_ROWS, chunk, 0)


def _gate(ya, yb, yc, main, tail, g_out, layer, tm=MIXER_TM):
    m = ya.shape[0]
    assert m % tm == 0 and tm % GATE_ROWS == 0, (m, tm)
    g_out, g_out_spec = _row_param(g_out, layer)
    return pl.pallas_call(
        _gate_kernel,
        grid=(m // tm,),
        in_specs=[pl.BlockSpec((tm, A_WIDTH), lambda i: (i, 0)),
                  pl.BlockSpec((tm, B_WIDTH), lambda i: (i, 0)),
                  pl.BlockSpec((tm, B_WIDTH), lambda i: (i, OFF_ZB // B_WIDTH)),
                  pl.BlockSpec((tm, C_WIDTH), lambda i: (i, 0)),
                  pl.BlockSpec((tm, C_WIDTH), lambda i: (i, TAIL_ZC // C_WIDTH)),
                  g_out_spec],
        out_specs=pl.BlockSpec((tm, D_MODEL), lambda i: (i, 0)),
        out_shape=jax.ShapeDtypeStruct((m, D_MODEL), BF16),
        compiler_params=_params(("parallel",)),
        name="gate",
    )(ya, yb, main, yc, tail, g_out)


def _prep_w_uq(w):
    depth = w.shape[0]
    w = w.reshape(depth, Q_LORA, HEADS, C_NOPE + C_ROPE)
    gap = jnp.zeros((depth, Q_LORA, HEADS, ROPE_X2 - ROPE_HALF), w.dtype)
    w = jnp.concatenate([w[..., :C_NOPE + ROPE_HALF], gap,
                         w[..., C_NOPE + ROPE_HALF:], gap], axis=3)
    return w.reshape(depth, Q_LORA, HEADS * C_QK).astype(BF16)


def _prep_w_ukv(w):
    depth = w.shape[0]
    w = w.reshape(depth, KV_LORA, HEADS, C_NOPE + HEAD_DIM)
    wk = w[..., :C_NOPE].reshape(depth, KV_LORA, HEADS * C_NOPE)
    wv_t = jnp.swapaxes(w[..., C_NOPE:].reshape(depth, KV_LORA, C_WIDTH), 1, 2)
    return wk.astype(BF16), wv_t.astype(BF16)


def _rope_tables(positions):
    inv_freq = 1.0 / (ROPE_THETA ** (jnp.arange(0, C_ROPE, 2, dtype=F32) / C_ROPE))
    ang = positions.astype(F32).reshape(-1, 1) * inv_freq
    cos, sin = jnp.cos(ang), jnp.sin(ang)
    gap = jnp.zeros((ang.shape[0], ROPE_X2 - ROPE_HALF), F32)
    return (jnp.concatenate([cos, gap, cos, gap], axis=1),
            jnp.concatenate([-sin, gap, sin, gap], axis=1))


def kernel(x, positions, g_pre, w_in, a_g_v, a_w_s, a_b_s, c_g_q, c_g_kv,
           c_w_uq, c_w_ukv, g_out, w_out, g_final):
    b, s, d = x.shape
    m = b * s
    depth = w_in.shape[0]
    tabs = _rope_tables(positions)
    w_in_t = jnp.swapaxes(w_in, 1, 2)
    w_uq_p = _prep_w_uq(c_w_uq)
    w_k_p, w_vt_p = _prep_w_ukv(c_w_ukv)
    a_bias = jnp.repeat(jnp.swapaxes(a_b_s, 1, 2), A_CH, axis=2)
    h = x.reshape(m, d)
    for l in range(depth):
        hn = _rmsnorm(h, g_pre, l, BF16)
        main = _in_proj_main(hn, w_in_t, l)
        tail = _in_proj_tail(hn, w_in_t, l)
        ya = _gmlp(main, a_g_v, a_w_s, a_bias, g_out, l)
        yb = _stick_breaking(main.reshape(b, s, D_MAIN)).reshape(m, B_WIDTH)
        q_cat, k_cat, vt = _mla_project(tail, c_g_q, c_g_kv, w_uq_p, w_k_p,
                                        w_vt_p, tabs, l, b, s)
        yc = _mla_attention(q_cat, k_cat, vt, b, s).reshape(m, C_WIDTH)
        y = _gate(ya, yb, yc, main, tail, g_out, l)
        h = _matmul_residual(y, w_out, l, h)
    return _rmsnorm(h, g_final.reshape(1, d), 0, F32).reshape(b, s, d)
```
